```python
import math
import jax, jax.numpy as jnp
from jax import lax
import numpy as np

D_MODEL = 2048
BATCH = 2
SEQ = 4096
DEPTH = 1

HG_HEADS = 8
HG_DK = 128
HG_DV = 128
HG_CHUNK = 64
ATT_HEADS = 8
ATT_KV_HEADS = 2
HEAD_DIM = 128
WINDOW = 128
ATT_BLOCK = 128
ROPE_THETA = 10000.0
N_EXPERTS = 16
EXPERT_FF = 1024
CAPACITY_FACTOR = 2
NORM_EPS = 1e-6

HG_QK = HG_HEADS * HG_DK
HG_V = HG_HEADS * HG_DV
ATT_Q = ATT_HEADS * HEAD_DIM
ATT_KV = ATT_KV_HEADS * HEAD_DIM
IN_SPLITS = (HG_QK, HG_QK, HG_QK, HG_V, HG_V, ATT_Q, ATT_KV, ATT_KV, D_MODEL, D_MODEL)
IN_WIDTH = 5 * 1024 + 1024 + 256 + 256 + 2 * D_MODEL

kernel_name = "hybrid_hgrn2_swa_ecmoe_block"


def rmsnorm(x, g):
    xf = x.astype(jnp.float32)
    y = xf * lax.rsqrt(jnp.mean(xf * xf, axis=-1, keepdims=True) + NORM_EPS)
    return y.astype(x.dtype) * g


def to_heads(t, n):
    b, s, _ = t.shape
    return t.reshape(b, s, n, -1).transpose(0, 2, 1, 3)


def hgrn2_scan(q, k, v, log_f):
    B, H, S, DK = q.shape
    DV = v.shape[-1]
    L = HG_CHUNK
    N = S // L
    rs = lambda t: t.astype(jnp.float32).reshape(B, H, N, L, t.shape[-1])
    q, k, v, log_f = rs(q), rs(k), rs(v), rs(log_f)
    b = jnp.cumsum(log_f, axis=3)
    b_ref = b[..., L // 2:L // 2 + 1, :]
    q_in = q * jnp.exp(b - b_ref)
    k_in = k * jnp.exp(b_ref - b)
    causal_in_chunk = jnp.tril(jnp.ones((L, L), dtype=bool))
    a = jnp.einsum('bhnld,bhnmd->bhnlm', q_in, k_in)
    a = jnp.where(causal_in_chunk, a, 0.0)
    o_intra = jnp.einsum('bhnlm,bhnmv->bhnlv', a, v)
    b_last = b[..., -1:, :]
    chunk_state = jnp.einsum('bhnld,bhnlv->bhndv', k * jnp.exp(b_last - b), v)
    chunk_decay = jnp.exp(b_last[..., 0, :])

    def step(s_prev, inp):
        dec, cs = inp
        return dec[..., None] * s_prev + cs, s_prev

    _, s_prevs = lax.scan(step, jnp.zeros((B, H, DK, DV), jnp.float32),
                          (jnp.moveaxis(chunk_decay, 2, 0), jnp.moveaxis(chunk_state, 2, 0)))
    s_prevs = jnp.moveaxis(s_prevs, 0, 2)
    o_inter = jnp.einsum('bhnld,bhndv->bhnlv', q * jnp.exp(b), s_prevs)
    return (o_intra + o_inter).reshape(B, H, S, DV)


def rope(x, pos):
    half = x.shape[-1] // 2
    inv = ROPE_THETA ** (-jnp.arange(half, dtype=jnp.float32) / half)
    ang = pos[:, None, :, None].astype(jnp.float32) * inv
    cos, sin = jnp.cos(ang), jnp.sin(ang)
    xf = x.astype(jnp.float32)
    x1, x2 = xf[..., :half], xf[..., half:]
    return jnp.concatenate([x1 * cos - x2 * sin, x1 * sin + x2 * cos], axis=-1).astype(x.dtype)


def window_attention(q, k, v, sink):
    B, HQ, S, Dh = q.shape
    HKV = k.shape[1]
    G = HQ // HKV
    Bk = ATT_BLOCK
    NB = S // Bk
    qb = q.reshape(B, HKV, G, NB, Bk, Dh).astype(jnp.float32)
    pad = ((0, 0), (0, 0), (Bk, Bk), (0, 0))
    kp = jnp.pad(k, pad).reshape(B, HKV, NB + 2, Bk, Dh)
    vp = jnp.pad(v, pad).reshape(B, HKV, NB + 2, Bk, Dh)
    kb = jnp.concatenate([kp[:, :, :-2], kp[:, :, 1:-1], kp[:, :, 2:]], axis=3)
    vb = jnp.concatenate([vp[:, :, :-2], vp[:, :, 1:-1], vp[:, :, 2:]], axis=3)
    s = jnp.einsum('bhgnqd,bhnkd->bhgnqk', qb, kb.astype(jnp.float32)) * (Dh ** -0.5)
    qi = jnp.arange(Bk)[:, None]
    kj = jnp.arange(3 * Bk)[None, :]
    band = jnp.abs(kj - Bk - qi) <= WINDOW
    key_pos = jnp.arange(NB)[:, None] * Bk - Bk + jnp.arange(3 * Bk)[None, :]
    in_range = (key_pos >= 0) & (key_pos < S)
    mask = band[None] & in_range[:, None, :]
    s = jnp.where(mask, s, -jnp.inf)
    sk = sink.astype(jnp.float32).reshape(1, HKV, G, 1, 1, 1)
    m = jnp.maximum(jnp.max(s, axis=-1, keepdims=True), sk)
    p = jnp.exp(s - m)
    denom = jnp.sum(p, axis=-1, keepdims=True) + jnp.exp(sk - m)
    o = jnp.einsum('bhgnqk,bhnkd->bhgnqd', p, vb.astype(jnp.float32)) / denom
    o = o.reshape(B, HQ, S, Dh).astype(q.dtype)
    return o.transpose(0, 2, 1, 3).reshape(B, S, HQ * Dh)


def hybrid_mixer(h, positions, w_in, lb, hg_out_norm, attn_sink, w_branch_a, w_branch_b, w_out):
    B, S, _ = h.shape
    proj = h @ w_in
    offsets = [int(o) for o in np.cumsum(IN_SPLITS)[:-1]]
    hq, hf_f, hf_b, hi, hg, aq, ak, av, gate_a, gate_b = jnp.split(proj, offsets, axis=-1)

    q = jax.nn.silu(to_heads(hq, HG_HEADS))
    v = to_heads(hi, HG_HEADS)

    def forget(z, lb_d):
        lb_h = lb_d.reshape(HG_HEADS, 1, HG_DK)
        return lb_h + (1.0 - lb_h) * jax.nn.sigmoid(to_heads(z, HG_HEADS).astype(jnp.float32))

    f_fwd = forget(hf_f, lb[0])
    f_bwd = forget(hf_b, lb[1])
    o_f = hgrn2_scan(q, 1.0 - f_fwd, v, jnp.log(f_fwd))
    flip = lambda t: jnp.flip(t, axis=2)
    o_b = flip(hgrn2_scan(flip(q), flip(1.0 - f_bwd), flip(v), flip(jnp.log(f_bwd))))
    o = (o_f + o_b).astype(h.dtype)
    o = rmsnorm(o, hg_out_norm[:, None, :]) * jax.nn.silu(to_heads(hg, HG_HEADS))
    y_a = o.transpose(0, 2, 1, 3).reshape(B, S, HG_V) @ w_branch_a

    qa = rope(to_heads(aq, ATT_HEADS), positions)
    ka = rope(to_heads(ak, ATT_KV_HEADS), positions)
    va = to_heads(av, ATT_KV_HEADS)
    y_b = window_attention(qa, ka, va, attn_sink) @ w_branch_b

    merged = jax.nn.sigmoid(gate_a) * y_a + jax.nn.sigmoid(gate_b) * y_b
    return merged @ w_out


def expert_choice_ffn(h, w_router, w_gate, w_up, w_down):
    B, T, D = h.shape
    C = CAPACITY_FACTOR * T // N_EXPERTS
    logits = jnp.einsum('btd,de->bte', h.astype(jnp.float32), w_router.astype(jnp.float32))
    aff = jax.nn.softmax(logits, axis=-1)
    top_aff, top_idx = lax.top_k(jnp.swapaxes(aff, 1, 2), C)
    xe = jax.vmap(lambda hb, ib: hb[ib])(h, top_idx)
    g = jnp.einsum('becd,edf->becf', xe, w_gate)
    u = jnp.einsum('becd,edf->becf', xe, w_up)
    y = jnp.einsum('becf,efd->becd', jax.nn.silu(g) * u, w_down)
    y = y * top_aff[..., None].astype(y.dtype)
    return jax.vmap(lambda ib, yb: jnp.zeros((T, D), yb.dtype).at[ib.reshape(-1)].add(yb.reshape(-1, D)))(top_idx, y)


def setup_inputs(seed: int = 0) -> dict:
    key = jax.random.key(seed)
    ks = jax.random.split(key, 24)
    f32 = jnp.float32
    nrm = lambda k, shape, scale: jax.random.normal(k, shape, f32) * scale
    gain = lambda k, shape: 1.0 + 0.02 * jax.random.normal(k, shape, f32)
    D = D_MODEL
    return {
        "x": nrm(ks[0], (BATCH, SEQ, D), 1.0),
        "c": nrm(ks[1], (BATCH, D), 1.0),
        "positions": jnp.broadcast_to(jnp.arange(SEQ, dtype=jnp.int32), (BATCH, SEQ)),
        "w_ada": nrm(ks[2], (DEPTH, D, 6 * D), 0.5 * D ** -0.5),
        "b_ada": nrm(ks[3], (DEPTH, 6 * D), 0.02),
        "g_pre_mix": gain(ks[4], (DEPTH, D)),
        "g_post_mix": gain(ks[5], (DEPTH, D)),
        "g_pre_ffn": gain(ks[6], (DEPTH, D)),
        "g_post_ffn": gain(ks[7], (DEPTH, D)),
        "w_in": nrm(ks[8], (DEPTH, D, IN_WIDTH), D ** -0.5),
        "hg_lb_logits": nrm(ks[9], (2, DEPTH + 1, HG_QK), 0.1),
        "hg_out_norm": gain(ks[10], (DEPTH, HG_HEADS, HG_DV)),
        "attn_sink": nrm(ks[11], (DEPTH, ATT_HEADS), 0.5),
        "w_branch_a": nrm(ks[12], (DEPTH, HG_V, D), HG_V ** -0.5),
        "w_branch_b": nrm(ks[13], (DEPTH, ATT_Q, D), ATT_Q ** -0.5),
        "w_out": nrm(ks[14], (DEPTH, D, D), D ** -0.5),
        "w_router": nrm(ks[15], (DEPTH, D, N_EXPERTS), D ** -0.5),
        "w_exp_gate": nrm(ks[16], (DEPTH, N_EXPERTS, D, EXPERT_FF), D ** -0.5),
        "w_exp_up": nrm(ks[17], (DEPTH, N_EXPERTS, D, EXPERT_FF), D ** -0.5),
        "w_exp_down": nrm(ks[18], (DEPTH, N_EXPERTS, EXPERT_FF, D), EXPERT_FF ** -0.5),
    }


def reference(x, c, positions, w_ada, b_ada, g_pre_mix, g_post_mix, g_pre_ffn, g_post_ffn,
              w_in, hg_lb_logits, hg_out_norm, attn_sink, w_branch_a, w_branch_b, w_out,
              w_router, w_exp_gate, w_exp_up, w_exp_down):
    lb_all = jnp.cumsum(jax.nn.softmax(hg_lb_logits.astype(jnp.float32), axis=1), axis=1)
    for layer in range(DEPTH):
        mod = jax.nn.silu(c) @ w_ada[layer] + b_ada[layer]
        sh1, sc1, gt1, sh2, sc2, gt2 = jnp.split(mod[:, None, :], 6, axis=-1)
        h = rmsnorm(x, g_pre_mix[layer]) * (1.0 + sc1) + sh1
        y = hybrid_mixer(h, positions, w_in[layer], lb_all[:, layer], hg_out_norm[layer],
                         attn_sink[layer], w_branch_a[layer], w_branch_b[layer], w_out[layer])
        x = x + gt1 * rmsnorm(y, g_post_mix[layer])
        h = rmsnorm(x, g_pre_ffn[layer]) * (1.0 + sc2) + sh2
        y = expert_choice_ffn(h, w_router[layer], w_exp_gate[layer], w_exp_up[layer], w_exp_down[layer])
        x = x + gt2 * rmsnorm(y, g_post_ffn[layer])
    return x
```

```python
import functools
from typing import NamedTuple

import jax
import jax.numpy as jnp
from jax import lax
from jax.experimental import pallas as pl
from jax.experimental.pallas import tpu as pltpu

F32 = jnp.float32
BF16 = jnp.bfloat16

LANES = 128
HEAD = 128
HG_CHUNK = 64
WINDOW = 128
ROPE_THETA = 10000.0
NORM_EPS = 1e-6
CAPACITY_FACTOR = 2
VMEM_LIMIT_BYTES = 56 * 1024 * 1024
BISECT_ITERS = 152
SLOT_CHUNK = 128
TOKEN_BLOCK = 512


class Cfg(NamedTuple):
    B: int
    S: int
    D: int
    HH: int
    AH: int
    KVH: int
    E: int
    FF: int

    @property
    def T(self):
        return self.B * self.S

    @property
    def C(self):
        return CAPACITY_FACTOR * self.S // self.E

    @property
    def offsets(self):
        hq = self.HH * HEAD
        widths = (hq, hq, hq, hq, hq, self.AH * HEAD, self.KVH * HEAD, self.KVH * HEAD, self.D, self.D)
        offs, o = [], 0
        for w in widths:
            offs.append(o)
            o += w
        return tuple(offs), o


def _params(*sem):
    return pltpu.CompilerParams(dimension_semantics=sem, vmem_limit_bytes=VMEM_LIMIT_BYTES)


def _pick(n, target):
    if n <= target:
        return n
    t = (target // LANES) * LANES
    while t > LANES and n % t:
        t -= LANES
    return t


def _dot(a, b):
    return jnp.dot(a, b, preferred_element_type=F32)


def _dot_nt(a, b):
    return lax.dot_general(a, b, (((1,), (1,)), ((), ())), preferred_element_type=F32)


def _dot_tn(a, b):
    return lax.dot_general(a, b, (((0,), (0,)), ((), ())), preferred_element_type=F32)


def _silu(x):
    return x * jax.nn.sigmoid(x)


def _rms(x):
    return x * lax.rsqrt(jnp.mean(x * x, axis=-1, keepdims=True) + NORM_EPS)


def _ada_kernel(c_ref, w_ref, b_ref, o_ref):
    a = _silu(c_ref[...]).astype(BF16)
    o_ref[...] = _dot(a, w_ref[...].astype(BF16)) + b_ref[...]


def _ada(c_pad, w, b):
    rows, d = c_pad.shape
    n = w.shape[1]
    tn = _pick(n, 1024)
    return pl.pallas_call(
        _ada_kernel,
        grid=(n // tn,),
        in_specs=[pl.BlockSpec((rows, d), lambda j: (0, 0)),
                  pl.BlockSpec((d, tn), lambda j: (0, j)),
                  pl.BlockSpec((1, tn), lambda j: (0, j))],
        out_specs=pl.BlockSpec((rows, tn), lambda j: (0, j)),
        out_shape=jax.ShapeDtypeStruct((rows, n), F32),
        compiler_params=_params("arbitrary"),
        name="ada",
    )(c_pad, w, b)


def _prenorm_kernel(x_ref, g_ref, mod_ref, o_ref):
    y = _rms(x_ref[...]) * g_ref[...]
    shift = mod_ref[0, 0:1, :]
    scale = mod_ref[0, 1:2, :]
    o_ref[...] = (y * (1.0 + scale) + shift).astype(BF16)


def _prenorm(cfg, x2, g, mod3):
    tm = _pick(cfg.S, 256)
    per_b = cfg.S // tm
    return pl.pallas_call(
        _prenorm_kernel,
        grid=(cfg.T // tm,),
        in_specs=[pl.BlockSpec((tm, cfg.D), lambda i: (i, 0)),
                  pl.BlockSpec((1, cfg.D), lambda i: (0, 0)),
                  pl.BlockSpec((1, 6, cfg.D), lambda i: (i // per_b, 0, 0))],
        out_specs=pl.BlockSpec((tm, cfg.D), lambda i: (i, 0)),
        out_shape=jax.ShapeDtypeStruct((cfg.T, cfg.D), BF16),
        compiler_params=_params("arbitrary"),
        name="prenorm",
    )(x2, g, mod3)


def _inproj_kernel(a_ref, w_ref, o_ref, wbf_ref):
    @pl.when(pl.program_id(1) == 0)
    def _():
        wbf_ref[...] = w_ref[...].astype(BF16)

    o_ref[...] = _dot(a_ref[...], wbf_ref[...])


def _inproj(a, w):
    m, k = a.shape
    n = w.shape[1]
    tn = _pick(n, 1536)
    tm = _pick(m, 512)
    return pl.pallas_call(
        _inproj_kernel,
        grid=(n // tn, m // tm),
        in_specs=[pl.BlockSpec((tm, k), lambda j, i: (i, 0)),
                  pl.BlockSpec((k, tn), lambda j, i: (0, j))],
        out_specs=pl.BlockSpec((tm, tn), lambda j, i: (i, j)),
        out_shape=jax.ShapeDtypeStruct((m, n), F32),
        scratch_shapes=[pltpu.VMEM((k, tn), BF16)],
        compiler_params=_params("arbitrary", "arbitrary"),
        name="inproj",
    )(a, w)


def _rope_kernel(x_ref, pos_ref, inv_ref, sgn_ref, o_ref, cos_ref, sin_ref, *, n_q, q_scale):
    h = pl.program_id(1)

    @pl.when(h == 0)
    def _():
        ang = pos_ref[...].astype(F32) * inv_ref[...]
        cos_ref[...] = jnp.cos(ang)
        sin_ref[...] = jnp.sin(ang) * sgn_ref[...]

    x = x_ref[...]
    r = x * cos_ref[...] + pltpu.roll(x, HEAD // 2, 1) * sin_ref[...]
    scale = jnp.where(h < n_q, q_scale, 1.0).astype(F32)
    o_ref[...] = (r * scale).astype(BF16)


def _rope(cfg, proj, pos2, inv, sgn):
    offs, _ = cfg.offsets
    col0 = offs[5] // HEAD
    nh = cfg.AH + cfg.KVH
    tm = _pick(cfg.T, 512)
    return pl.pallas_call(
        functools.partial(_rope_kernel, n_q=cfg.AH, q_scale=HEAD ** -0.5),
        grid=(cfg.T // tm, nh),
        in_specs=[pl.BlockSpec((tm, HEAD), lambda i, h: (i, col0 + h)),
                  pl.BlockSpec((tm, 1), lambda i, h: (i, 0)),
                  pl.BlockSpec((1, HEAD), lambda i, h: (0, 0)),
                  pl.BlockSpec((1, HEAD), lambda i, h: (0, 0))],
        out_specs=pl.BlockSpec((tm, HEAD), lambda i, h: (i, h)),
        out_shape=jax.ShapeDtypeStruct((cfg.T, nh * HEAD), BF16),
        scratch_shapes=[pltpu.VMEM((tm, HEAD), F32), pltpu.VMEM((tm, HEAD), F32)],
        compiler_params=_params("arbitrary", "arbitrary"),
        name="rope",
    )(proj, pos2, inv, sgn)


def _attn_kernel(sink_ref, q_ref, kp_ref, kc_ref, kn_ref, vp_ref, vc_ref, vn_ref, o_ref, *, G, NB):
    kvh = pl.program_id(1)
    n = pl.program_id(2)
    blk = WINDOW
    k = jnp.concatenate([kp_ref[...], kc_ref[...], kn_ref[...]], axis=0)
    v = jnp.concatenate([vp_ref[...], vc_ref[...], vn_ref[...]], axis=0).astype(BF16)
    qi = lax.broadcasted_iota(jnp.int32, (blk, 3 * blk), 0)
    kj = lax.broadcasted_iota(jnp.int32, (blk, 3 * blk), 1)
    valid = jnp.abs(kj - blk - qi) <= WINDOW
    valid = valid & ((kj >= blk) | (n > 0)) & ((kj < 2 * blk) | (n < NB - 1))
    for g in range(G):
        q = q_ref[:, g * HEAD:(g + 1) * HEAD]
        s = jnp.where(valid, _dot_nt(q, k), -jnp.inf)
        sk = sink_ref[kvh * G + g]
        m = jnp.maximum(jnp.max(s, axis=-1, keepdims=True), sk)
        p = jnp.exp(s - m)
        denom = jnp.sum(p, axis=-1, keepdims=True) + jnp.exp(sk - m)
        o = _dot(p.astype(BF16), v) / denom
        o_ref[:, g * HEAD:(g + 1) * HEAD] = o.astype(BF16)


def _attn(cfg, qk_rot, proj, sink):
    offs, _ = cfg.offsets
    G = cfg.AH // cfg.KVH
    blk = WINDOW
    NB = cfg.S // blk
    kcol = cfg.AH
    vcol = offs[7] // HEAD
    prev = lambda n: jnp.maximum(n - 1, 0)
    nxt = lambda n: jnp.minimum(n + 1, NB - 1)
    return pl.pallas_call(
        functools.partial(_attn_kernel, G=G, NB=NB),
        grid=(cfg.B, cfg.KVH, NB),
        in_specs=[pl.BlockSpec(memory_space=pltpu.SMEM),
                  pl.BlockSpec((blk, G * HEAD), lambda b, h, n: (b * NB + n, h)),
                  pl.BlockSpec((blk, HEAD), lambda b, h, n: (b * NB + prev(n), kcol + h)),
                  pl.BlockSpec((blk, HEAD), lambda b, h, n: (b * NB + n, kcol + h)),
                  pl.BlockSpec((blk, HEAD), lambda b, h, n: (b * NB + nxt(n), kcol + h)),
                  pl.BlockSpec((blk, HEAD), lambda b, h, n: (b * NB + prev(n), vcol + h)),
                  pl.BlockSpec((blk, HEAD), lambda b, h, n: (b * NB + n, vcol + h)),
                  pl.BlockSpec((blk, HEAD), lambda b, h, n: (b * NB + nxt(n), vcol + h))],
        out_specs=pl.BlockSpec((blk, G * HEAD), lambda b, h, n: (b * NB + n, h)),
        out_shape=jax.ShapeDtypeStruct((cfg.T, cfg.AH * HEAD), BF16),
        compiler_params=_params("arbitrary", "arbitrary", "arbitrary"),
        name="attn",
    )(sink, qk_rot, qk_rot, qk_rot, qk_rot, proj, proj, proj)


def _lower_bound(lbl_ref, d, layer):
    n = lbl_ref.shape[1]
    rows = [lbl_ref[d, j:j + 1, :] for j in range(n)]
    m = functools.reduce(jnp.maximum, rows)
    es = [jnp.exp(r - m) for r in rows]
    return sum(es[:layer + 1]) / sum(es)


def _hgrn_chunk(qs, k, logf, v, st, tri, ref_i, last_i):
    hi = logf.astype(BF16)
    lo = (logf - hi.astype(F32)).astype(BF16)
    tri_bf = tri.astype(BF16)
    b = _dot(tri_bf, hi) + _dot(tri_bf, lo)
    bref = b[ref_i:ref_i + 1, :]
    blast = b[last_i:last_i + 1, :]
    q_in = (qs * jnp.exp(b - bref)).astype(BF16)
    k_in = (k * jnp.exp(bref - b)).astype(BF16)
    a = _dot_nt(q_in, k_in) * tri
    v_bf = v.astype(BF16)
    q_dec = (qs * jnp.exp(b)).astype(BF16)
    o = _dot(a.astype(BF16), v_bf) + _dot_nt(q_dec, st.astype(BF16))
    k_dec = (k * jnp.exp(blast - b)).astype(BF16)
    st_new = st * jnp.exp(blast) + _dot_tn(v_bf, k_dec)
    return o, st_new


def _hgrn_kernel(lbl_ref, gain_ref, q_ref, v_ref, zf_ref, zb_ref, g_ref, o_ref,
                 stf_ref, stb_ref, of_ref, *, NT, R, layer):
    s = pl.program_id(2)
    L = HG_CHUNK
    nc = R // L
    li = lax.broadcasted_iota(jnp.int32, (L, L), 0)
    mi = lax.broadcasted_iota(jnp.int32, (L, L), 1)

    @pl.when(s == 0)
    def _():
        stf_ref[...] = jnp.zeros_like(stf_ref)
        stb_ref[...] = jnp.zeros_like(stb_ref)

    def gates(z_ref, d):
        lb = _lower_bound(lbl_ref, d, layer)
        f = lb + (1.0 - lb) * jax.nn.sigmoid(z_ref[...])
        return 1.0 - f, jnp.log(f)

    @pl.when(s < NT)
    def _():
        qs = _silu(q_ref[...])
        k, logf = gates(zf_ref, 0)
        v = v_ref[...]
        tri = jnp.where(mi <= li, 1.0, 0.0).astype(F32)
        st = stf_ref[...]
        for c in range(nc):
            r = slice(c * L, (c + 1) * L)
            o, st = _hgrn_chunk(qs[r], k[r], logf[r], v[r], st, tri, L // 2, L - 1)
            of_ref[pl.ds(pl.multiple_of(s * R + c * L, L), L), :] = o
        stf_ref[...] = st

    @pl.when(s >= NT)
    def _():
        t = 2 * NT - 1 - s
        qs = _silu(q_ref[...])
        k, logf = gates(zb_ref, 1)
        v = v_ref[...]
        gate = _silu(g_ref[...])
        tri = jnp.where(mi >= li, 1.0, 0.0).astype(F32)
        st = stb_ref[...]
        for c in reversed(range(nc)):
            r = slice(c * L, (c + 1) * L)
            o, st = _hgrn_chunk(qs[r], k[r], logf[r], v[r], st, tri, L - 1 - L // 2, 0)
            o = o + of_ref[pl.ds(pl.multiple_of(t * R + c * L, L), L), :]
            o_ref[r, :] = (_rms(o) * gain_ref[0] * gate[r]).astype(BF16)
        stb_ref[...] = st


def _hgrn(cfg, proj, lb_logits, gain3, layer):
    offs, _ = cfg.offsets
    qc, fc, bc, ic, gc = (o // HEAD for o in offs[:5])
    R = _pick(cfg.S, 512)
    NT = cfg.S // R
    tq = lambda s: jnp.where(s < NT, s, 2 * NT - 1 - s)
    tf = lambda s: jnp.minimum(s, NT - 1)
    tb = lambda s: jnp.where(s < NT, NT - 1, 2 * NT - 1 - s)
    nl = lb_logits.shape[1]
    return pl.pallas_call(
        functools.partial(_hgrn_kernel, NT=NT, R=R, layer=layer),
        grid=(cfg.B, cfg.HH, 2 * NT),
        in_specs=[pl.BlockSpec((2, nl, HEAD), lambda b, h, s: (0, 0, h)),
                  pl.BlockSpec((1, 1, HEAD), lambda b, h, s: (h, 0, 0)),
                  pl.BlockSpec((R, HEAD), lambda b, h, s: (b * NT + tq(s), qc + h)),
                  pl.BlockSpec((R, HEAD), lambda b, h, s: (b * NT + tq(s), ic + h)),
                  pl.BlockSpec((R, HEAD), lambda b, h, s: (b * NT + tf(s), fc + h)),
                  pl.BlockSpec((R, HEAD), lambda b, h, s: (b * NT + tb(s), bc + h)),
                  pl.BlockSpec((R, HEAD), lambda b, h, s: (b * NT + tb(s), gc + h))],
        out_specs=pl.BlockSpec((R, HEAD), lambda b, h, s: (b * NT + tb(s), h)),
        out_shape=jax.ShapeDtypeStruct((cfg.T, cfg.HH * HEAD), BF16),
        scratch_shapes=[pltpu.VMEM((HEAD, HEAD), F32), pltpu.VMEM((HEAD, HEAD), F32),
                        pltpu.VMEM((cfg.S, HEAD), F32)],
        compiler_params=_params("arbitrary", "arbitrary", "arbitrary"),
        name="hgrn",
    )(lb_logits, gain3, proj, proj, proj, proj, proj)


def _merge_kernel(oa_ref, ob_ref, ga_ref, gb_ref, wa_ref, wb_ref, o_ref, wabf_ref, wbbf_ref):
    @pl.when(pl.program_id(1) == 0)
    def _():
        wabf_ref[...] = wa_ref[...].astype(BF16)
        wbbf_ref[...] = wb_ref[...].astype(BF16)

    ya = _dot(oa_ref[...], wabf_ref[...])
    yb = _dot(ob_ref[...], wbbf_ref[...])
    o_ref[...] = (jax.nn.sigmoid(ga_ref[...]) * ya + jax.nn.sigmoid(gb_ref[...]) * yb).astype(BF16)


def _merge(cfg, oa, ob, proj, wa, wb):
    offs, _ = cfg.offsets
    tn = _pick(cfg.D, 512)
    assert offs[8] % tn == 0 and offs[9] % tn == 0
    ga0, gb0 = offs[8] // tn, offs[9] // tn
    tm = _pick(cfg.T, 512)
    ka, kb = oa.shape[1], ob.shape[1]
    return pl.pallas_call(
        _merge_kernel,
        grid=(cfg.D // tn, cfg.T // tm),
        in_specs=[pl.BlockSpec((tm, ka), lambda j, i: (i, 0)),
                  pl.BlockSpec((tm, kb), lambda j, i: (i, 0)),
                  pl.BlockSpec((tm, tn), lambda j, i: (i, ga0 + j)),
                  pl.BlockSpec((tm, tn), lambda j, i: (i, gb0 + j)),
                  pl.BlockSpec((ka, tn), lambda j, i: (0, j)),
                  pl.BlockSpec((kb, tn), lambda j, i: (0, j))],
        out_specs=pl.BlockSpec((tm, tn), lambda j, i: (i, j)),
        out_shape=jax.ShapeDtypeStruct((cfg.T, cfg.D), BF16),
        scratch_shapes=[pltpu.VMEM((ka, tn), BF16), pltpu.VMEM((kb, tn), BF16)],
        compiler_params=_params("arbitrary", "arbitrary"),
        name="merge",
    )(oa, ob, proj, proj, wa, wb)


def _outproj_kernel(a_ref, w_ref, x_ref, mod_ref, gpost_ref, gpre_ref, wr_ref,
                    x1_ref, h_ref, aff_ref, wbf_ref, *, E):
    @pl.when(pl.program_id(0) == 0)
    def _():
        wbf_ref[...] = w_ref[...].astype(BF16)

    y = _dot(a_ref[...], wbf_ref[...])
    x1 = x_ref[...] + mod_ref[0, 2:3, :] * (_rms(y) * gpost_ref[...])
    x1_ref[...] = x1
    h = _rms(x1) * gpre_ref[...] * (1.0 + mod_ref[0, 4:5, :]) + mod_ref[0, 3:4, :]
    h_hi = h.astype(BF16)
    h_ref[...] = h_hi
    h_lo = (h - h_hi.astype(F32)).astype(BF16)
    wr = wr_ref[...]
    w_hi = wr.astype(BF16)
    w_lo = (wr - w_hi.astype(F32)).astype(BF16)
    logits = _dot(h_hi, w_hi) + _dot(h_lo, w_hi) + _dot(h_hi, w_lo)
    lane = lax.broadcasted_iota(jnp.int32, logits.shape, 1)
    logits = jnp.where(lane < E, logits, -jnp.inf)
    p = jnp.exp(logits - jnp.max(logits, axis=-1, keepdims=True))
    aff_ref[...] = p / jnp.sum(p, axis=-1, keepdims=True)


def _outproj(cfg, merged, w, x2, mod3, gpost, gpre, wr_pad):
    tm = _pick(cfg.S, 256)
    per_b = cfg.S // tm
    D = cfg.D
    const = lambda i: (0, 0)
    row = lambda i: (i, 0)
    return pl.pallas_call(
        functools.partial(_outproj_kernel, E=cfg.E),
        grid=(cfg.T // tm,),
        in_specs=[pl.BlockSpec((tm, D), row),
                  pl.BlockSpec((D, D), const, pipeline_mode=pl.Buffered(1)),
                  pl.BlockSpec((tm, D), row),
                  pl.BlockSpec((1, 6, D), lambda i: (i // per_b, 0, 0)),
                  pl.BlockSpec((1, D), const),
                  pl.BlockSpec((1, D), const),
                  pl.BlockSpec((D, LANES), const)],
        out_specs=[pl.BlockSpec((tm, D), row),
                   pl.BlockSpec((tm, D), row),
                   pl.BlockSpec((tm, LANES), row)],
        out_shape=[jax.ShapeDtypeStruct((cfg.T, D), F32),
                   jax.ShapeDtypeStruct((cfg.T, D), BF16),
                   jax.ShapeDtypeStruct((cfg.T, LANES), F32)],
        scratch_shapes=[pltpu.VMEM((D, D), BF16)],
        compiler_params=_params("arbitrary"),
        name="outproj",
    )(merged, w, x2, mod3, gpost, gpre, wr_pad)


def _route_kernel(aff_ref, posm_ref, postm_ref, offs_ref, a_ref, *, E, S, C):
    nt = S // LANES
    for t in range(nt):
        a_ref[:, t * LANES:(t + 1) * LANES] = aff_ref[t * LANES:(t + 1) * LANES, :].T[:E, :]

    cap = jnp.float32(C)

    def bisect(_, carry):
        lo, hi = carry
        mid = 0.5 * (lo + hi)
        cnt = jnp.sum(jnp.where(a_ref[...] >= mid, 1.0, 0.0), axis=1, keepdims=True)
        ge = cnt >= cap
        return jnp.where(ge, mid, lo), jnp.where(ge, hi, mid)

    lo, hi = lax.fori_loop(0, BISECT_ITERS, bisect,
                           (jnp.zeros((E, 1), F32), jnp.full((E, 1), 2.0, F32)))
    need = cap - jnp.sum(jnp.where(a_ref[...] >= hi, 1.0, 0.0), axis=1, keepdims=True)

    ui = lax.broadcasted_iota(jnp.int32, (LANES, LANES), 0)
    uj = lax.broadcasted_iota(jnp.int32, (LANES, LANES), 1)
    upper = jnp.where(ui <= uj, 1.0, 0.0).astype(BF16)
    lane = lax.broadcasted_iota(jnp.int32, (E, LANES), 1)
    run_tie = jnp.zeros((E, 1), F32)
    run_sel = jnp.zeros((E, 1), F32)
    offs = jnp.zeros((E, LANES), F32)
    pad = jnp.full((LANES - E, LANES), -1.0, F32)
    for t in range(nt):
        cols = slice(t * LANES, (t + 1) * LANES)
        a = a_ref[:, cols]
        tie = (a >= lo) & (a < hi)
        tie_t = jnp.where(tie, 1.0, 0.0)
        tie_rank = _dot(tie_t.astype(BF16), upper) - tie_t + run_tie
        sel = (a >= hi) | (tie & (tie_rank < need))
        sel_t = jnp.where(sel, 1.0, 0.0)
        pos = _dot(sel_t.astype(BF16), upper) - sel_t + run_sel
        posm = jnp.where(sel, pos, -1.0)
        posm_ref[0, :, cols] = posm
        postm_ref[cols, :] = jnp.concatenate([posm, pad], axis=0).T
        offs = jnp.where(lane == t, run_sel, offs)
        run_tie = run_tie + jnp.sum(tie_t, axis=1, keepdims=True)
        run_sel = run_sel + jnp.sum(sel_t, axis=1, keepdims=True)
    offs = jnp.where(lane == nt, run_sel, offs)
    offs_ref[0] = offs.astype(jnp.int32)


def _route(cfg, aff_tm):
    E, S = cfg.E, cfg.S
    assert S // LANES < LANES and E % 8 == 0
    return pl.pallas_call(
        functools.partial(_route_kernel, E=E, S=S, C=cfg.C),
        grid=(cfg.B,),
        in_specs=[pl.BlockSpec((S, LANES), lambda b: (b, 0))],
        out_specs=[pl.BlockSpec((1, E, S), lambda b: (b, 0, 0)),
                   pl.BlockSpec((S, LANES), lambda b: (b, 0)),
                   pl.BlockSpec((1, E, LANES), lambda b: (b, 0, 0))],
        out_shape=[jax.ShapeDtypeStruct((cfg.B, E, S), F32),
                   jax.ShapeDtypeStruct((cfg.T, LANES), F32),
                   jax.ShapeDtypeStruct((cfg.B, E, LANES), jnp.int32)],
        scratch_shapes=[pltpu.VMEM((E, S), F32)],
        compiler_params=_params("arbitrary"),
        name="route",
    )(aff_tm)


def _overlap(offs_ref, base, token_block, slot_chunk):
    per = TOKEN_BLOCK // LANES
    first = offs_ref[base + token_block * per]
    end = offs_ref[base + (token_block + 1) * per]
    return (first < (slot_chunk + 1) * SLOT_CHUNK) & (end > slot_chunk * SLOT_CHUNK)


def _gather_kernel(offs_ref, posm_ref, h_ref, x_ref, acc_ref, *, E, S, C):
    base = (pl.program_id(0) * E + pl.program_id(1)) * LANES
    acc_ref[...] = jnp.zeros_like(acc_ref)
    for jc in range(C // SLOT_CHUNK):
        slot = (lax.broadcasted_iota(jnp.int32, (SLOT_CHUNK, TOKEN_BLOCK), 0) + jc * SLOT_CHUNK).astype(F32)
        for kb in range(S // TOKEN_BLOCK):
            @pl.when(_overlap(offs_ref, base, kb, jc))
            def _():
                toks = slice(kb * TOKEN_BLOCK, (kb + 1) * TOKEN_BLOCK)
                onehot = jnp.where(posm_ref[0, :, toks] == slot, 1.0, 0.0).astype(BF16)
                rows = slice(jc * SLOT_CHUNK, (jc + 1) * SLOT_CHUNK)
                acc_ref[rows, :] += _dot(onehot, h_ref[toks, :])
    x_ref[0, 0] = acc_ref[...].astype(BF16)


def _gather(cfg, offs_flat, posm3, h2):
    B, E, S, C, D = cfg.B, cfg.E, cfg.S, cfg.C, cfg.D
    grid_spec = pltpu.PrefetchScalarGridSpec(
        num_scalar_prefetch=1,
        grid=(B, E),
        in_specs=[pl.BlockSpec((1, 1, S), lambda b, e, offs: (b * E + e, 0, 0)),
                  pl.BlockSpec((S, D), lambda b, e, offs: (b, 0))],
        out_specs=pl.BlockSpec((1, 1, C, D), lambda b, e, offs: (b, e, 0, 0)),
        scratch_shapes=[pltpu.VMEM((C, D), F32)],
    )
    return pl.pallas_call(
        functools.partial(_gather_kernel, E=E, S=S, C=C),
        grid_spec=grid_spec,
        out_shape=jax.ShapeDtypeStruct((B, E, C, D), BF16),
        compiler_params=_params("arbitrary", "arbitrary"),
        name="gather",
    )(offs_flat, posm3, h2)


def _ffn_kernel(x_ref, wg_ref, wu_ref, wd_ref, y_ref, acc_ref):
    f = pl.program_id(1)

    @pl.when(f == 0)
    def _():
        acc_ref[...] = jnp.zeros_like(acc_ref)

    wg = wg_ref[0].astype(BF16)
    wu = wu_ref[0].astype(BF16)
    wd = wd_ref[0].astype(BF16)
    for b in range(x_ref.shape[0]):
        x = x_ref[b, 0]
        a = (_silu(_dot(x, wg)) * _dot(x, wu)).astype(BF16)
        acc_ref[b] += _dot(a, wd)

    @pl.when(f == pl.num_programs(1) - 1)
    def _():
        y_ref[:, 0] = acc_ref[...].astype(BF16)


def _ffn(cfg, xe, wg, wu, wd):
    B, E, C, D, FF = cfg.B, cfg.E, cfg.C, cfg.D, cfg.FF
    tf = _pick(FF, 256)
    return pl.pallas_call(
        _ffn_kernel,
        grid=(E, FF // tf),
        in_specs=[pl.BlockSpec((B, 1, C, D), lambda e, f: (0, e, 0, 0)),
                  pl.BlockSpec((1, D, tf), lambda e, f: (e, 0, f)),
                  pl.BlockSpec((1, D, tf), lambda e, f: (e, 0, f)),
                  pl.BlockSpec((1, tf, D), lambda e, f: (e, f, 0))],
        out_specs=pl.BlockSpec((B, 1, C, D), lambda e, f: (0, e, 0, 0)),
        out_shape=jax.ShapeDtypeStruct((B, E, C, D), BF16),
        scratch_shapes=[pltpu.VMEM((B, C, D), F32)],
        compiler_params=_params("arbitrary", "arbitrary"),
        name="ffn",
    )(xe, wg, wu, wd)


def _combine_kernel(offs_ref, pos_ref, aff_ref, y_ref, x1_ref, mod_ref, g_ref, o_ref, acc_ref, *, E, C):
    b, kb, e = pl.program_id(0), pl.program_id(1), pl.program_id(2)
    base = (b * E + e) * LANES

    @pl.when(e == 0)
    def _():
        acc_ref[...] = jnp.zeros_like(acc_ref)

    lane = lax.broadcasted_iota(jnp.int32, pos_ref.shape, 1)
    mine = lane == e
    pos = jnp.sum(jnp.where(mine, pos_ref[...], 0.0), axis=1, keepdims=True)
    aff = jnp.sum(jnp.where(mine, aff_ref[...], 0.0), axis=1, keepdims=True)
    for jc in range(C // SLOT_CHUNK):
        @pl.when(_overlap(offs_ref, base, kb, jc))
        def _():
            slot = (lax.broadcasted_iota(jnp.int32, (TOKEN_BLOCK, SLOT_CHUNK), 1) + jc * SLOT_CHUNK).astype(F32)
            onehot = jnp.where(pos == slot, 1.0, 0.0).astype(BF16)
            rows = slice(jc * SLOT_CHUNK, (jc + 1) * SLOT_CHUNK)
            acc_ref[...] += aff * _dot(onehot, y_ref[0, 0, rows, :])

    @pl.when(e == E - 1)
    def _():
        o_ref[...] = x1_ref[...] + mod_ref[0, 5:6, :] * (_rms(acc_ref[...]) * g_ref[...])


def _combine(cfg, offs_flat, pos_tm, aff_tm, y, x1, mod3, g):
    B, E, S, C, D = cfg.B, cfg.E, cfg.S, cfg.C, cfg.D
    nk = S // TOKEN_BLOCK
    tok = lambda b, k, e, offs: (b * nk + k, 0)
    grid_spec = pltpu.PrefetchScalarGridSpec(
        num_scalar_prefetch=1,
        grid=(B, nk, E),
        in_specs=[pl.BlockSpec((TOKEN_BLOCK, LANES), tok),
                  pl.BlockSpec((TOKEN_BLOCK, LANES), tok),
                  pl.BlockSpec((1, 1, C, D), lambda b, k, e, offs: (b, e, 0, 0)),
                  pl.BlockSpec((TOKEN_BLOCK, D), tok),
                  pl.BlockSpec((1, 6, D), lambda b, k, e, offs: (b, 0, 0)),
                  pl.BlockSpec((1, D), lambda b, k, e, offs: (0, 0))],
        out_specs=pl.BlockSpec((TOKEN_BLOCK, D), tok),
        scratch_shapes=[pltpu.VMEM((TOKEN_BLOCK, D), F32)],
    )
    return pl.pallas_call(
        functools.partial(_combine_kernel, E=E, C=C),
        grid_spec=grid_spec,
        out_shape=jax.ShapeDtypeStruct((cfg.T, D), F32),
        compiler_params=_params("arbitrary", "arbitrary", "arbitrary"),
        name="combine",
    )(offs_flat, pos_tm, aff_tm, y, x1, mod3, g)


def _layer(cfg, layer, x2, c_pad, pos2, inv, sgn, w_ada, b_ada, g_pre_mix, g_post_mix, g_pre_ffn,
           g_post_ffn, w_in, hg_lb_logits, hg_out_norm, attn_sink, w_branch_a, w_branch_b, w_out,
           w_router, w_exp_gate, w_exp_up, w_exp_down):
    D, E = cfg.D, cfg.E
    row = lambda v: v.reshape(1, -1)
    mod = _ada(c_pad, w_ada[layer], row(b_ada[layer]))[:cfg.B]
    mod3 = mod.reshape(cfg.B, 6, D)
    h = _prenorm(cfg, x2, row(g_pre_mix[layer]), mod3)
    proj = _inproj(h, w_in[layer])
    qk_rot = _rope(cfg, proj, pos2, inv, sgn)
    att = _attn(cfg, qk_rot, proj, attn_sink[layer])
    oa = _hgrn(cfg, proj, hg_lb_logits, hg_out_norm[layer].reshape(cfg.HH, 1, HEAD), layer)
    merged = _merge(cfg, oa, att, proj, w_branch_a[layer], w_branch_b[layer])
    wr_pad = jnp.pad(w_router[layer], ((0, 0), (0, LANES - E)))
    x1, h2, aff_tm = _outproj(cfg, merged, w_out[layer], x2, mod3, row(g_post_mix[layer]),
                              row(g_pre_ffn[layer]), wr_pad)
    posm, pos_tm, offs = _route(cfg, aff_tm)
    offs_flat = offs.reshape(-1)
    xe = _gather(cfg, offs_flat, posm.reshape(cfg.B * E, 1, cfg.S), h2)
    ye = _ffn(cfg, xe, w_exp_gate[layer], w_exp_up[layer], w_exp_down[layer])
    return _combine(cfg, offs_flat, pos_tm, aff_tm, ye, x1, mod3, row(g_post_ffn[layer]))


def _block(cfg, x, c, positions, w_ada, *rest):
    depth = w_ada.shape[0]
    x2 = x.reshape(cfg.T, cfg.D)
    c_pad = jnp.pad(c, ((0, 8 - cfg.B), (0, 0)))
    pos2 = positions.reshape(cfg.T, 1)
    half = HEAD // 2
    inv_half = ROPE_THETA ** (-jnp.arange(half, dtype=F32) / half)
    inv = jnp.concatenate([inv_half, inv_half]).reshape(1, HEAD)
    sgn = jnp.concatenate([-jnp.ones((half,), F32), jnp.ones((half,), F32)]).reshape(1, HEAD)
    for layer in range(depth):
        x2 = _layer(cfg, layer, x2, c_pad, pos2, inv, sgn, w_ada, *rest)
    return x2.reshape(cfg.B, cfg.S, cfg.D)


def kernel(x, c, positions, w_ada, b_ada, g_pre_mix, g_post_mix, g_pre_ffn, g_post_ffn, w_in,
           hg_lb_logits, hg_out_norm, attn_sink, w_branch_a, w_branch_b, w_out, w_router,
           w_exp_gate, w_exp_up, w_exp_down):
    B, S, D = x.shape
    cfg = Cfg(B=B, S=S, D=D,
              HH=hg_out_norm.shape[1], AH=attn_sink.shape[1],
              KVH=(w_in.shape[2] - 5 * hg_out_norm.shape[1] * HEAD - attn_sink.shape[1] * HEAD - 2 * D) // (2 * HEAD),
              E=w_router.shape[2], FF=w_exp_gate.shape[3])
    return _block(cfg, x, c, positions, w_ada, b_ada, g_pre_mix, g_post_mix, g_pre_ffn, g_post_ffn,
                  w_in, hg_lb_logits, hg_out_norm, attn_sink, w_branch_a, w_branch_b, w_out,
                  w_router, w_exp_gate, w_exp_up, w_exp_down)
```

```python
import functools
from typing import NamedTuple

import jax
import jax.numpy as jnp
from jax import lax
from jax.experimental import pallas as pl
from jax.experimental.pallas import tpu as pltpu

F32 = jnp.float32
BF16 = jnp.bfloat16

LANES = 128
HEAD = 128
HG_CHUNK = 64
WINDOW = 128
ROPE_THETA = 10000.0
NORM_EPS = 1e-6
CAPACITY_FACTOR = 2
VMEM_LIMIT_BYTES = 56 * 1024 * 1024
BISECT_ITERS = 152
SLOT_CHUNK = 128
TOKEN_BLOCK = 512
HGRN_HEADS_PER_STEP = 2
COMBINE_TOKENS = 256
COMBINE_WINDOW = 64
BF16_ROWS = 16


class Cfg(NamedTuple):
    B: int
    S: int
    D: int
    HH: int
    AH: int
    KVH: int
    E: int
    FF: int

    @property
    def T(self):
        return self.B * self.S

    @property
    def C(self):
        return CAPACITY_FACTOR * self.S // self.E

    @property
    def offsets(self):
        hq = self.HH * HEAD
        widths = (hq, hq, hq, hq, hq, self.AH * HEAD, self.KVH * HEAD, self.KVH * HEAD, self.D, self.D)
        offs, o = [], 0
        for w in widths:
            offs.append(o)
            o += w
        return tuple(offs), o


def _params(*sem):
    return pltpu.CompilerParams(dimension_semantics=sem, vmem_limit_bytes=VMEM_LIMIT_BYTES)


def _pick(n, target):
    if n <= target:
        return n
    t = (target // LANES) * LANES
    while t > LANES and n % t:
        t -= LANES
    return t


def _dot(a, b):
    return jnp.dot(a, b, preferred_element_type=F32)


def _dot_nt(a, b):
    return lax.dot_general(a, b, (((1,), (1,)), ((), ())), preferred_element_type=F32)


def _dot_tn(a, b):
    return lax.dot_general(a, b, (((0,), (0,)), ((), ())), preferred_element_type=F32)


def _silu(x):
    return x * jax.nn.sigmoid(x)


def _rms(x):
    return x * lax.rsqrt(jnp.mean(x * x, axis=-1, keepdims=True) + NORM_EPS)


def _ada_kernel(c_ref, w_ref, b_ref, o_ref):
    a = _silu(c_ref[...]).astype(BF16)
    o_ref[...] = _dot(a, w_ref[...].astype(BF16)) + b_ref[...]


def _ada(c_pad, w, b):
    rows, d = c_pad.shape
    n = w.shape[1]
    tn = _pick(n, 1024)
    return pl.pallas_call(
        _ada_kernel,
        grid=(n // tn,),
        in_specs=[pl.BlockSpec((rows, d), lambda j: (0, 0)),
                  pl.BlockSpec((d, tn), lambda j: (0, j)),
                  pl.BlockSpec((1, tn), lambda j: (0, j))],
        out_specs=pl.BlockSpec((rows, tn), lambda j: (0, j)),
        out_shape=jax.ShapeDtypeStruct((rows, n), F32),
        compiler_params=_params("arbitrary"),
        name="ada",
    )(c_pad, w, b)


def _prenorm_kernel(x_ref, g_ref, mod_ref, o_ref):
    y = _rms(x_ref[...]) * g_ref[...]
    shift = mod_ref[0, 0:1, :]
    scale = mod_ref[0, 1:2, :]
    o_ref[...] = (y * (1.0 + scale) + shift).astype(BF16)


def _prenorm(cfg, x2, g, mod3):
    tm = _pick(cfg.S, 256)
    per_b = cfg.S // tm
    return pl.pallas_call(
        _prenorm_kernel,
        grid=(cfg.T // tm,),
        in_specs=[pl.BlockSpec((tm, cfg.D), lambda i: (i, 0)),
                  pl.BlockSpec((1, cfg.D), lambda i: (0, 0)),
                  pl.BlockSpec((1, 6, cfg.D), lambda i: (i // per_b, 0, 0))],
        out_specs=pl.BlockSpec((tm, cfg.D), lambda i: (i, 0)),
        out_shape=jax.ShapeDtypeStruct((cfg.T, cfg.D), BF16),
        compiler_params=_params("arbitrary"),
        name="prenorm",
    )(x2, g, mod3)


def _inproj_kernel(a_ref, w_ref, o_ref, wbf_ref):
    @pl.when(pl.program_id(1) == 0)
    def _():
        wbf_ref[...] = w_ref[...].astype(BF16)

    o_ref[...] = _dot(a_ref[...], wbf_ref[...])


def _inproj(a, w):
    m, k = a.shape
    n = w.shape[1]
    tn = _pick(n, 1536)
    tm = _pick(m, 512)
    return pl.pallas_call(
        _inproj_kernel,
        grid=(n // tn, m // tm),
        in_specs=[pl.BlockSpec((tm, k), lambda j, i: (i, 0)),
                  pl.BlockSpec((k, tn), lambda j, i: (0, j))],
        out_specs=pl.BlockSpec((tm, tn), lambda j, i: (i, j)),
        out_shape=jax.ShapeDtypeStruct((m, n), F32),
        scratch_shapes=[pltpu.VMEM((k, tn), BF16)],
        compiler_params=_params("arbitrary", "arbitrary"),
        name="inproj",
    )(a, w)


def _rope_kernel(q_ref, k_ref, pos_ref, inv_ref, sgn_ref, qo_ref, ko_ref, *, q_scale):
    ang = pos_ref[...].astype(F32) * inv_ref[...]
    cos = jnp.cos(ang)
    sin = jnp.sin(ang) * sgn_ref[...]

    def rotate(x_ref, o_ref, scale):
        for h in range(x_ref.shape[1] // HEAD):
            cols = slice(h * HEAD, (h + 1) * HEAD)
            x = x_ref[:, cols]
            r = x * cos + pltpu.roll(x, HEAD // 2, 1) * sin
            o_ref[:, cols] = (r if scale is None else r * scale).astype(BF16)

    rotate(q_ref, qo_ref, q_scale)
    rotate(k_ref, ko_ref, None)


def _rope(cfg, proj, pos2, inv, sgn):
    offs, _ = cfg.offsets
    wq, wk = cfg.AH * HEAD, cfg.KVH * HEAD
    assert offs[5] % wq == 0 and offs[6] % wk == 0
    tm = _pick(cfg.T, 256)
    return pl.pallas_call(
        functools.partial(_rope_kernel, q_scale=HEAD ** -0.5),
        grid=(cfg.T // tm,),
        in_specs=[pl.BlockSpec((tm, wq), lambda i: (i, offs[5] // wq)),
                  pl.BlockSpec((tm, wk), lambda i: (i, offs[6] // wk)),
                  pl.BlockSpec((tm, 1), lambda i: (i, 0)),
                  pl.BlockSpec((1, HEAD), lambda i: (0, 0)),
                  pl.BlockSpec((1, HEAD), lambda i: (0, 0))],
        out_specs=[pl.BlockSpec((tm, wq), lambda i: (i, 0)),
                   pl.BlockSpec((tm, wk), lambda i: (i, 0))],
        out_shape=[jax.ShapeDtypeStruct((cfg.T, wq), BF16),
                   jax.ShapeDtypeStruct((cfg.T, wk), BF16)],
        compiler_params=_params("arbitrary"),
        name="rope",
    )(proj, proj, pos2, inv, sgn)


def _attn_kernel(sink_ref, q_ref, kp_ref, kc_ref, kn_ref, vp_ref, vc_ref, vn_ref, o_ref, *, G, NB):
    kvh = pl.program_id(1)
    n = pl.program_id(2)
    blk = WINDOW
    k = jnp.concatenate([kp_ref[...], kc_ref[...], kn_ref[...]], axis=0)
    v = jnp.concatenate([vp_ref[...], vc_ref[...], vn_ref[...]], axis=0).astype(BF16)
    qi = lax.broadcasted_iota(jnp.int32, (blk, 3 * blk), 0)
    kj = lax.broadcasted_iota(jnp.int32, (blk, 3 * blk), 1)
    valid = jnp.abs(kj - blk - qi) <= WINDOW
    valid = valid & ((kj >= blk) | (n > 0)) & ((kj < 2 * blk) | (n < NB - 1))
    for g in range(G):
        q = q_ref[:, g * HEAD:(g + 1) * HEAD]
        s = jnp.where(valid, _dot_nt(q, k), -jnp.inf)
        sk = sink_ref[kvh * G + g]
        m = jnp.maximum(jnp.max(s, axis=-1, keepdims=True), sk)
        p = jnp.exp(s - m)
        denom = jnp.sum(p, axis=-1, keepdims=True) + jnp.exp(sk - m)
        o = _dot(p.astype(BF16), v) / denom
        o_ref[:, g * HEAD:(g + 1) * HEAD] = o.astype(BF16)


def _attn(cfg, q_rot, k_rot, proj, sink):
    offs, _ = cfg.offsets
    G = cfg.AH // cfg.KVH
    blk = WINDOW
    NB = cfg.S // blk
    kcol = 0
    vcol = offs[7] // HEAD
    prev = lambda n: jnp.maximum(n - 1, 0)
    nxt = lambda n: jnp.minimum(n + 1, NB - 1)
    return pl.pallas_call(
        functools.partial(_attn_kernel, G=G, NB=NB),
        grid=(cfg.B, cfg.KVH, NB),
        in_specs=[pl.BlockSpec(memory_space=pltpu.SMEM),
                  pl.BlockSpec((blk, G * HEAD), lambda b, h, n: (b * NB + n, h)),
                  pl.BlockSpec((blk, HEAD), lambda b, h, n: (b * NB + prev(n), kcol + h)),
                  pl.BlockSpec((blk, HEAD), lambda b, h, n: (b * NB + n, kcol + h)),
                  pl.BlockSpec((blk, HEAD), lambda b, h, n: (b * NB + nxt(n), kcol + h)),
                  pl.BlockSpec((blk, HEAD), lambda b, h, n: (b * NB + prev(n), vcol + h)),
                  pl.BlockSpec((blk, HEAD), lambda b, h, n: (b * NB + n, vcol + h)),
                  pl.BlockSpec((blk, HEAD), lambda b, h, n: (b * NB + nxt(n), vcol + h))],
        out_specs=pl.BlockSpec((blk, G * HEAD), lambda b, h, n: (b * NB + n, h)),
        out_shape=jax.ShapeDtypeStruct((cfg.T, cfg.AH * HEAD), BF16),
        compiler_params=_params("arbitrary", "arbitrary", "arbitrary"),
        name="attn",
    )(sink, q_rot, k_rot, k_rot, k_rot, proj, proj, proj)


def _lower_bound(lbl_ref, d, layer):
    n = lbl_ref.shape[1]
    rows = [lbl_ref[d, j:j + 1, :] for j in range(n)]
    m = functools.reduce(jnp.maximum, rows)
    es = [jnp.exp(r - m) for r in rows]
    return sum(es[:layer + 1]) / sum(es)


def _hgrn_pair(qs, k, logf, v, st, tri2, fwd):
    L = HG_CHUNK
    ref_i, last_i = (L // 2, L - 1) if fwd else (L - 1 - L // 2, 0)
    hi = logf.astype(BF16)
    lo = (logf - hi.astype(F32)).astype(BF16)
    both = _dot(tri2.astype(BF16), jnp.concatenate([hi, lo], axis=1))
    b = both[:, :HEAD] + both[:, HEAD:]

    def per_chunk(rows):
        return jnp.concatenate([jnp.broadcast_to(r, (L, HEAD)) for r in rows], axis=0)

    brefs = [b[c * L + ref_i:c * L + ref_i + 1, :] for c in (0, 1)]
    blasts = [b[c * L + last_i:c * L + last_i + 1, :] for c in (0, 1)]
    d = b - per_chunk(brefs)
    q_in = qs * jnp.exp(d)
    k_in = k * jnp.exp(-d)
    a = jnp.where(tri2 > 0.0, _dot_nt(q_in.astype(BF16), k_in.astype(BF16)), 0.0)
    v_bf = v.astype(BF16)
    o_intra = _dot(a.astype(BF16), v_bf)
    q_dec = (q_in * per_chunk([jnp.exp(r) for r in brefs])).astype(BF16)
    k_dec = k_in * per_chunk([jnp.exp(l - r) for l, r in zip(blasts, brefs)])
    first = lax.broadcasted_iota(jnp.int32, k_dec.shape, 0) < L
    zero = jnp.zeros_like(k_dec)
    k_blk = jnp.concatenate([jnp.where(first, k_dec, zero), jnp.where(first, zero, k_dec)], axis=1).astype(BF16)
    cs = _dot_tn(v_bf, k_blk)
    o_inter = [None, None]
    for c in ((0, 1) if fwd else (1, 0)):
        r = slice(c * L, (c + 1) * L)
        o_inter[c] = _dot_nt(q_dec[r], st.astype(BF16))
        st = st * jnp.exp(blasts[c]) + cs[:, c * HEAD:(c + 1) * HEAD]
    return o_intra + jnp.concatenate(o_inter, axis=0), st


def _hgrn_kernel(lbl_ref, gain_ref, qf_ref, vf_ref, zf_ref, qb_ref, vb_ref, zb_ref, g_ref, o_ref,
                 st_ref, of_ref, ob_ref, *, NT, R, HP, layer):
    i = pl.program_id(2)
    L = 2 * HG_CHUNK
    nc = R // L
    li = lax.broadcasted_iota(jnp.int32, (L, L), 0)
    mi = lax.broadcasted_iota(jnp.int32, (L, L), 1)
    same = (li // HG_CHUNK) == (mi // HG_CHUNK)
    tri_f = jnp.where(same & (mi <= li), 1.0, 0.0).astype(F32)
    tri_b = jnp.where(same & (mi >= li), 1.0, 0.0).astype(F32)

    @pl.when(i == 0)
    def _():
        st_ref[...] = jnp.zeros_like(st_ref)

    lanes = [slice(p * HEAD, (p + 1) * HEAD) for p in range(HP)]
    streams = []
    for p in range(HP):
        for d, (q_ref, v_ref, z_ref) in enumerate(((qf_ref, vf_ref, zf_ref), (qb_ref, vb_ref, zb_ref))):
            lb = _lower_bound(lbl_ref, d, layer)[:, lanes[p]]
            f = lb + (1.0 - lb) * jax.nn.sigmoid(z_ref[:, lanes[p]])
            streams.append(dict(p=p, d=d, qs=_silu(q_ref[:, lanes[p]]), k=1.0 - f, logf=jnp.log(f),
                                v=v_ref[:, lanes[p]], st=st_ref[p, d]))
    for c in range(nc):
        for sm in streams:
            fwd = sm["d"] == 0
            cc = c if fwd else nc - 1 - c
            r = slice(cc * L, (cc + 1) * L)
            o, sm["st"] = _hgrn_pair(sm["qs"][r], sm["k"][r], sm["logf"][r], sm["v"][r], sm["st"],
                                     tri_f if fwd else tri_b, fwd)
            tile = i if fwd else NT - 1 - i
            rows = pl.ds(pl.multiple_of(tile * R + cc * L, L), L)
            (of_ref if fwd else ob_ref)[rows, lanes[sm["p"]]] = o
    for sm in streams:
        st_ref[sm["p"], sm["d"]] = sm["st"]

    @pl.when(i == NT - 1)
    def _():
        for t in range(NT):
            rows = slice(t * R, (t + 1) * R)
            for p in range(HP):
                o = of_ref[rows, lanes[p]] + ob_ref[rows, lanes[p]]
                gate = _silu(g_ref[rows, lanes[p]])
                o_ref[rows, lanes[p]] = (_rms(o) * gain_ref[0, :, lanes[p]] * gate).astype(BF16)


def _hgrn(cfg, proj, lb_logits, gain3, layer):
    offs, _ = cfg.offsets
    HP = HGRN_HEADS_PER_STEP
    W = HP * HEAD
    assert cfg.HH % HP == 0 and all(o % W == 0 for o in offs[:5])
    qc, fc, bc, ic, gc = (o // W for o in offs[:5])
    R = _pick(cfg.S, 512)
    NT = cfg.S // R
    nl = lb_logits.shape[1]
    fwd = lambda col: pl.BlockSpec((R, W), lambda b, h, i: (b * NT + i, col + h))
    bwd = lambda col: pl.BlockSpec((R, W), lambda b, h, i: (b * NT + NT - 1 - i, col + h))
    return pl.pallas_call(
        functools.partial(_hgrn_kernel, NT=NT, R=R, HP=HP, layer=layer),
        grid=(cfg.B, cfg.HH // HP, NT),
        in_specs=[pl.BlockSpec((2, nl, W), lambda b, h, i: (0, 0, h)),
                  pl.BlockSpec((1, 1, W), lambda b, h, i: (h, 0, 0)),
                  fwd(qc), fwd(ic), fwd(fc), bwd(qc), bwd(ic), bwd(bc),
                  pl.BlockSpec((cfg.S, W), lambda b, h, i: (b, gc + h))],
        out_specs=pl.BlockSpec((cfg.S, W), lambda b, h, i: (b, h)),
        out_shape=jax.ShapeDtypeStruct((cfg.T, cfg.HH * HEAD), BF16),
        scratch_shapes=[pltpu.VMEM((HP, 2, HEAD, HEAD), F32),
                        pltpu.VMEM((cfg.S, W), F32), pltpu.VMEM((cfg.S, W), F32)],
        compiler_params=_params("arbitrary", "arbitrary", "arbitrary"),
        name="hgrn",
    )(lb_logits, gain3, proj, proj, proj, proj, proj, proj, proj)


def _merge_kernel(oa_ref, ob_ref, ga_ref, gb_ref, wa_ref, wb_ref, o_ref, wabf_ref, wbbf_ref):
    @pl.when(pl.program_id(1) == 0)
    def _():
        wabf_ref[...] = wa_ref[...].astype(BF16)
        wbbf_ref[...] = wb_ref[...].astype(BF16)

    ya = _dot(oa_ref[...], wabf_ref[...])
    yb = _dot(ob_ref[...], wbbf_ref[...])
    o_ref[...] = (jax.nn.sigmoid(ga_ref[...]) * ya + jax.nn.sigmoid(gb_ref[...]) * yb).astype(BF16)


def _merge(cfg, oa, ob, proj, wa, wb):
    offs, _ = cfg.offsets
    tn = _pick(cfg.D, 512)
    assert offs[8] % tn == 0 and offs[9] % tn == 0
    ga0, gb0 = offs[8] // tn, offs[9] // tn
    tm = _pick(cfg.T, 512)
    ka, kb = oa.shape[1], ob.shape[1]
    return pl.pallas_call(
        _merge_kernel,
        grid=(cfg.D // tn, cfg.T // tm),
        in_specs=[pl.BlockSpec((tm, ka), lambda j, i: (i, 0)),
                  pl.BlockSpec((tm, kb), lambda j, i: (i, 0)),
                  pl.BlockSpec((tm, tn), lambda j, i: (i, ga0 + j)),
                  pl.BlockSpec((tm, tn), lambda j, i: (i, gb0 + j)),
                  pl.BlockSpec((ka, tn), lambda j, i: (0, j)),
                  pl.BlockSpec((kb, tn), lambda j, i: (0, j))],
        out_specs=pl.BlockSpec((tm, tn), lambda j, i: (i, j)),
        out_shape=jax.ShapeDtypeStruct((cfg.T, cfg.D), BF16),
        scratch_shapes=[pltpu.VMEM((ka, tn), BF16), pltpu.VMEM((kb, tn), BF16)],
        compiler_params=_params("arbitrary", "arbitrary"),
        name="merge",
    )(oa, ob, proj, proj, wa, wb)


def _outproj_kernel(a_ref, w_ref, x_ref, mod_ref, gpost_ref, gpre_ref, wr_ref,
                    x1_ref, h_ref, aff_ref, wbf_ref, *, E):
    @pl.when(pl.program_id(0) == 0)
    def _():
        wbf_ref[...] = w_ref[...].astype(BF16)

    y = _dot(a_ref[...], wbf_ref[...])
    x1 = x_ref[...] + mod_ref[0, 2:3, :] * (_rms(y) * gpost_ref[...])
    x1_ref[...] = x1
    h = _rms(x1) * gpre_ref[...] * (1.0 + mod_ref[0, 4:5, :]) + mod_ref[0, 3:4, :]
    h_hi = h.astype(BF16)
    h_ref[...] = h_hi
    h_lo = (h - h_hi.astype(F32)).astype(BF16)
    wr = wr_ref[...]
    w_hi = wr.astype(BF16)
    w_lo = (wr - w_hi.astype(F32)).astype(BF16)
    logits = _dot(h_hi, w_hi) + _dot(h_lo, w_hi) + _dot(h_hi, w_lo)
    lane = lax.broadcasted_iota(jnp.int32, logits.shape, 1)
    logits = jnp.where(lane < E, logits, -jnp.inf)
    p = jnp.exp(logits - jnp.max(logits, axis=-1, keepdims=True))
    aff_ref[...] = p / jnp.sum(p, axis=-1, keepdims=True)


def _outproj(cfg, merged, w, x2, mod3, gpost, gpre, wr_pad):
    tm = _pick(cfg.S, 256)
    per_b = cfg.S // tm
    D = cfg.D
    const = lambda i: (0, 0)
    row = lambda i: (i, 0)
    return pl.pallas_call(
        functools.partial(_outproj_kernel, E=cfg.E),
        grid=(cfg.T // tm,),
        in_specs=[pl.BlockSpec((tm, D), row),
                  pl.BlockSpec((D, D), const, pipeline_mode=pl.Buffered(1)),
                  pl.BlockSpec((tm, D), row),
                  pl.BlockSpec((1, 6, D), lambda i: (i // per_b, 0, 0)),
                  pl.BlockSpec((1, D), const),
                  pl.BlockSpec((1, D), const),
                  pl.BlockSpec((D, LANES), const)],
        out_specs=[pl.BlockSpec((tm, D), row),
                   pl.BlockSpec((tm, D), row),
                   pl.BlockSpec((tm, LANES), row)],
        out_shape=[jax.ShapeDtypeStruct((cfg.T, D), F32),
                   jax.ShapeDtypeStruct((cfg.T, D), BF16),
                   jax.ShapeDtypeStruct((cfg.T, LANES), F32)],
        scratch_shapes=[pltpu.VMEM((D, D), BF16)],
        compiler_params=_params("arbitrary"),
        name="outproj",
    )(merged, w, x2, mod3, gpost, gpre, wr_pad)


def _route_kernel(aff_ref, posm_ref, postm_ref, offs_ref, a_ref, *, E, S, C):
    nt = S // LANES
    for t in range(nt):
        a_ref[:, t * LANES:(t + 1) * LANES] = aff_ref[t * LANES:(t + 1) * LANES, :].T[:E, :]

    cap = jnp.float32(C)

    def bisect(_, carry):
        lo, hi = carry
        mid = 0.5 * (lo + hi)
        cnt = jnp.sum(jnp.where(a_ref[...] >= mid, 1.0, 0.0), axis=1, keepdims=True)
        ge = cnt >= cap
        return jnp.where(ge, mid, lo), jnp.where(ge, hi, mid)

    lo, hi = lax.fori_loop(0, BISECT_ITERS, bisect,
                           (jnp.zeros((E, 1), F32), jnp.full((E, 1), 2.0, F32)))
    need = cap - jnp.sum(jnp.where(a_ref[...] >= hi, 1.0, 0.0), axis=1, keepdims=True)

    ui = lax.broadcasted_iota(jnp.int32, (LANES, LANES), 0)
    uj = lax.broadcasted_iota(jnp.int32, (LANES, LANES), 1)
    upper = jnp.where(ui <= uj, 1.0, 0.0).astype(BF16)
    lane = lax.broadcasted_iota(jnp.int32, (E, LANES), 1)
    run_tie = jnp.zeros((E, 1), F32)
    run_sel = jnp.zeros((E, 1), F32)
    offs = jnp.zeros((E, LANES), F32)
    pad = jnp.full((LANES - E, LANES), -1.0, F32)
    for t in range(nt):
        cols = slice(t * LANES, (t + 1) * LANES)
        a = a_ref[:, cols]
        tie = (a >= lo) & (a < hi)
        tie_t = jnp.where(tie, 1.0, 0.0)
        tie_rank = _dot(tie_t.astype(BF16), upper) - tie_t + run_tie
        sel = (a >= hi) | (tie & (tie_rank < need))
        sel_t = jnp.where(sel, 1.0, 0.0)
        pos = _dot(sel_t.astype(BF16), upper) - sel_t + run_sel
        posm = jnp.where(sel, pos, -1.0)
        posm_ref[0, :, cols] = posm
        postm_ref[cols, :] = jnp.concatenate([posm, pad], axis=0).T
        offs = jnp.where(lane == t, run_sel, offs)
        run_tie = run_tie + jnp.sum(tie_t, axis=1, keepdims=True)
        run_sel = run_sel + jnp.sum(sel_t, axis=1, keepdims=True)
    offs = jnp.where(lane == nt, run_sel, offs)
    offs_ref[0] = offs.astype(jnp.int32)


def _route(cfg, aff_tm):
    E, S = cfg.E, cfg.S
    assert S // LANES < LANES and E % 8 == 0
    return pl.pallas_call(
        functools.partial(_route_kernel, E=E, S=S, C=cfg.C),
        grid=(cfg.B,),
        in_specs=[pl.BlockSpec((S, LANES), lambda b: (b, 0))],
        out_specs=[pl.BlockSpec((1, E, S), lambda b: (b, 0, 0)),
                   pl.BlockSpec((S, LANES), lambda b: (b, 0)),
                   pl.BlockSpec((1, E, LANES), lambda b: (b, 0, 0))],
        out_shape=[jax.ShapeDtypeStruct((cfg.B, E, S), F32),
                   jax.ShapeDtypeStruct((cfg.T, LANES), F32),
                   jax.ShapeDtypeStruct((cfg.B, E, LANES), jnp.int32)],
        scratch_shapes=[pltpu.VMEM((E, S), F32)],
        compiler_params=_params("arbitrary"),
        name="route",
    )(aff_tm)


def _overlap(offs_ref, base, token_block, slot_chunk):
    per = TOKEN_BLOCK // LANES
    first = offs_ref[base + token_block * per]
    end = offs_ref[base + (token_block + 1) * per]
    return (first < (slot_chunk + 1) * SLOT_CHUNK) & (end > slot_chunk * SLOT_CHUNK)


def _gather_kernel(offs_ref, posm_ref, h_ref, x_ref, acc_ref, *, E, S, C):
    base = (pl.program_id(0) * E + pl.program_id(1)) * LANES
    acc_ref[...] = jnp.zeros_like(acc_ref)
    for jc in range(C // SLOT_CHUNK):
        slot = (lax.broadcasted_iota(jnp.int32, (SLOT_CHUNK, TOKEN_BLOCK), 0) + jc * SLOT_CHUNK).astype(F32)
        for kb in range(S // TOKEN_BLOCK):
            @pl.when(_overlap(offs_ref, base, kb, jc))
            def _():
                toks = slice(kb * TOKEN_BLOCK, (kb + 1) * TOKEN_BLOCK)
                onehot = jnp.where(posm_ref[0, :, toks] == slot, 1.0, 0.0).astype(BF16)
                rows = slice(jc * SLOT_CHUNK, (jc + 1) * SLOT_CHUNK)
                acc_ref[rows, :] += _dot(onehot, h_ref[toks, :])
    x_ref[0, 0] = acc_ref[...].astype(BF16)


def _gather(cfg, offs_flat, posm3, h2):
    B, E, S, C, D = cfg.B, cfg.E, cfg.S, cfg.C, cfg.D
    grid_spec = pltpu.PrefetchScalarGridSpec(
        num_scalar_prefetch=1,
        grid=(B, E),
        in_specs=[pl.BlockSpec((1, 1, S), lambda b, e, offs: (b * E + e, 0, 0)),
                  pl.BlockSpec((S, D), lambda b, e, offs: (b, 0))],
        out_specs=pl.BlockSpec((1, 1, C, D), lambda b, e, offs: (b, e, 0, 0)),
        scratch_shapes=[pltpu.VMEM((C, D), F32)],
    )
    return pl.pallas_call(
        functools.partial(_gather_kernel, E=E, S=S, C=C),
        grid_spec=grid_spec,
        out_shape=jax.ShapeDtypeStruct((B, E, C, D), BF16),
        compiler_params=_params("arbitrary", "arbitrary"),
        name="gather",
    )(offs_flat, posm3, h2)


def _ffn_kernel(x_ref, wg_ref, wu_ref, wd_ref, y_ref, acc_ref):
    f = pl.program_id(1)

    @pl.when(f == 0)
    def _():
        acc_ref[...] = jnp.zeros_like(acc_ref)

    wg = wg_ref[0].astype(BF16)
    wu = wu_ref[0].astype(BF16)
    wd = wd_ref[0].astype(BF16)
    for b in range(x_ref.shape[0]):
        x = x_ref[b, 0]
        a = (_silu(_dot(x, wg)) * _dot(x, wu)).astype(BF16)
        acc_ref[b] += _dot(a, wd)

    @pl.when(f == pl.num_programs(1) - 1)
    def _():
        y_ref[:, 0] = acc_ref[...].astype(BF16)


def _ffn(cfg, xe, wg, wu, wd):
    B, E, C, D, FF = cfg.B, cfg.E, cfg.C, cfg.D, cfg.FF
    tf = _pick(FF, 256)
    return pl.pallas_call(
        _ffn_kernel,
        grid=(E, FF // tf),
        in_specs=[pl.BlockSpec((B, 1, C, D), lambda e, f: (0, e, 0, 0)),
                  pl.BlockSpec((1, D, tf), lambda e, f: (e, 0, f)),
                  pl.BlockSpec((1, D, tf), lambda e, f: (e, 0, f)),
                  pl.BlockSpec((1, tf, D), lambda e, f: (e, f, 0))],
        out_specs=pl.BlockSpec((B, 1, C, D), lambda e, f: (0, e, 0, 0)),
        out_shape=jax.ShapeDtypeStruct((B, E, C, D), BF16),
        scratch_shapes=[pltpu.VMEM((B, C, D), F32)],
        compiler_params=_params("arbitrary", "arbitrary"),
        name="ffn",
    )(xe, wg, wu, wd)


def _combine_kernel(offs_ref, pos_ref, aff_ref, y_ref, x1_ref, mod_ref, g_ref, o_ref, z_ref, acc_ref, *, E, C):
    b, kb = pl.program_id(0), pl.program_id(1)
    W, TB = COMBINE_WINDOW, COMBINE_TOKENS
    per = TB // LANES

    def window(e):
        base = (b * E + e) * LANES + kb * per
        first, end = offs_ref[base], offs_ref[base + per]
        start = jnp.minimum((first // BF16_ROWS) * BF16_ROWS, C - W)
        return pl.multiple_of(start, BF16_ROWS), end

    starts = []
    for e in range(E):
        start, _ = window(e)
        z_ref[e * W:(e + 1) * W, :] = y_ref[0, e, pl.ds(start, W), :]
        starts.append(start)

    lane = lax.broadcasted_iota(jnp.int32, (TB, LANES), 1)
    seg = lane // W
    off = lane - seg * W
    group = LANES // W
    pieces = []
    for p in range(E // group):
        slot = jnp.zeros((TB, LANES), jnp.int32)
        pos = jnp.full((TB, LANES), -1.0, F32)
        aff = jnp.zeros((TB, LANES), F32)
        for q in range(group):
            e = p * group + q
            mine = seg == q
            slot = jnp.where(mine, starts[e] + off, slot)
            pos = jnp.where(mine, pos_ref[:, e:e + 1], pos)
            aff = jnp.where(mine, aff_ref[:, e:e + 1], aff)
        pieces.append(jnp.where(pos == slot.astype(F32), aff, 0.0).astype(BF16))
    acc_ref[...] = _dot(jnp.concatenate(pieces, axis=1), z_ref[...])

    def spill(e, carry):
        start, end = window(e)
        n_extra = (jnp.maximum(end - start - W, 0) + W - 1) // W
        mine = lane == e
        pos = jnp.sum(jnp.where(mine, pos_ref[...], 0.0), axis=1, keepdims=True)
        aff = jnp.sum(jnp.where(mine, aff_ref[...], 0.0), axis=1, keepdims=True)

        def extra(w, c2):
            cur = start + (w + 1) * W
            s2 = pl.multiple_of(jnp.minimum(cur, C - W), BF16_ROWS)
            slot = s2 + lax.broadcasted_iota(jnp.int32, (TB, W), 1)
            m = jnp.where((pos == slot.astype(F32)) & (slot >= cur), aff, 0.0).astype(BF16)
            acc_ref[...] += _dot(m, y_ref[0, e, pl.ds(s2, W), :])
            return c2

        return lax.fori_loop(0, n_extra, extra, carry)

    lax.fori_loop(0, E, spill, 0)
    o_ref[...] = x1_ref[...] + mod_ref[0, 5:6, :] * (_rms(acc_ref[...]) * g_ref[...])


def _combine(cfg, offs_flat, pos_tm, aff_tm, y, x1, mod3, g):
    B, E, S, C, D = cfg.B, cfg.E, cfg.S, cfg.C, cfg.D
    W, TB = COMBINE_WINDOW, COMBINE_TOKENS
    assert LANES % W == 0 and E % (LANES // W) == 0 and C % W == 0 and W % BF16_ROWS == 0 and S % TB == 0
    nk = S // TB
    tok = lambda b, k, offs: (b * nk + k, 0)
    grid_spec = pltpu.PrefetchScalarGridSpec(
        num_scalar_prefetch=1,
        grid=(B, nk),
        in_specs=[pl.BlockSpec((TB, LANES), tok),
                  pl.BlockSpec((TB, LANES), tok),
                  pl.BlockSpec((1, E, C, D), lambda b, k, offs: (b, 0, 0, 0), pipeline_mode=pl.Buffered(1)),
                  pl.BlockSpec((TB, D), tok),
                  pl.BlockSpec((1, 6, D), lambda b, k, offs: (b, 0, 0)),
                  pl.BlockSpec((1, D), lambda b, k, offs: (0, 0))],
        out_specs=pl.BlockSpec((TB, D), tok),
        scratch_shapes=[pltpu.VMEM((E * W, D), BF16), pltpu.VMEM((TB, D), F32)],
    )
    return pl.pallas_call(
        functools.partial(_combine_kernel, E=E, C=C),
        grid_spec=grid_spec,
        out_shape=jax.ShapeDtypeStruct((cfg.T, D), F32),
        compiler_params=_params("arbitrary", "arbitrary"),
        name="combine",
    )(offs_flat, pos_tm, aff_tm, y, x1, mod3, g)


def _layer(cfg, layer, x2, c_pad, pos2, inv, sgn, w_ada, b_ada, g_pre_mix, g_post_mix, g_pre_ffn,
           g_post_ffn, w_in, hg_lb_logits, hg_out_norm, attn_sink, w_branch_a, w_branch_b, w_out,
           w_router, w_exp_gate, w_exp_up, w_exp_down):
    D, E = cfg.D, cfg.E
    row = lambda v: v.reshape(1, -1)
    mod = _ada(c_pad, w_ada[layer], row(b_ada[layer]))[:cfg.B]
    mod3 = mod.reshape(cfg.B, 6, D)
    h = _prenorm(cfg, x2, row(g_pre_mix[layer]), mod3)
    proj = _inproj(h, w_in[layer])
    q_rot, k_rot = _rope(cfg, proj, pos2, inv, sgn)
    att = _attn(cfg, q_rot, k_rot, proj, attn_sink[layer])
    gain3 = hg_out_norm[layer].reshape(cfg.HH // HGRN_HEADS_PER_STEP, 1, HGRN_HEADS_PER_STEP * HEAD)
    oa = _hgrn(cfg, proj, hg_lb_logits, gain3, layer)
    merged = _merge(cfg, oa, att, proj, w_branch_a[layer], w_branch_b[layer])
    wr_pad = jnp.pad(w_router[layer], ((0, 0), (0, LANES - E)))
    x1, h2, aff_tm = _outproj(cfg, merged, w_out[layer], x2, mod3, row(g_post_mix[layer]),
                              row(g_pre_ffn[layer]), wr_pad)
    posm, pos_tm, offs = _route(cfg, aff_tm)
    offs_flat = offs.reshape(-1)
    xe = _gather(cfg, offs_flat, posm.reshape(cfg.B * E, 1, cfg.S), h2)
    ye = _ffn(cfg, xe, w_exp_gate[layer], w_exp_up[layer], w_exp_down[layer])
    return _combine(cfg, offs_flat, pos_tm, aff_tm, ye, x1, mod3, row(g_post_ffn[layer]))


def _block(cfg, x, c, positions, w_ada, *rest):
    depth = w_ada.shape[0]
    x2 = x.reshape(cfg.T, cfg.D)
    c_pad = jnp.pad(c, ((0, 8 - cfg.B), (0, 0)))
    pos2 = positions.reshape(cfg.T, 1)
    half = HEAD // 2
    inv_half = ROPE_THETA ** (-jnp.arange(half, dtype=F32) / half)
    inv = jnp.concatenate([inv_half, inv_half]).reshape(1, HEAD)
    sgn = jnp.concatenate([-jnp.ones((half,), F32), jnp.ones((half,), F32)]).reshape(1, HEAD)
    for layer in range(depth):
        x2 = _layer(cfg, layer, x2, c_pad, pos2, inv, sgn, w_ada, *rest)
    return x2.reshape(cfg.B, cfg.S, cfg.D)


def kernel(x, c, positions, w_ada, b_ada, g_pre_mix, g_post_mix, g_pre_ffn, g_post_ffn, w_in,
           hg_lb_logits, hg_out_norm, attn_sink, w_branch_a, w_branch_b, w_out, w_router,
           w_exp_gate, w_exp_up, w_exp_down):
    B, S, D = x.shape
    cfg = Cfg(B=B, S=S, D=D,
              HH=hg_out_norm.shape[1], AH=attn_sink.shape[1],
              KVH=(w_in.shape[2] - 5 * hg_out_norm.shape[1] * HEAD - attn_sink.shape[1] * HEAD - 2 * D) // (2 * HEAD),
              E=w_router.shape[2], FF=w_exp_gate.shape[3])
    return _block(cfg, x, c, positions, w_ada, b_ada, g_pre_mix, g_post_mix, g_pre_ffn, g_post_ffn,
                  w_in, hg_lb_logits, hg_out_norm, attn_sink, w_branch_a, w_branch_b, w_out,
                  w_router, w_exp_gate, w_exp_up, w_exp_down)
```

```python
import functools
from typing import NamedTuple

import jax
import jax.numpy as jnp
from jax import lax
from jax.experimental import pallas as pl
from jax.experimental.pallas import tpu as pltpu

F32 = jnp.float32
BF16 = jnp.bfloat16

LANES = 128
HEAD = 128
HG_CHUNK = 64
WINDOW = 128
ROPE_THETA = 10000.0
NORM_EPS = 1e-6
CAPACITY_FACTOR = 2
VMEM_LIMIT_BYTES = 56 * 1024 * 1024
SEARCH_ROUNDS = 76
HGRN_HEADS_PER_STEP = 2
COMBINE_TOKENS = 256
COMBINE_WINDOW = 64
GATHER_EXPERTS = 8
BF16_ROWS = 16


class Cfg(NamedTuple):
    B: int
    S: int
    D: int
    HH: int
    AH: int
    KVH: int
    E: int
    FF: int

    @property
    def T(self):
        return self.B * self.S

    @property
    def C(self):
        return CAPACITY_FACTOR * self.S // self.E

    @property
    def offsets(self):
        hq = self.HH * HEAD
        widths = (hq, hq, hq, hq, hq, self.AH * HEAD, self.KVH * HEAD, self.KVH * HEAD, self.D, self.D)
        offs, o = [], 0
        for w in widths:
            offs.append(o)
            o += w
        return tuple(offs), o


def _params(*sem):
    return pltpu.CompilerParams(dimension_semantics=sem, vmem_limit_bytes=VMEM_LIMIT_BYTES)


def _pick(n, target):
    if n <= target:
        return n
    t = (target // LANES) * LANES
    while t > LANES and n % t:
        t -= LANES
    return t


def _dot(a, b):
    return jnp.dot(a, b, preferred_element_type=F32)


def _dot_nt(a, b):
    return lax.dot_general(a, b, (((1,), (1,)), ((), ())), preferred_element_type=F32)


def _dot_tn(a, b):
    return lax.dot_general(a, b, (((0,), (0,)), ((), ())), preferred_element_type=F32)


def _silu(x):
    return x * jax.nn.sigmoid(x)


def _rms(x):
    return x * lax.rsqrt(jnp.mean(x * x, axis=-1, keepdims=True) + NORM_EPS)


def _ada_kernel(c_ref, w_ref, b_ref, o_ref):
    a = _silu(c_ref[...]).astype(BF16)
    o_ref[...] = _dot(a, w_ref[...].astype(BF16)) + b_ref[...]


def _ada(c_pad, w, b):
    rows, d = c_pad.shape
    n = w.shape[1]
    tn = _pick(n, 1024)
    return pl.pallas_call(
        _ada_kernel,
        grid=(n // tn,),
        in_specs=[pl.BlockSpec((rows, d), lambda j: (0, 0)),
                  pl.BlockSpec((d, tn), lambda j: (0, j)),
                  pl.BlockSpec((1, tn), lambda j: (0, j))],
        out_specs=pl.BlockSpec((rows, tn), lambda j: (0, j)),
        out_shape=jax.ShapeDtypeStruct((rows, n), F32),
        compiler_params=_params("arbitrary"),
        name="ada",
    )(c_pad, w, b)


def _prenorm_kernel(x_ref, g_ref, mod_ref, o_ref):
    y = _rms(x_ref[...]) * g_ref[...]
    shift = mod_ref[0, 0:1, :]
    scale = mod_ref[0, 1:2, :]
    o_ref[...] = (y * (1.0 + scale) + shift).astype(BF16)


def _prenorm(cfg, x2, g, mod3):
    tm = _pick(cfg.S, 256)
    per_b = cfg.S // tm
    return pl.pallas_call(
        _prenorm_kernel,
        grid=(cfg.T // tm,),
        in_specs=[pl.BlockSpec((tm, cfg.D), lambda i: (i, 0)),
                  pl.BlockSpec((1, cfg.D), lambda i: (0, 0)),
                  pl.BlockSpec((1, 6, cfg.D), lambda i: (i // per_b, 0, 0))],
        out_specs=pl.BlockSpec((tm, cfg.D), lambda i: (i, 0)),
        out_shape=jax.ShapeDtypeStruct((cfg.T, cfg.D), BF16),
        compiler_params=_params("arbitrary"),
        name="prenorm",
    )(x2, g, mod3)


def _inproj_kernel(a_ref, w_ref, o_ref, wbf_ref):
    @pl.when(pl.program_id(1) == 0)
    def _():
        wbf_ref[...] = w_ref[...].astype(BF16)

    o_ref[...] = _dot(a_ref[...], wbf_ref[...])


def _inproj(a, w):
    m, k = a.shape
    n = w.shape[1]
    tn = _pick(n, 1536)
    tm = _pick(m, 1024)
    return pl.pallas_call(
        _inproj_kernel,
        grid=(n // tn, m // tm),
        in_specs=[pl.BlockSpec((tm, k), lambda j, i: (i, 0)),
                  pl.BlockSpec((k, tn), lambda j, i: (0, j))],
        out_specs=pl.BlockSpec((tm, tn), lambda j, i: (i, j)),
        out_shape=jax.ShapeDtypeStruct((m, n), F32),
        scratch_shapes=[pltpu.VMEM((k, tn), BF16)],
        compiler_params=_params("arbitrary", "arbitrary"),
        name="inproj",
    )(a, w)


def _rope_kernel(q_ref, k_ref, pos_ref, inv_ref, sgn_ref, qo_ref, ko_ref, *, q_scale):
    ang = pos_ref[...].astype(F32) * inv_ref[...]
    cos = jnp.cos(ang)
    sin = jnp.sin(ang) * sgn_ref[...]

    def rotate(x_ref, o_ref, scale):
        for h in range(x_ref.shape[1] // HEAD):
            cols = slice(h * HEAD, (h + 1) * HEAD)
            x = x_ref[:, cols]
            r = x * cos + pltpu.roll(x, HEAD // 2, 1) * sin
            o_ref[:, cols] = (r if scale is None else r * scale).astype(BF16)

    rotate(q_ref, qo_ref, q_scale)
    rotate(k_ref, ko_ref, None)


def _rope(cfg, proj, pos2, inv, sgn):
    offs, _ = cfg.offsets
    wq, wk = cfg.AH * HEAD, cfg.KVH * HEAD
    assert offs[5] % wq == 0 and offs[6] % wk == 0
    tm = _pick(cfg.T, 256)
    return pl.pallas_call(
        functools.partial(_rope_kernel, q_scale=HEAD ** -0.5),
        grid=(cfg.T // tm,),
        in_specs=[pl.BlockSpec((tm, wq), lambda i: (i, offs[5] // wq)),
                  pl.BlockSpec((tm, wk), lambda i: (i, offs[6] // wk)),
                  pl.BlockSpec((tm, 1), lambda i: (i, 0)),
                  pl.BlockSpec((1, HEAD), lambda i: (0, 0)),
                  pl.BlockSpec((1, HEAD), lambda i: (0, 0))],
        out_specs=[pl.BlockSpec((tm, wq), lambda i: (i, 0)),
                   pl.BlockSpec((tm, wk), lambda i: (i, 0))],
        out_shape=[jax.ShapeDtypeStruct((cfg.T, wq), BF16),
                   jax.ShapeDtypeStruct((cfg.T, wk), BF16)],
        compiler_params=_params("arbitrary"),
        name="rope",
    )(proj, proj, pos2, inv, sgn)


def _attn_kernel(sink_ref, q_ref, kp_ref, kc_ref, kn_ref, vp_ref, vc_ref, vn_ref, o_ref, *, G, NB):
    kvh = pl.program_id(1)
    n = pl.program_id(2)
    blk = WINDOW
    k = jnp.concatenate([kp_ref[...], kc_ref[...], kn_ref[...]], axis=0)
    v = jnp.concatenate([vp_ref[...], vc_ref[...], vn_ref[...]], axis=0).astype(BF16)
    q = jnp.concatenate([q_ref[:, g * HEAD:(g + 1) * HEAD] for g in range(G)], axis=0)
    row = lax.broadcasted_iota(jnp.int32, (G * blk, 3 * blk), 0)
    qi = row - (row // blk) * blk
    kj = lax.broadcasted_iota(jnp.int32, (G * blk, 3 * blk), 1)
    valid = jnp.abs(kj - blk - qi) <= WINDOW
    valid = valid & ((kj >= blk) | (n > 0)) & ((kj < 2 * blk) | (n < NB - 1))
    head = lax.broadcasted_iota(jnp.int32, (G * blk, 1), 0) // blk
    sk = jnp.zeros((G * blk, 1), F32)
    for g in range(G):
        sk = jnp.where(head == g, sink_ref[kvh * G + g], sk)
    s = jnp.where(valid, _dot_nt(q, k), -jnp.inf)
    m = jnp.maximum(jnp.max(s, axis=-1, keepdims=True), sk)
    p = jnp.exp(s - m)
    denom = jnp.sum(p, axis=-1, keepdims=True) + jnp.exp(sk - m)
    o = (_dot(p.astype(BF16), v) / denom).astype(BF16)
    for g in range(G):
        o_ref[:, g * HEAD:(g + 1) * HEAD] = o[g * blk:(g + 1) * blk]


def _attn(cfg, q_rot, k_rot, proj, sink):
    offs, _ = cfg.offsets
    G = cfg.AH // cfg.KVH
    blk = WINDOW
    NB = cfg.S // blk
    kcol = 0
    vcol = offs[7] // HEAD
    prev = lambda n: jnp.maximum(n - 1, 0)
    nxt = lambda n: jnp.minimum(n + 1, NB - 1)
    return pl.pallas_call(
        functools.partial(_attn_kernel, G=G, NB=NB),
        grid=(cfg.B, cfg.KVH, NB),
        in_specs=[pl.BlockSpec(memory_space=pltpu.SMEM),
                  pl.BlockSpec((blk, G * HEAD), lambda b, h, n: (b * NB + n, h)),
                  pl.BlockSpec((blk, HEAD), lambda b, h, n: (b * NB + prev(n), kcol + h)),
                  pl.BlockSpec((blk, HEAD), lambda b, h, n: (b * NB + n, kcol + h)),
                  pl.BlockSpec((blk, HEAD), lambda b, h, n: (b * NB + nxt(n), kcol + h)),
                  pl.BlockSpec((blk, HEAD), lambda b, h, n: (b * NB + prev(n), vcol + h)),
                  pl.BlockSpec((blk, HEAD), lambda b, h, n: (b * NB + n, vcol + h)),
                  pl.BlockSpec((blk, HEAD), lambda b, h, n: (b * NB + nxt(n), vcol + h))],
        out_specs=pl.BlockSpec((blk, G * HEAD), lambda b, h, n: (b * NB + n, h)),
        out_shape=jax.ShapeDtypeStruct((cfg.T, cfg.AH * HEAD), BF16),
        compiler_params=_params("arbitrary", "arbitrary", "arbitrary"),
        name="attn",
    )(sink, q_rot, k_rot, k_rot, k_rot, proj, proj, proj)


def _lower_bound(lbl_ref, d, layer):
    n = lbl_ref.shape[1]
    rows = [lbl_ref[d, j:j + 1, :] for j in range(n)]
    m = functools.reduce(jnp.maximum, rows)
    es = [jnp.exp(r - m) for r in rows]
    return sum(es[:layer + 1]) / sum(es)


def _hgrn_pair(qs, k, logf, v, st, tri2, fwd):
    L = HG_CHUNK
    ref_i, last_i = (L // 2, L - 1) if fwd else (L - 1 - L // 2, 0)
    hi = logf.astype(BF16)
    lo = (logf - hi.astype(F32)).astype(BF16)
    both = _dot(tri2.astype(BF16), jnp.concatenate([hi, lo], axis=1))
    b = both[:, :HEAD] + both[:, HEAD:]

    def per_chunk(rows):
        return jnp.concatenate([jnp.broadcast_to(r, (L, HEAD)) for r in rows], axis=0)

    brefs = [b[c * L + ref_i:c * L + ref_i + 1, :] for c in (0, 1)]
    blasts = [b[c * L + last_i:c * L + last_i + 1, :] for c in (0, 1)]
    d = b - per_chunk(brefs)
    q_in = qs * jnp.exp(d)
    k_in = k * jnp.exp(-d)
    a = jnp.where(tri2 > 0.0, _dot_nt(q_in.astype(BF16), k_in.astype(BF16)), 0.0)
    v_bf = v.astype(BF16)
    o_intra = _dot(a.astype(BF16), v_bf)
    q_dec = (q_in * per_chunk([jnp.exp(r) for r in brefs])).astype(BF16)
    k_dec = k_in * per_chunk([jnp.exp(l - r) for l, r in zip(blasts, brefs)])
    first = lax.broadcasted_iota(jnp.int32, k_dec.shape, 0) < L
    zero = jnp.zeros_like(k_dec)
    k_blk = jnp.concatenate([jnp.where(first, k_dec, zero), jnp.where(first, zero, k_dec)], axis=1).astype(BF16)
    cs = _dot_tn(v_bf, k_blk)
    o_inter = [None, None]
    for c in ((0, 1) if fwd else (1, 0)):
        r = slice(c * L, (c + 1) * L)
        o_inter[c] = _dot_nt(q_dec[r], st.astype(BF16))
        st = st * jnp.exp(blasts[c]) + cs[:, c * HEAD:(c + 1) * HEAD]
    return o_intra + jnp.concatenate(o_inter, axis=0), st


def _hgrn_kernel(lbl_ref, gain_ref, qf_ref, vf_ref, zf_ref, qb_ref, vb_ref, zb_ref, g_ref, o_ref,
                 st_ref, of_ref, ob_ref, *, NT, R, HP, layer):
    i = pl.program_id(2)
    L = 2 * HG_CHUNK
    nc = R // L
    li = lax.broadcasted_iota(jnp.int32, (L, L), 0)
    mi = lax.broadcasted_iota(jnp.int32, (L, L), 1)
    same = (li // HG_CHUNK) == (mi // HG_CHUNK)
    tri_f = jnp.where(same & (mi <= li), 1.0, 0.0).astype(F32)
    tri_b = jnp.where(same & (mi >= li), 1.0, 0.0).astype(F32)

    @pl.when(i == 0)
    def _():
        st_ref[...] = jnp.zeros_like(st_ref)

    lanes = [slice(p * HEAD, (p + 1) * HEAD) for p in range(HP)]
    streams = []
    for p in range(HP):
        for d, (q_ref, v_ref, z_ref) in enumerate(((qf_ref, vf_ref, zf_ref), (qb_ref, vb_ref, zb_ref))):
            lb = _lower_bound(lbl_ref, d, layer)[:, lanes[p]]
            f = lb + (1.0 - lb) * jax.nn.sigmoid(z_ref[:, lanes[p]])
            streams.append(dict(p=p, d=d, qs=_silu(q_ref[:, lanes[p]]), k=1.0 - f, logf=jnp.log(f),
                                v=v_ref[:, lanes[p]], st=st_ref[p, d]))
    for c in range(nc):
        for sm in streams:
            fwd = sm["d"] == 0
            cc = c if fwd else nc - 1 - c
            r = slice(cc * L, (cc + 1) * L)
            o, sm["st"] = _hgrn_pair(sm["qs"][r], sm["k"][r], sm["logf"][r], sm["v"][r], sm["st"],
                                     tri_f if fwd else tri_b, fwd)
            tile = i if fwd else NT - 1 - i
            rows = pl.ds(pl.multiple_of(tile * R + cc * L, L), L)
            (of_ref if fwd else ob_ref)[rows, lanes[sm["p"]]] = o
    for sm in streams:
        st_ref[sm["p"], sm["d"]] = sm["st"]

    @pl.when(i == NT - 1)
    def _():
        for t in range(NT):
            rows = slice(t * R, (t + 1) * R)
            for p in range(HP):
                o = of_ref[rows, lanes[p]] + ob_ref[rows, lanes[p]]
                gate = _silu(g_ref[rows, lanes[p]])
                o_ref[rows, lanes[p]] = (_rms(o) * gain_ref[0, :, lanes[p]] * gate).astype(BF16)


def _hgrn(cfg, proj, lb_logits, gain3, layer):
    offs, _ = cfg.offsets
    HP = HGRN_HEADS_PER_STEP
    W = HP * HEAD
    assert cfg.HH % HP == 0 and all(o % W == 0 for o in offs[:5])
    qc, fc, bc, ic, gc = (o // W for o in offs[:5])
    R = _pick(cfg.S, 512)
    NT = cfg.S // R
    nl = lb_logits.shape[1]
    fwd = lambda col: pl.BlockSpec((R, W), lambda b, h, i: (b * NT + i, col + h))
    bwd = lambda col: pl.BlockSpec((R, W), lambda b, h, i: (b * NT + NT - 1 - i, col + h))
    return pl.pallas_call(
        functools.partial(_hgrn_kernel, NT=NT, R=R, HP=HP, layer=layer),
        grid=(cfg.B, cfg.HH // HP, NT),
        in_specs=[pl.BlockSpec((2, nl, W), lambda b, h, i: (0, 0, h)),
                  pl.BlockSpec((1, 1, W), lambda b, h, i: (h, 0, 0)),
                  fwd(qc), fwd(ic), fwd(fc), bwd(qc), bwd(ic), bwd(bc),
                  pl.BlockSpec((cfg.S, W), lambda b, h, i: (b, gc + h))],
        out_specs=pl.BlockSpec((cfg.S, W), lambda b, h, i: (b, h)),
        out_shape=jax.ShapeDtypeStruct((cfg.T, cfg.HH * HEAD), BF16),
        scratch_shapes=[pltpu.VMEM((HP, 2, HEAD, HEAD), F32),
                        pltpu.VMEM((cfg.S, W), F32), pltpu.VMEM((cfg.S, W), F32)],
        compiler_params=_params("arbitrary", "arbitrary", "arbitrary"),
        name="hgrn",
    )(lb_logits, gain3, proj, proj, proj, proj, proj, proj, proj)


def _merge_kernel(oa_ref, ob_ref, *refs, nsub):
    ga_refs, gb_refs = refs[:nsub], refs[nsub:2 * nsub]
    wa_ref, wb_ref, o_ref, wabf_ref, wbbf_ref = refs[2 * nsub:]

    @pl.when(pl.program_id(1) == 0)
    def _():
        wabf_ref[...] = wa_ref[...].astype(BF16)
        wbbf_ref[...] = wb_ref[...].astype(BF16)

    ya = _dot(oa_ref[...], wabf_ref[...])
    yb = _dot(ob_ref[...], wbbf_ref[...])
    gw = ya.shape[1] // nsub
    for u in range(nsub):
        cols = slice(u * gw, (u + 1) * gw)
        o_ref[:, cols] = (jax.nn.sigmoid(ga_refs[u][...]) * ya[:, cols]
                          + jax.nn.sigmoid(gb_refs[u][...]) * yb[:, cols]).astype(BF16)


def _merge(cfg, oa, ob, proj, wa, wb):
    offs, _ = cfg.offsets
    gw = _pick(cfg.D, 512)
    assert offs[8] % gw == 0 and offs[9] % gw == 0
    nsub = 2 if cfg.D % (2 * gw) == 0 else 1
    tn = nsub * gw
    ga0, gb0 = offs[8] // gw, offs[9] // gw
    tm = _pick(cfg.T, 512)
    ka, kb = oa.shape[1], ob.shape[1]
    gate = lambda c0, u: pl.BlockSpec((tm, gw), lambda j, i: (i, c0 + nsub * j + u))
    return pl.pallas_call(
        functools.partial(_merge_kernel, nsub=nsub),
        grid=(cfg.D // tn, cfg.T // tm),
        in_specs=[pl.BlockSpec((tm, ka), lambda j, i: (i, 0)),
                  pl.BlockSpec((tm, kb), lambda j, i: (i, 0)),
                  *[gate(ga0, u) for u in range(nsub)],
                  *[gate(gb0, u) for u in range(nsub)],
                  pl.BlockSpec((ka, tn), lambda j, i: (0, j)),
                  pl.BlockSpec((kb, tn), lambda j, i: (0, j))],
        out_specs=pl.BlockSpec((tm, tn), lambda j, i: (i, j)),
        out_shape=jax.ShapeDtypeStruct((cfg.T, cfg.D), BF16),
        scratch_shapes=[pltpu.VMEM((ka, tn), BF16), pltpu.VMEM((kb, tn), BF16)],
        compiler_params=_params("arbitrary", "arbitrary"),
        name="merge",
    )(oa, ob, *([proj] * (2 * nsub)), wa, wb)


def _outproj_kernel(a_ref, w_ref, x_ref, mod_ref, gpost_ref, gpre_ref, wr_ref,
                    x1_ref, h_ref, aff_ref, wbf_ref, *, E):
    @pl.when(pl.program_id(0) == 0)
    def _():
        wbf_ref[...] = w_ref[...].astype(BF16)

    y = _dot(a_ref[...], wbf_ref[...])
    x1 = x_ref[...] + mod_ref[0, 2:3, :] * (_rms(y) * gpost_ref[...])
    x1_ref[...] = x1
    h = _rms(x1) * gpre_ref[...] * (1.0 + mod_ref[0, 4:5, :]) + mod_ref[0, 3:4, :]
    h_hi = h.astype(BF16)
    h_ref[...] = h_hi
    h_lo = (h - h_hi.astype(F32)).astype(BF16)
    wr = wr_ref[...]
    w_hi = wr.astype(BF16)
    w_lo = (wr - w_hi.astype(F32)).astype(BF16)
    logits = _dot(h_hi, w_hi) + _dot(h_lo, w_hi) + _dot(h_hi, w_lo)
    lane = lax.broadcasted_iota(jnp.int32, logits.shape, 1)
    logits = jnp.where(lane < E, logits, -jnp.inf)
    p = jnp.exp(logits - jnp.max(logits, axis=-1, keepdims=True))
    aff_ref[...] = p / jnp.sum(p, axis=-1, keepdims=True)


def _outproj(cfg, merged, w, x2, mod3, gpost, gpre, wr_pad):
    tm = _pick(cfg.S, 256)
    per_b = cfg.S // tm
    D = cfg.D
    const = lambda i: (0, 0)
    row = lambda i: (i, 0)
    return pl.pallas_call(
        functools.partial(_outproj_kernel, E=cfg.E),
        grid=(cfg.T // tm,),
        in_specs=[pl.BlockSpec((tm, D), row),
                  pl.BlockSpec((D, D), const, pipeline_mode=pl.Buffered(1)),
                  pl.BlockSpec((tm, D), row),
                  pl.BlockSpec((1, 6, D), lambda i: (i // per_b, 0, 0)),
                  pl.BlockSpec((1, D), const),
                  pl.BlockSpec((1, D), const),
                  pl.BlockSpec((D, LANES), const)],
        out_specs=[pl.BlockSpec((tm, D), row),
                   pl.BlockSpec((tm, D), row),
                   pl.BlockSpec((tm, LANES), row)],
        out_shape=[jax.ShapeDtypeStruct((cfg.T, D), F32),
                   jax.ShapeDtypeStruct((cfg.T, D), BF16),
                   jax.ShapeDtypeStruct((cfg.T, LANES), F32)],
        scratch_shapes=[pltpu.VMEM((D, D), BF16)],
        compiler_params=_params("arbitrary"),
        name="outproj",
    )(merged, w, x2, mod3, gpost, gpre, wr_pad)


def _route_kernel(aff_ref, posm_ref, postm_ref, offs_ref, a_ref, *, E, S, C):
    nt = S // LANES
    for t in range(nt):
        a_ref[:, t * LANES:(t + 1) * LANES] = aff_ref[t * LANES:(t + 1) * LANES, :].T[:E, :]

    cap = jnp.float32(C)

    def narrow(_, carry):
        lo, hi = carry
        width = hi - lo
        a = a_ref[...]
        m1, m2, m3 = (lo + frac * width for frac in (0.25, 0.5, 0.75))
        g1, g2, g3 = (jnp.sum(jnp.where(a >= m, 1.0, 0.0), axis=1, keepdims=True) >= cap for m in (m1, m2, m3))
        new_lo = jnp.where(g3, m3, jnp.where(g2, m2, jnp.where(g1, m1, lo)))
        new_hi = jnp.where(g3, hi, jnp.where(g2, m3, jnp.where(g1, m2, m1)))
        return new_lo, new_hi

    lo, hi = lax.fori_loop(0, SEARCH_ROUNDS, narrow,
                           (jnp.zeros((E, 1), F32), jnp.full((E, 1), 2.0, F32)))
    need = cap - jnp.sum(jnp.where(a_ref[...] >= hi, 1.0, 0.0), axis=1, keepdims=True)

    ui = lax.broadcasted_iota(jnp.int32, (LANES, LANES), 0)
    uj = lax.broadcasted_iota(jnp.int32, (LANES, LANES), 1)
    upper = jnp.where(ui <= uj, 1.0, 0.0).astype(BF16)
    lane = lax.broadcasted_iota(jnp.int32, (E, LANES), 1)
    run_tie = jnp.zeros((E, 1), F32)
    run_sel = jnp.zeros((E, 1), F32)
    offs = jnp.zeros((E, LANES), F32)
    pad = jnp.full((LANES - E, LANES), -1.0, F32)
    for t in range(nt):
        cols = slice(t * LANES, (t + 1) * LANES)
        a = a_ref[:, cols]
        tie = (a >= lo) & (a < hi)
        tie_t = jnp.where(tie, 1.0, 0.0)
        tie_rank = _dot(tie_t.astype(BF16), upper) - tie_t + run_tie
        sel = (a >= hi) | (tie & (tie_rank < need))
        sel_t = jnp.where(sel, 1.0, 0.0)
        pos = _dot(sel_t.astype(BF16), upper) - sel_t + run_sel
        posm = jnp.where(sel, pos, -1.0)
        posm_ref[0, :, cols] = posm
        postm_ref[cols, :] = jnp.concatenate([posm, pad], axis=0).T
        offs = jnp.where(lane == t, run_sel, offs)
        run_tie = run_tie + jnp.sum(tie_t, axis=1, keepdims=True)
        run_sel = run_sel + jnp.sum(sel_t, axis=1, keepdims=True)
    offs = jnp.where(lane == nt, run_sel, offs)
    offs_ref[0] = offs.astype(jnp.int32)


def _route(cfg, aff_tm):
    E, S = cfg.E, cfg.S
    assert S // LANES < LANES and E % 8 == 0
    return pl.pallas_call(
        functools.partial(_route_kernel, E=E, S=S, C=cfg.C),
        grid=(cfg.B,),
        in_specs=[pl.BlockSpec((S, LANES), lambda b: (b, 0))],
        out_specs=[pl.BlockSpec((1, E, S), lambda b: (b, 0, 0)),
                   pl.BlockSpec((S, LANES), lambda b: (b, 0)),
                   pl.BlockSpec((1, E, LANES), lambda b: (b, 0, 0))],
        out_shape=[jax.ShapeDtypeStruct((cfg.B, E, S), F32),
                   jax.ShapeDtypeStruct((cfg.T, LANES), F32),
                   jax.ShapeDtypeStruct((cfg.B, E, LANES), jnp.int32)],
        scratch_shapes=[pltpu.VMEM((E, S), F32)],
        compiler_params=_params("arbitrary"),
        name="route",
    )(aff_tm)


def _slot_window(offs_ref, row, token_block, C):
    base = row * LANES + token_block * (COMBINE_TOKENS // LANES)
    first, end = offs_ref[base], offs_ref[base + COMBINE_TOKENS // LANES]
    start = jnp.minimum((first // BF16_ROWS) * BF16_ROWS, C - COMBINE_WINDOW)
    return pl.multiple_of(start, BF16_ROWS), end


def _extra_windows(start, end):
    return (jnp.maximum(end - start - COMBINE_WINDOW, 0) + COMBINE_WINDOW - 1) // COMBINE_WINDOW


def _gather_kernel(offs_ref, posm_ref, h_ref, x_ref, *, E, G, C):
    b, grp, kb = pl.program_id(0), pl.program_id(1), pl.program_id(2)
    W, TB = COMBINE_WINDOW, COMBINE_TOKENS

    @pl.when(kb == 0)
    def _():
        x_ref[...] = jnp.zeros_like(x_ref)

    slot0 = lax.broadcasted_iota(jnp.int32, (W, TB), 0)
    starts, pieces = [], []
    for j in range(G):
        start, _ = _slot_window(offs_ref, b * E + grp * G + j, kb, C)
        starts.append(start)
        pieces.append(jnp.where(posm_ref[j] == (start + slot0).astype(F32), 1.0, 0.0).astype(BF16))
    rows = _dot(jnp.concatenate(pieces, axis=0), h_ref[...]).astype(BF16)
    for j in range(G):
        win = pl.ds(starts[j], W)
        x_ref[0, j, win, :] = x_ref[0, j, win, :] + rows[j * W:(j + 1) * W]

    def spill(j, carry):
        start, end = _slot_window(offs_ref, b * E + grp * G + j, kb, C)

        def extra(w, c2):
            cur = start + (w + 1) * W
            s2 = pl.multiple_of(jnp.minimum(cur, C - W), BF16_ROWS)
            slot = s2 + slot0
            onehot = jnp.where((posm_ref[j] == slot.astype(F32)) & (slot >= cur), 1.0, 0.0).astype(BF16)
            win = pl.ds(s2, W)
            x_ref[0, j, win, :] = x_ref[0, j, win, :] + _dot(onehot, h_ref[...]).astype(BF16)
            return c2

        return lax.fori_loop(0, _extra_windows(start, end), extra, carry)

    lax.fori_loop(0, G, spill, 0)


def _gather(cfg, offs_flat, posm3, h2):
    B, E, S, C, D = cfg.B, cfg.E, cfg.S, cfg.C, cfg.D
    G, TB = min(E, GATHER_EXPERTS), COMBINE_TOKENS
    assert E % G == 0 and S % TB == 0 and C % COMBINE_WINDOW == 0
    nk = S // TB
    grid_spec = pltpu.PrefetchScalarGridSpec(
        num_scalar_prefetch=1,
        grid=(B, E // G, nk),
        in_specs=[pl.BlockSpec((G, 1, TB), lambda b, g, k, offs: (b * (E // G) + g, 0, k)),
                  pl.BlockSpec((TB, D), lambda b, g, k, offs: (b * nk + k, 0))],
        out_specs=pl.BlockSpec((1, G, C, D), lambda b, g, k, offs: (b, g, 0, 0)),
    )
    return pl.pallas_call(
        functools.partial(_gather_kernel, E=E, G=G, C=C),
        grid_spec=grid_spec,
        out_shape=jax.ShapeDtypeStruct((B, E, C, D), BF16),
        compiler_params=_params("arbitrary", "arbitrary", "arbitrary"),
        name="gather",
    )(offs_flat, posm3, h2)


def _ffn_kernel(x_ref, wg_ref, wu_ref, wd_ref, y_ref, acc_ref):
    f = pl.program_id(1)

    @pl.when(f == 0)
    def _():
        acc_ref[...] = jnp.zeros_like(acc_ref)

    wg = wg_ref[0].astype(BF16)
    wu = wu_ref[0].astype(BF16)
    wd = wd_ref[0].astype(BF16)
    for b in range(x_ref.shape[0]):
        x = x_ref[b, 0]
        a = (_silu(_dot(x, wg)) * _dot(x, wu)).astype(BF16)
        acc_ref[b] += _dot(a, wd)

    @pl.when(f == pl.num_programs(1) - 1)
    def _():
        y_ref[:, 0] = acc_ref[...].astype(BF16)


def _ffn(cfg, xe, wg, wu, wd):
    B, E, C, D, FF = cfg.B, cfg.E, cfg.C, cfg.D, cfg.FF
    tf = _pick(FF, 256)
    return pl.pallas_call(
        _ffn_kernel,
        grid=(E, FF // tf),
        in_specs=[pl.BlockSpec((B, 1, C, D), lambda e, f: (0, e, 0, 0)),
                  pl.BlockSpec((1, D, tf), lambda e, f: (e, 0, f)),
                  pl.BlockSpec((1, D, tf), lambda e, f: (e, 0, f)),
                  pl.BlockSpec((1, tf, D), lambda e, f: (e, f, 0))],
        out_specs=pl.BlockSpec((B, 1, C, D), lambda e, f: (0, e, 0, 0)),
        out_shape=jax.ShapeDtypeStruct((B, E, C, D), BF16),
        scratch_shapes=[pltpu.VMEM((B, C, D), F32)],
        compiler_params=_params("arbitrary", "arbitrary"),
        name="ffn",
    )(xe, wg, wu, wd)


def _combine_kernel(offs_ref, pos_ref, aff_ref, y_ref, x1_ref, mod_ref, g_ref, o_ref, z_ref, acc_ref, *, E, C):
    b, kb = pl.program_id(0), pl.program_id(1)
    W, TB = COMBINE_WINDOW, COMBINE_TOKENS

    starts = []
    for e in range(E):
        start, _ = _slot_window(offs_ref, b * E + e, kb, C)
        z_ref[e * W:(e + 1) * W, :] = y_ref[0, e, pl.ds(start, W), :]
        starts.append(start)

    lane = lax.broadcasted_iota(jnp.int32, (TB, LANES), 1)
    seg = lane // W
    off = lane - seg * W
    group = LANES // W
    pieces = []
    for p in range(E // group):
        slot = jnp.zeros((TB, LANES), jnp.int32)
        pos = jnp.full((TB, LANES), -1.0, F32)
        aff = jnp.zeros((TB, LANES), F32)
        for q in range(group):
            e = p * group + q
            mine = seg == q
            slot = jnp.where(mine, starts[e] + off, slot)
            pos = jnp.where(mine, pos_ref[:, e:e + 1], pos)
            aff = jnp.where(mine, aff_ref[:, e:e + 1], aff)
        pieces.append(jnp.where(pos == slot.astype(F32), aff, 0.0).astype(BF16))
    acc_ref[...] = _dot(jnp.concatenate(pieces, axis=1), z_ref[...])

    def spill(e, carry):
        start, end = _slot_window(offs_ref, b * E + e, kb, C)
        mine = lane == e
        pos = jnp.sum(jnp.where(mine, pos_ref[...], 0.0), axis=1, keepdims=True)
        aff = jnp.sum(jnp.where(mine, aff_ref[...], 0.0), axis=1, keepdims=True)

        def extra(w, c2):
            cur = start + (w + 1) * W
            s2 = pl.multiple_of(jnp.minimum(cur, C - W), BF16_ROWS)
            slot = s2 + lax.broadcasted_iota(jnp.int32, (TB, W), 1)
            m = jnp.where((pos == slot.astype(F32)) & (slot >= cur), aff, 0.0).astype(BF16)
            acc_ref[...] += _dot(m, y_ref[0, e, pl.ds(s2, W), :])
            return c2

        return lax.fori_loop(0, _extra_windows(start, end), extra, carry)

    lax.fori_loop(0, E, spill, 0)
    o_ref[...] = x1_ref[...] + mod_ref[0, 5:6, :] * (_rms(acc_ref[...]) * g_ref[...])


def _combine(cfg, offs_flat, pos_tm, aff_tm, y, x1, mod3, g):
    B, E, S, C, D = cfg.B, cfg.E, cfg.S, cfg.C, cfg.D
    W, TB = COMBINE_WINDOW, COMBINE_TOKENS
    assert LANES % W == 0 and E % (LANES // W) == 0 and C % W == 0 and W % BF16_ROWS == 0 and S % TB == 0
    nk = S // TB
    tok = lambda b, k, offs: (b * nk + k, 0)
    grid_spec = pltpu.PrefetchScalarGridSpec(
        num_scalar_prefetch=1,
        grid=(B, nk),
        in_specs=[pl.BlockSpec((TB, LANES), tok),
                  pl.BlockSpec((TB, LANES), tok),
                  pl.BlockSpec((1, E, C, D), lambda b, k, offs: (b, 0, 0, 0), pipeline_mode=pl.Buffered(1)),
                  pl.BlockSpec((TB, D), tok),
                  pl.BlockSpec((1, 6, D), lambda b, k, offs: (b, 0, 0)),
                  pl.BlockSpec((1, D), lambda b, k, offs: (0, 0))],
        out_specs=pl.BlockSpec((TB, D), tok),
        scratch_shapes=[pltpu.VMEM((E * W, D), BF16), pltpu.VMEM((TB, D), F32)],
    )
    return pl.pallas_call(
        functools.partial(_combine_kernel, E=E, C=C),
        grid_spec=grid_spec,
        out_shape=jax.ShapeDtypeStruct((cfg.T, D), F32),
        compiler_params=_params("arbitrary", "arbitrary"),
        name="combine",
    )(offs_flat, pos_tm, aff_tm, y, x1, mod3, g)


def _layer(cfg, layer, x2, c_pad, pos2, inv, sgn, w_ada, b_ada, g_pre_mix, g_post_mix, g_pre_ffn,
           g_post_ffn, w_in, hg_lb_logits, hg_out_norm, attn_sink, w_branch_a, w_branch_b, w_out,
           w_router, w_exp_gate, w_exp_up, w_exp_down):
    D, E = cfg.D, cfg.E
    row = lambda v: v.reshape(1, -1)
    mod = _ada(c_pad, w_ada[layer], row(b_ada[layer]))[:cfg.B]
    mod3 = mod.reshape(cfg.B, 6, D)
    h = _prenorm(cfg, x2, row(g_pre_mix[layer]), mod3)
    proj = _inproj(h, w_in[layer])
    q_rot, k_rot = _rope(cfg, proj, pos2, inv, sgn)
    att = _attn(cfg, q_rot, k_rot, proj, attn_sink[layer])
    gain3 = hg_out_norm[layer].reshape(cfg.HH // HGRN_HEADS_PER_STEP, 1, HGRN_HEADS_PER_STEP * HEAD)
    oa = _hgrn(cfg, proj, hg_lb_logits, gain3, layer)
    merged = _merge(cfg, oa, att, proj, w_branch_a[layer], w_branch_b[layer])
    wr_pad = jnp.pad(w_router[layer], ((0, 0), (0, LANES - E)))
    x1, h2, aff_tm = _outproj(cfg, merged, w_out[layer], x2, mod3, row(g_post_mix[layer]),
                              row(g_pre_ffn[layer]), wr_pad)
    posm, pos_tm, offs = _route(cfg, aff_tm)
    offs_flat = offs.reshape(-1)
    xe = _gather(cfg, offs_flat, posm.reshape(cfg.B * E, 1, cfg.S), h2)
    ye = _ffn(cfg, xe, w_exp_gate[layer], w_exp_up[layer], w_exp_down[layer])
    return _combine(cfg, offs_flat, pos_tm, aff_tm, ye, x1, mod3, row(g_post_ffn[layer]))


def _block(cfg, x, c, positions, w_ada, *rest):
    depth = w_ada.shape[0]
    x2 = x.reshape(cfg.T, cfg.D)
    c_pad = jnp.pad(c, ((0, 8 - cfg.B), (0, 0)))
    pos2 = positions.reshape(cfg.T, 1)
    half = HEAD // 2
    inv_half = ROPE_THETA ** (-jnp.arange(half, dtype=F32) / half)
    inv = jnp.concatenate([inv_half, inv_half]).reshape(1, HEAD)
    sgn = jnp.concatenate([-jnp.ones((half,), F32), jnp.ones((half,), F32)]).reshape(1, HEAD)
    for layer in range(depth):
        x2 = _layer(cfg, layer, x2, c_pad, pos2, inv, sgn, w_ada, *rest)
    return x2.reshape(cfg.B, cfg.S, cfg.D)


def kernel(x, c, positions, w_ada, b_ada, g_pre_mix, g_post_mix, g_pre_ffn, g_post_ffn, w_in,
           hg_lb_logits, hg_out_norm, attn_sink, w_branch_a, w_branch_b, w_out, w_router,
           w_exp_gate, w_exp_up, w_exp_down):
    B, S, D = x.shape
    cfg = Cfg(B=B, S=S, D=D,
              HH=hg_out_norm.shape[1], AH=attn_sink.shape[1],
              KVH=(w_in.shape[2] - 5 * hg_out_norm.shape[1] * HEAD - attn_sink.shape[1] * HEAD - 2 * D) // (2 * HEAD),
              E=w_router.shape[2], FF=w_exp_gate.shape[3])
    return _block(cfg, x, c, positions, w_ada, b_ada, g_pre_mix, g_post_mix, g_pre_ffn, g_post_ffn,
                  w_in, hg_lb_logits, hg_out_norm, attn_sink, w_branch_a, w_branch_b, w_out,
                  w_router, w_exp_gate, w_exp_up, w_exp_down)
```

```python
import functools
from typing import NamedTuple

import jax
import jax.numpy as jnp
from jax import lax
from jax.experimental import pallas as pl
from jax.experimental.pallas import tpu as pltpu

F32 = jnp.float32
BF16 = jnp.bfloat16

LANES = 128
HEAD = 128
HG_CHUNK = 64
WINDOW = 128
ROPE_THETA = 10000.0
NORM_EPS = 1e-6
CAPACITY_FACTOR = 2
VMEM_LIMIT_BYTES = 56 * 1024 * 1024
SEARCH_ROUNDS = 76
HGRN_HEADS_PER_STEP = 2
COMBINE_TOKENS = 256
COMBINE_WINDOW = 64
GATHER_EXPERTS = 8
BF16_ROWS = 16


class Cfg(NamedTuple):
    B: int
    S: int
    D: int
    HH: int
    AH: int
    KVH: int
    E: int
    FF: int

    @property
    def T(self):
        return self.B * self.S

    @property
    def C(self):
        return CAPACITY_FACTOR * self.S // self.E

    @property
    def offsets(self):
        hq = self.HH * HEAD
        widths = (hq, hq, hq, hq, hq, self.AH * HEAD, self.KVH * HEAD, self.KVH * HEAD, self.D, self.D)
        offs, o = [], 0
        for w in widths:
            offs.append(o)
            o += w
        return tuple(offs), o


def _params(*sem):
    return pltpu.CompilerParams(dimension_semantics=sem, vmem_limit_bytes=VMEM_LIMIT_BYTES)


def _pick(n, target):
    if n <= target:
        return n
    t = (target // LANES) * LANES
    while t > LANES and n % t:
        t -= LANES
    return t


def _dot(a, b):
    return jnp.dot(a, b, preferred_element_type=F32)


def _dot_nt(a, b):
    return lax.dot_general(a, b, (((1,), (1,)), ((), ())), preferred_element_type=F32)


def _dot_tn(a, b):
    return lax.dot_general(a, b, (((0,), (0,)), ((), ())), preferred_element_type=F32)


def _silu(x):
    return x * jax.nn.sigmoid(x)


def _rms(x):
    return x * lax.rsqrt(jnp.mean(x * x, axis=-1, keepdims=True) + NORM_EPS)


def _ada_kernel(c_ref, w_ref, b_ref, o_ref):
    a = _silu(c_ref[...]).astype(BF16)
    o_ref[...] = _dot(a, w_ref[...].astype(BF16)) + b_ref[...]


def _ada(c_pad, w, b):
    rows, d = c_pad.shape
    n = w.shape[1]
    tn = _pick(n, 1024)
    return pl.pallas_call(
        _ada_kernel,
        grid=(n // tn,),
        in_specs=[pl.BlockSpec((rows, d), lambda j: (0, 0)),
                  pl.BlockSpec((d, tn), lambda j: (0, j)),
                  pl.BlockSpec((1, tn), lambda j: (0, j))],
        out_specs=pl.BlockSpec((rows, tn), lambda j: (0, j)),
        out_shape=jax.ShapeDtypeStruct((rows, n), F32),
        compiler_params=_params("arbitrary"),
        name="ada",
    )(c_pad, w, b)


def _prenorm_kernel(x_ref, g_ref, mod_ref, o_ref):
    y = _rms(x_ref[...]) * g_ref[...]
    shift = mod_ref[0, 0:1, :]
    scale = mod_ref[0, 1:2, :]
    o_ref[...] = (y * (1.0 + scale) + shift).astype(BF16)


def _prenorm(cfg, x2, g, mod3):
    tm = _pick(cfg.S, 512)
    per_b = cfg.S // tm
    return pl.pallas_call(
        _prenorm_kernel,
        grid=(cfg.T // tm,),
        in_specs=[pl.BlockSpec((tm, cfg.D), lambda i: (i, 0)),
                  pl.BlockSpec((1, cfg.D), lambda i: (0, 0)),
                  pl.BlockSpec((1, 6, cfg.D), lambda i: (i // per_b, 0, 0))],
        out_specs=pl.BlockSpec((tm, cfg.D), lambda i: (i, 0)),
        out_shape=jax.ShapeDtypeStruct((cfg.T, cfg.D), BF16),
        compiler_params=_params("arbitrary"),
        name="prenorm",
    )(x2, g, mod3)


def _inproj_kernel(a_ref, w_ref, o_ref, wbf_ref):
    @pl.when(pl.program_id(1) == 0)
    def _():
        wbf_ref[...] = w_ref[...].astype(BF16)

    o_ref[...] = _dot(a_ref[...], wbf_ref[...])


def _inproj(a, w):
    m, k = a.shape
    n = w.shape[1]
    tn = _pick(n, 1536)
    tm = _pick(m, 1024)
    return pl.pallas_call(
        _inproj_kernel,
        grid=(n // tn, m // tm),
        in_specs=[pl.BlockSpec((tm, k), lambda j, i: (i, 0)),
                  pl.BlockSpec((k, tn), lambda j, i: (0, j))],
        out_specs=pl.BlockSpec((tm, tn), lambda j, i: (i, j)),
        out_shape=jax.ShapeDtypeStruct((m, n), F32),
        scratch_shapes=[pltpu.VMEM((k, tn), BF16)],
        compiler_params=_params("arbitrary", "arbitrary"),
        name="inproj",
    )(a, w)


def _rope_kernel(q_ref, k_ref, pos_ref, inv_ref, sgn_ref, qo_ref, ko_ref, *, q_scale):
    ang = pos_ref[...].astype(F32) * inv_ref[...]
    cos = jnp.cos(ang)
    sin = jnp.sin(ang) * sgn_ref[...]

    def rotate(x_ref, o_ref, scale):
        for h in range(x_ref.shape[1] // HEAD):
            cols = slice(h * HEAD, (h + 1) * HEAD)
            x = x_ref[:, cols]
            r = x * cos + pltpu.roll(x, HEAD // 2, 1) * sin
            o_ref[:, cols] = (r if scale is None else r * scale).astype(BF16)

    rotate(q_ref, qo_ref, q_scale)
    rotate(k_ref, ko_ref, None)


def _rope(cfg, proj, pos2, inv, sgn):
    offs, _ = cfg.offsets
    wq, wk = cfg.AH * HEAD, cfg.KVH * HEAD
    assert offs[5] % wq == 0 and offs[6] % wk == 0
    tm = _pick(cfg.T, 256)
    return pl.pallas_call(
        functools.partial(_rope_kernel, q_scale=HEAD ** -0.5),
        grid=(cfg.T // tm,),
        in_specs=[pl.BlockSpec((tm, wq), lambda i: (i, offs[5] // wq)),
                  pl.BlockSpec((tm, wk), lambda i: (i, offs[6] // wk)),
                  pl.BlockSpec((tm, 1), lambda i: (i, 0)),
                  pl.BlockSpec((1, HEAD), lambda i: (0, 0)),
                  pl.BlockSpec((1, HEAD), lambda i: (0, 0))],
        out_specs=[pl.BlockSpec((tm, wq), lambda i: (i, 0)),
                   pl.BlockSpec((tm, wk), lambda i: (i, 0))],
        out_shape=[jax.ShapeDtypeStruct((cfg.T, wq), BF16),
                   jax.ShapeDtypeStruct((cfg.T, wk), BF16)],
        compiler_params=_params("arbitrary"),
        name="rope",
    )(proj, proj, pos2, inv, sgn)


def _attn_kernel(sink_ref, q_ref, kp_ref, kc_ref, kn_ref, vp_ref, vc_ref, vn_ref, o_ref, *, G, NB):
    kvh = pl.program_id(1)
    n = pl.program_id(2)
    blk = WINDOW
    k = jnp.concatenate([kp_ref[...], kc_ref[...], kn_ref[...]], axis=0)
    v = jnp.concatenate([vp_ref[...], vc_ref[...], vn_ref[...]], axis=0).astype(BF16)
    q = jnp.concatenate([q_ref[:, g * HEAD:(g + 1) * HEAD] for g in range(G)], axis=0)
    row = lax.broadcasted_iota(jnp.int32, (G * blk, 3 * blk), 0)
    qi = row - (row // blk) * blk
    kj = lax.broadcasted_iota(jnp.int32, (G * blk, 3 * blk), 1)
    valid = jnp.abs(kj - blk - qi) <= WINDOW
    valid = valid & ((kj >= blk) | (n > 0)) & ((kj < 2 * blk) | (n < NB - 1))
    head = lax.broadcasted_iota(jnp.int32, (G * blk, 1), 0) // blk
    sk = jnp.zeros((G * blk, 1), F32)
    for g in range(G):
        sk = jnp.where(head == g, sink_ref[kvh * G + g], sk)
    s = jnp.where(valid, _dot_nt(q, k), -jnp.inf)
    m = jnp.maximum(jnp.max(s, axis=-1, keepdims=True), sk)
    p = jnp.exp(s - m)
    denom = jnp.sum(p, axis=-1, keepdims=True) + jnp.exp(sk - m)
    o = (_dot(p.astype(BF16), v) / denom).astype(BF16)
    for g in range(G):
        o_ref[:, g * HEAD:(g + 1) * HEAD] = o[g * blk:(g + 1) * blk]


def _attn(cfg, q_rot, k_rot, proj, sink):
    offs, _ = cfg.offsets
    G = cfg.AH // cfg.KVH
    blk = WINDOW
    NB = cfg.S // blk
    kcol = 0
    vcol = offs[7] // HEAD
    prev = lambda n: jnp.maximum(n - 1, 0)
    nxt = lambda n: jnp.minimum(n + 1, NB - 1)
    return pl.pallas_call(
        functools.partial(_attn_kernel, G=G, NB=NB),
        grid=(cfg.B, cfg.KVH, NB),
        in_specs=[pl.BlockSpec(memory_space=pltpu.SMEM),
                  pl.BlockSpec((blk, G * HEAD), lambda b, h, n: (b * NB + n, h)),
                  pl.BlockSpec((blk, HEAD), lambda b, h, n: (b * NB + prev(n), kcol + h)),
                  pl.BlockSpec((blk, HEAD), lambda b, h, n: (b * NB + n, kcol + h)),
                  pl.BlockSpec((blk, HEAD), lambda b, h, n: (b * NB + nxt(n), kcol + h)),
                  pl.BlockSpec((blk, HEAD), lambda b, h, n: (b * NB + prev(n), vcol + h)),
                  pl.BlockSpec((blk, HEAD), lambda b, h, n: (b * NB + n, vcol + h)),
                  pl.BlockSpec((blk, HEAD), lambda b, h, n: (b * NB + nxt(n), vcol + h))],
        out_specs=pl.BlockSpec((blk, G * HEAD), lambda b, h, n: (b * NB + n, h)),
        out_shape=jax.ShapeDtypeStruct((cfg.T, cfg.AH * HEAD), BF16),
        compiler_params=_params("arbitrary", "arbitrary", "arbitrary"),
        name="attn",
    )(sink, q_rot, k_rot, k_rot, k_rot, proj, proj, proj)


def _lower_bound(lbl_ref, d, layer):
    n = lbl_ref.shape[1]
    rows = [lbl_ref[d, j:j + 1, :] for j in range(n)]
    m = functools.reduce(jnp.maximum, rows)
    es = [jnp.exp(r - m) for r in rows]
    return sum(es[:layer + 1]) / sum(es)


def _hgrn_pair(qs, k, logf, v, st, tri2, fwd):
    L = HG_CHUNK
    ref_i, last_i = (L // 2, L - 1) if fwd else (L - 1 - L // 2, 0)
    hi = logf.astype(BF16)
    lo = (logf - hi.astype(F32)).astype(BF16)
    both = _dot(tri2.astype(BF16), jnp.concatenate([hi, lo], axis=1))
    b = both[:, :HEAD] + both[:, HEAD:]

    def per_chunk(rows):
        return jnp.concatenate([jnp.broadcast_to(r, (L, HEAD)) for r in rows], axis=0)

    brefs = [b[c * L + ref_i:c * L + ref_i + 1, :] for c in (0, 1)]
    blasts = [b[c * L + last_i:c * L + last_i + 1, :] for c in (0, 1)]
    d = b - per_chunk(brefs)
    q_in = qs * jnp.exp(d)
    k_in = k * jnp.exp(-d)
    a = jnp.where(tri2 > 0.0, _dot_nt(q_in.astype(BF16), k_in.astype(BF16)), 0.0)
    v_bf = v.astype(BF16)
    o_intra = _dot(a.astype(BF16), v_bf)
    q_dec = (q_in * per_chunk([jnp.exp(r) for r in brefs])).astype(BF16)
    k_dec = k_in * per_chunk([jnp.exp(l - r) for l, r in zip(blasts, brefs)])
    first = lax.broadcasted_iota(jnp.int32, k_dec.shape, 0) < L
    zero = jnp.zeros_like(k_dec)
    k_blk = jnp.concatenate([jnp.where(first, k_dec, zero), jnp.where(first, zero, k_dec)], axis=1).astype(BF16)
    cs = _dot_tn(v_bf, k_blk)
    o_inter = [None, None]
    for c in ((0, 1) if fwd else (1, 0)):
        r = slice(c * L, (c + 1) * L)
        o_inter[c] = _dot_nt(q_dec[r], st.astype(BF16))
        st = st * jnp.exp(blasts[c]) + cs[:, c * HEAD:(c + 1) * HEAD]
    return o_intra + jnp.concatenate(o_inter, axis=0), st


def _hgrn_kernel(lbl_ref, gain_ref, qf_ref, vf_ref, zf_ref, qb_ref, vb_ref, zb_ref, g_ref, o_ref,
                 st_ref, of_ref, ob_ref, *, NT, R, HP, layer):
    i = pl.program_id(2)
    L = 2 * HG_CHUNK
    nc = R // L
    li = lax.broadcasted_iota(jnp.int32, (L, L), 0)
    mi = lax.broadcasted_iota(jnp.int32, (L, L), 1)
    same = (li // HG_CHUNK) == (mi // HG_CHUNK)
    tri_f = jnp.where(same & (mi <= li), 1.0, 0.0).astype(F32)
    tri_b = jnp.where(same & (mi >= li), 1.0, 0.0).astype(F32)

    @pl.when(i == 0)
    def _():
        st_ref[...] = jnp.zeros_like(st_ref)

    lanes = [slice(p * HEAD, (p + 1) * HEAD) for p in range(HP)]
    streams = []
    for p in range(HP):
        for d, (q_ref, v_ref, z_ref) in enumerate(((qf_ref, vf_ref, zf_ref), (qb_ref, vb_ref, zb_ref))):
            lb = _lower_bound(lbl_ref, d, layer)[:, lanes[p]]
            f = lb + (1.0 - lb) * jax.nn.sigmoid(z_ref[:, lanes[p]])
            streams.append(dict(p=p, d=d, qs=_silu(q_ref[:, lanes[p]]), k=1.0 - f, logf=jnp.log(f),
                                v=v_ref[:, lanes[p]], st=st_ref[p, d]))
    for c in range(nc):
        for sm in streams:
            fwd = sm["d"] == 0
            cc = c if fwd else nc - 1 - c
            r = slice(cc * L, (cc + 1) * L)
            o, sm["st"] = _hgrn_pair(sm["qs"][r], sm["k"][r], sm["logf"][r], sm["v"][r], sm["st"],
                                     tri_f if fwd else tri_b, fwd)
            tile = i if fwd else NT - 1 - i
            rows = pl.ds(pl.multiple_of(tile * R + cc * L, L), L)
            (of_ref if fwd else ob_ref)[rows, lanes[sm["p"]]] = o
    for sm in streams:
        st_ref[sm["p"], sm["d"]] = sm["st"]

    @pl.when(i == NT - 1)
    def _():
        for t in range(NT):
            rows = slice(t * R, (t + 1) * R)
            for p in range(HP):
                o = of_ref[rows, lanes[p]] + ob_ref[rows, lanes[p]]
                gate = _silu(g_ref[rows, lanes[p]])
                o_ref[rows, lanes[p]] = (_rms(o) * gain_ref[0, :, lanes[p]] * gate).astype(BF16)


def _hgrn(cfg, proj, lb_logits, gain3, layer):
    offs, _ = cfg.offsets
    HP = HGRN_HEADS_PER_STEP
    W = HP * HEAD
    assert cfg.HH % HP == 0 and all(o % W == 0 for o in offs[:5])
    qc, fc, bc, ic, gc = (o // W for o in offs[:5])
    R = _pick(cfg.S, 512)
    NT = cfg.S // R
    nl = lb_logits.shape[1]
    fwd = lambda col: pl.BlockSpec((R, W), lambda b, h, i: (b * NT + i, col + h))
    bwd = lambda col: pl.BlockSpec((R, W), lambda b, h, i: (b * NT + NT - 1 - i, col + h))
    return pl.pallas_call(
        functools.partial(_hgrn_kernel, NT=NT, R=R, HP=HP, layer=layer),
        grid=(cfg.B, cfg.HH // HP, NT),
        in_specs=[pl.BlockSpec((2, nl, W), lambda b, h, i: (0, 0, h)),
                  pl.BlockSpec((1, 1, W), lambda b, h, i: (h, 0, 0)),
                  fwd(qc), fwd(ic), fwd(fc), bwd(qc), bwd(ic), bwd(bc),
                  pl.BlockSpec((cfg.S, W), lambda b, h, i: (b, gc + h))],
        out_specs=pl.BlockSpec((cfg.S, W), lambda b, h, i: (b, h)),
        out_shape=jax.ShapeDtypeStruct((cfg.T, cfg.HH * HEAD), BF16),
        scratch_shapes=[pltpu.VMEM((HP, 2, HEAD, HEAD), F32),
                        pltpu.VMEM((cfg.S, W), F32), pltpu.VMEM((cfg.S, W), F32)],
        compiler_params=_params("arbitrary", "arbitrary", "arbitrary"),
        name="hgrn",
    )(lb_logits, gain3, proj, proj, proj, proj, proj, proj, proj)


def _merge_kernel(oa_ref, ob_ref, *refs, nsub):
    ga_refs, gb_refs = refs[:nsub], refs[nsub:2 * nsub]
    wa_ref, wb_ref, o_ref, wabf_ref, wbbf_ref = refs[2 * nsub:]

    @pl.when(pl.program_id(1) == 0)
    def _():
        wabf_ref[...] = wa_ref[...].astype(BF16)
        wbbf_ref[...] = wb_ref[...].astype(BF16)

    ya = _dot(oa_ref[...], wabf_ref[...])
    yb = _dot(ob_ref[...], wbbf_ref[...])
    gw = ya.shape[1] // nsub
    for u in range(nsub):
        cols = slice(u * gw, (u + 1) * gw)
        o_ref[:, cols] = (jax.nn.sigmoid(ga_refs[u][...]) * ya[:, cols]
                          + jax.nn.sigmoid(gb_refs[u][...]) * yb[:, cols]).astype(BF16)


def _merge(cfg, oa, ob, proj, wa, wb):
    offs, _ = cfg.offsets
    gw = _pick(cfg.D, 512)
    assert offs[8] % gw == 0 and offs[9] % gw == 0
    nsub = 2 if cfg.D % (2 * gw) == 0 else 1
    tn = nsub * gw
    ga0, gb0 = offs[8] // gw, offs[9] // gw
    tm = _pick(cfg.T, 512)
    ka, kb = oa.shape[1], ob.shape[1]
    gate = lambda c0, u: pl.BlockSpec((tm, gw), lambda j, i: (i, c0 + nsub * j + u))
    return pl.pallas_call(
        functools.partial(_merge_kernel, nsub=nsub),
        grid=(cfg.D // tn, cfg.T // tm),
        in_specs=[pl.BlockSpec((tm, ka), lambda j, i: (i, 0)),
                  pl.BlockSpec((tm, kb), lambda j, i: (i, 0)),
                  *[gate(ga0, u) for u in range(nsub)],
                  *[gate(gb0, u) for u in range(nsub)],
                  pl.BlockSpec((ka, tn), lambda j, i: (0, j)),
                  pl.BlockSpec((kb, tn), lambda j, i: (0, j))],
        out_specs=pl.BlockSpec((tm, tn), lambda j, i: (i, j)),
        out_shape=jax.ShapeDtypeStruct((cfg.T, cfg.D), BF16),
        scratch_shapes=[pltpu.VMEM((ka, tn), BF16), pltpu.VMEM((kb, tn), BF16)],
        compiler_params=_params("arbitrary", "arbitrary"),
        name="merge",
    )(oa, ob, *([proj] * (2 * nsub)), wa, wb)


def _cast_kernel(w_ref, o_ref):
    o_ref[...] = w_ref[...].astype(BF16)


def _to_bf16(w):
    k, n = w.shape
    tk = _pick(k, 512)
    return pl.pallas_call(
        _cast_kernel,
        grid=(k // tk,),
        in_specs=[pl.BlockSpec((tk, n), lambda i: (i, 0))],
        out_specs=pl.BlockSpec((tk, n), lambda i: (i, 0)),
        out_shape=jax.ShapeDtypeStruct((k, n), BF16),
        compiler_params=_params("arbitrary"),
        name="wcast",
    )(w)


def _outproj_kernel(a_ref, w_ref, x_ref, mod_ref, gpost_ref, gpre_ref, wr_ref,
                    x1_ref, h_ref, aff_ref, *, E, SUB):
    wr = wr_ref[...]
    w_hi = wr.astype(BF16)
    w_lo = (wr - w_hi.astype(F32)).astype(BF16)
    for u in range(a_ref.shape[0] // SUB):
        rows = slice(u * SUB, (u + 1) * SUB)
        y = _dot(a_ref[rows, :], w_ref[...])
        x1 = x_ref[rows, :] + mod_ref[0, 2:3, :] * (_rms(y) * gpost_ref[...])
        x1_ref[rows, :] = x1
        h = _rms(x1) * gpre_ref[...] * (1.0 + mod_ref[0, 4:5, :]) + mod_ref[0, 3:4, :]
        h_hi = h.astype(BF16)
        h_ref[rows, :] = h_hi
        h_lo = (h - h_hi.astype(F32)).astype(BF16)
        logits = _dot(h_hi, w_hi) + _dot(h_lo, w_hi) + _dot(h_hi, w_lo)
        lane = lax.broadcasted_iota(jnp.int32, logits.shape, 1)
        logits = jnp.where(lane < E, logits, -jnp.inf)
        p = jnp.exp(logits - jnp.max(logits, axis=-1, keepdims=True))
        aff_ref[rows, :] = p / jnp.sum(p, axis=-1, keepdims=True)


def _outproj(cfg, merged, w_bf, x2, mod3, gpost, gpre, wr_pad):
    tm = _pick(cfg.S, 512)
    sub = _pick(tm, 256)
    per_b = cfg.S // tm
    D = cfg.D
    const = lambda i: (0, 0)
    row = lambda i: (i, 0)
    return pl.pallas_call(
        functools.partial(_outproj_kernel, E=cfg.E, SUB=sub),
        grid=(cfg.T // tm,),
        in_specs=[pl.BlockSpec((tm, D), row),
                  pl.BlockSpec((D, D), const, pipeline_mode=pl.Buffered(1)),
                  pl.BlockSpec((tm, D), row),
                  pl.BlockSpec((1, 6, D), lambda i: (i // per_b, 0, 0)),
                  pl.BlockSpec((1, D), const),
                  pl.BlockSpec((1, D), const),
                  pl.BlockSpec((D, LANES), const)],
        out_specs=[pl.BlockSpec((tm, D), row),
                   pl.BlockSpec((tm, D), row),
                   pl.BlockSpec((tm, LANES), row)],
        out_shape=[jax.ShapeDtypeStruct((cfg.T, D), F32),
                   jax.ShapeDtypeStruct((cfg.T, D), BF16),
                   jax.ShapeDtypeStruct((cfg.T, LANES), F32)],
        compiler_params=_params("arbitrary"),
        name="outproj",
    )(merged, w_bf, x2, mod3, gpost, gpre, wr_pad)


def _route_kernel(aff_ref, posm_ref, postm_ref, offs_ref, a_ref, *, E, S, C):
    nt = S // LANES
    for t in range(nt):
        a_ref[:, t * LANES:(t + 1) * LANES] = aff_ref[t * LANES:(t + 1) * LANES, :].T[:E, :]

    cap = jnp.float32(C)

    def narrow(_, carry):
        lo, hi = carry
        width = hi - lo
        a = a_ref[...]
        m1, m2, m3 = (lo + frac * width for frac in (0.25, 0.5, 0.75))
        g1, g2, g3 = (jnp.sum(jnp.where(a >= m, 1.0, 0.0), axis=1, keepdims=True) >= cap for m in (m1, m2, m3))
        new_lo = jnp.where(g3, m3, jnp.where(g2, m2, jnp.where(g1, m1, lo)))
        new_hi = jnp.where(g3, hi, jnp.where(g2, m3, jnp.where(g1, m2, m1)))
        return new_lo, new_hi

    lo, hi = lax.fori_loop(0, SEARCH_ROUNDS, narrow,
                           (jnp.zeros((E, 1), F32), jnp.full((E, 1), 2.0, F32)))
    need = cap - jnp.sum(jnp.where(a_ref[...] >= hi, 1.0, 0.0), axis=1, keepdims=True)

    ui = lax.broadcasted_iota(jnp.int32, (LANES, LANES), 0)
    uj = lax.broadcasted_iota(jnp.int32, (LANES, LANES), 1)
    upper = jnp.where(ui <= uj, 1.0, 0.0).astype(BF16)
    lane = lax.broadcasted_iota(jnp.int32, (E, LANES), 1)
    run_tie = jnp.zeros((E, 1), F32)
    run_sel = jnp.zeros((E, 1), F32)
    offs = jnp.zeros((E, LANES), F32)
    pad = jnp.full((LANES - E, LANES), -1.0, F32)
    for t in range(nt):
        cols = slice(t * LANES, (t + 1) * LANES)
        a = a_ref[:, cols]
        tie = (a >= lo) & (a < hi)
        tie_t = jnp.where(tie, 1.0, 0.0)
        tie_rank = _dot(tie_t.astype(BF16), upper) - tie_t + run_tie
        sel = (a >= hi) | (tie & (tie_rank < need))
        sel_t = jnp.where(sel, 1.0, 0.0)
        pos = _dot(sel_t.astype(BF16), upper) - sel_t + run_sel
        posm = jnp.where(sel, pos, -1.0)
        posm_ref[0, :, cols] = posm
        postm_ref[cols, :] = jnp.concatenate([posm, pad], axis=0).T
        offs = jnp.where(lane == t, run_sel, offs)
        run_tie = run_tie + jnp.sum(tie_t, axis=1, keepdims=True)
        run_sel = run_sel + jnp.sum(sel_t, axis=1, keepdims=True)
    offs = jnp.where(lane == nt, run_sel, offs)
    offs_ref[0] = offs.astype(jnp.int32)


def _route(cfg, aff_tm):
    E, S = cfg.E, cfg.S
    assert S // LANES < LANES and E % 8 == 0
    return pl.pallas_call(
        functools.partial(_route_kernel, E=E, S=S, C=cfg.C),
        grid=(cfg.B,),
        in_specs=[pl.BlockSpec((S, LANES), lambda b: (b, 0))],
        out_specs=[pl.BlockSpec((1, E, S), lambda b: (b, 0, 0)),
                   pl.BlockSpec((S, LANES), lambda b: (b, 0)),
                   pl.BlockSpec((1, E, LANES), lambda b: (b, 0, 0))],
        out_shape=[jax.ShapeDtypeStruct((cfg.B, E, S), F32),
                   jax.ShapeDtypeStruct((cfg.T, LANES), F32),
                   jax.ShapeDtypeStruct((cfg.B, E, LANES), jnp.int32)],
        scratch_shapes=[pltpu.VMEM((E, S), F32)],
        compiler_params=_params("arbitrary"),
        name="route",
    )(aff_tm)


def _slot_window(offs_ref, row, token_block, C):
    base = row * LANES + token_block * (COMBINE_TOKENS // LANES)
    first, end = offs_ref[base], offs_ref[base + COMBINE_TOKENS // LANES]
    start = jnp.minimum((first // BF16_ROWS) * BF16_ROWS, C - COMBINE_WINDOW)
    return pl.multiple_of(start, BF16_ROWS), end


def _extra_windows(start, end):
    return (jnp.maximum(end - start - COMBINE_WINDOW, 0) + COMBINE_WINDOW - 1) // COMBINE_WINDOW


def _gather_kernel(offs_ref, posm_ref, h_ref, x_ref, *, E, G, C):
    b, grp, kb = pl.program_id(0), pl.program_id(1), pl.program_id(2)
    W, TB = COMBINE_WINDOW, COMBINE_TOKENS

    @pl.when(kb == 0)
    def _():
        x_ref[...] = jnp.zeros_like(x_ref)

    slot0 = lax.broadcasted_iota(jnp.int32, (W, TB), 0)
    starts, pieces = [], []
    for j in range(G):
        start, _ = _slot_window(offs_ref, b * E + grp * G + j, kb, C)
        starts.append(start)
        pieces.append(jnp.where(posm_ref[j] == (start + slot0).astype(F32), 1.0, 0.0).astype(BF16))
    rows = _dot(jnp.concatenate(pieces, axis=0), h_ref[...]).astype(BF16)
    for j in range(G):
        win = pl.ds(starts[j], W)
        x_ref[0, j, win, :] = x_ref[0, j, win, :] + rows[j * W:(j + 1) * W]

    def spill(j, carry):
        start, end = _slot_window(offs_ref, b * E + grp * G + j, kb, C)

        def extra(w, c2):
            cur = start + (w + 1) * W
            s2 = pl.multiple_of(jnp.minimum(cur, C - W), BF16_ROWS)
            slot = s2 + slot0
            onehot = jnp.where((posm_ref[j] == slot.astype(F32)) & (slot >= cur), 1.0, 0.0).astype(BF16)
            win = pl.ds(s2, W)
            x_ref[0, j, win, :] = x_ref[0, j, win, :] + _dot(onehot, h_ref[...]).astype(BF16)
            return c2

        return lax.fori_loop(0, _extra_windows(start, end), extra, carry)

    lax.fori_loop(0, G, spill, 0)


def _gather(cfg, offs_flat, posm3, h2):
    B, E, S, C, D = cfg.B, cfg.E, cfg.S, cfg.C, cfg.D
    G, TB = min(E, GATHER_EXPERTS), COMBINE_TOKENS
    assert E % G == 0 and S % TB == 0 and C % COMBINE_WINDOW == 0
    nk = S // TB
    grid_spec = pltpu.PrefetchScalarGridSpec(
        num_scalar_prefetch=1,
        grid=(B, E // G, nk),
        in_specs=[pl.BlockSpec((G, 1, TB), lambda b, g, k, offs: (b * (E // G) + g, 0, k)),
                  pl.BlockSpec((TB, D), lambda b, g, k, offs: (b * nk + k, 0))],
        out_specs=pl.BlockSpec((1, G, C, D), lambda b, g, k, offs: (b, g, 0, 0)),
    )
    return pl.pallas_call(
        functools.partial(_gather_kernel, E=E, G=G, C=C),
        grid_spec=grid_spec,
        out_shape=jax.ShapeDtypeStruct((B, E, C, D), BF16),
        compiler_params=_params("arbitrary", "arbitrary", "arbitrary"),
        name="gather",
    )(offs_flat, posm3, h2)


def _ffn_kernel(x_ref, wg_ref, wu_ref, wd_ref, y_ref, acc_ref):
    f = pl.program_id(1)

    @pl.when(f == 0)
    def _():
        acc_ref[...] = jnp.zeros_like(acc_ref)

    wg = wg_ref[0].astype(BF16)
    wu = wu_ref[0].astype(BF16)
    wd = wd_ref[0].astype(BF16)
    for b in range(x_ref.shape[0]):
        x = x_ref[b, 0]
        a = (_silu(_dot(x, wg)) * _dot(x, wu)).astype(BF16)
        acc_ref[b] += _dot(a, wd)

    @pl.when(f == pl.num_programs(1) - 1)
    def _():
        y_ref[:, 0] = acc_ref[...].astype(BF16)


def _ffn(cfg, xe, wg, wu, wd):
    B, E, C, D, FF = cfg.B, cfg.E, cfg.C, cfg.D, cfg.FF
    tf = _pick(FF, 256)
    return pl.pallas_call(
        _ffn_kernel,
        grid=(E, FF // tf),
        in_specs=[pl.BlockSpec((B, 1, C, D), lambda e, f: (0, e, 0, 0)),
                  pl.BlockSpec((1, D, tf), lambda e, f: (e, 0, f)),
                  pl.BlockSpec((1, D, tf), lambda e, f: (e, 0, f)),
                  pl.BlockSpec((1, tf, D), lambda e, f: (e, f, 0))],
        out_specs=pl.BlockSpec((B, 1, C, D), lambda e, f: (0, e, 0, 0)),
        out_shape=jax.ShapeDtypeStruct((B, E, C, D), BF16),
        scratch_shapes=[pltpu.VMEM((B, C, D), F32)],
        compiler_params=_params("arbitrary", "arbitrary"),
        name="ffn",
    )(xe, wg, wu, wd)


def _combine_kernel(offs_ref, pos_ref, aff_ref, ppos_ref, paff_ref, y_ref, x1_ref, mod_ref, g_ref, o_ref,
                    z_ref, acc_ref, *, E, C):
    b, k = pl.program_id(0), pl.program_id(1)
    W, TB = COMBINE_WINDOW, COMBINE_TOKENS
    lane = lax.broadcasted_iota(jnp.int32, (TB, LANES), 1)

    @pl.when(k == 0)
    def _():
        acc_ref[...] = jnp.zeros_like(acc_ref)

    def spill(e, carry):
        start, end = _slot_window(offs_ref, b * E + e, k - 1, C)
        mine = lane == e
        pos = jnp.sum(jnp.where(mine, ppos_ref[...], 0.0), axis=1, keepdims=True)
        aff = jnp.sum(jnp.where(mine, paff_ref[...], 0.0), axis=1, keepdims=True)

        def extra(w, c2):
            cur = start + (w + 1) * W
            s2 = pl.multiple_of(jnp.minimum(cur, C - W), BF16_ROWS)
            slot = s2 + lax.broadcasted_iota(jnp.int32, (TB, W), 1)
            m = jnp.where((pos == slot.astype(F32)) & (slot >= cur), aff, 0.0).astype(BF16)
            acc_ref[(k + 1) % 2] += _dot(m, y_ref[0, e, pl.ds(s2, W), :])
            return c2

        return lax.fori_loop(0, _extra_windows(start, end), extra, carry)

    @pl.when(k > 0)
    def _():
        lax.fori_loop(0, E, spill, 0)

    def fused(cur, prev):
        o_ref[...] = x1_ref[...] + mod_ref[0, 5:6, :] * (_rms(acc_ref[prev]) * g_ref[...])

        starts = []
        for e in range(E):
            start, _ = _slot_window(offs_ref, b * E + e, k, C)
            z_ref[e * W:(e + 1) * W, :] = y_ref[0, e, pl.ds(start, W), :]
            starts.append(start)
        seg = lane // W
        off = lane - seg * W
        group = LANES // W
        pieces = []
        for p in range(E // group):
            slot = jnp.zeros((TB, LANES), jnp.int32)
            pos = jnp.full((TB, LANES), -1.0, F32)
            aff = jnp.zeros((TB, LANES), F32)
            for q in range(group):
                e = p * group + q
                mine = seg == q
                slot = jnp.where(mine, starts[e] + off, slot)
                pos = jnp.where(mine, pos_ref[:, e:e + 1], pos)
                aff = jnp.where(mine, aff_ref[:, e:e + 1], aff)
            pieces.append(jnp.where(pos == slot.astype(F32), aff, 0.0).astype(BF16))
        acc_ref[cur] = _dot(jnp.concatenate(pieces, axis=1), z_ref[...])

    @pl.when(k % 2 == 0)
    def _():
        fused(0, 1)

    @pl.when(k % 2 == 1)
    def _():
        fused(1, 0)


def _combine(cfg, offs_flat, pos_tm, aff_tm, y, x1, mod3, g):
    B, E, S, C, D = cfg.B, cfg.E, cfg.S, cfg.C, cfg.D
    W, TB = COMBINE_WINDOW, COMBINE_TOKENS
    assert LANES % W == 0 and E % (LANES // W) == 0 and C % W == 0 and W % BF16_ROWS == 0 and S % TB == 0
    nk = S // TB
    assert (nk + 2) * (TB // LANES) <= LANES
    cur = lambda b, k, offs: (b * nk + jnp.minimum(k, nk - 1), 0)
    lag = lambda b, k, offs: (b * nk + jnp.maximum(k - 1, 0), 0)
    grid_spec = pltpu.PrefetchScalarGridSpec(
        num_scalar_prefetch=1,
        grid=(B, nk + 1),
        in_specs=[pl.BlockSpec((TB, LANES), cur),
                  pl.BlockSpec((TB, LANES), cur),
                  pl.BlockSpec((TB, LANES), lag),
                  pl.BlockSpec((TB, LANES), lag),
                  pl.BlockSpec((1, E, C, D), lambda b, k, offs: (b, 0, 0, 0), pipeline_mode=pl.Buffered(1)),
                  pl.BlockSpec((TB, D), lag),
                  pl.BlockSpec((1, 6, D), lambda b, k, offs: (b, 0, 0)),
                  pl.BlockSpec((1, D), lambda b, k, offs: (0, 0))],
        out_specs=pl.BlockSpec((TB, D), lag),
        scratch_shapes=[pltpu.VMEM((E * W, D), BF16), pltpu.VMEM((2, TB, D), F32)],
    )
    return pl.pallas_call(
        functools.partial(_combine_kernel, E=E, C=C),
        grid_spec=grid_spec,
        out_shape=jax.ShapeDtypeStruct((cfg.T, D), F32),
        compiler_params=_params("arbitrary", "arbitrary"),
        name="combine",
    )(offs_flat, pos_tm, aff_tm, pos_tm, aff_tm, y, x1, mod3, g)


def _layer(cfg, layer, x2, c_pad, pos2, inv, sgn, w_ada, b_ada, g_pre_mix, g_post_mix, g_pre_ffn,
           g_post_ffn, w_in, hg_lb_logits, hg_out_norm, attn_sink, w_branch_a, w_branch_b, w_out,
           w_router, w_exp_gate, w_exp_up, w_exp_down):
    D, E = cfg.D, cfg.E
    row = lambda v: v.reshape(1, -1)
    mod = _ada(c_pad, w_ada[layer], row(b_ada[layer]))[:cfg.B]
    mod3 = mod.reshape(cfg.B, 6, D)
    h = _prenorm(cfg, x2, row(g_pre_mix[layer]), mod3)
    proj = _inproj(h, w_in[layer])
    q_rot, k_rot = _rope(cfg, proj, pos2, inv, sgn)
    att = _attn(cfg, q_rot, k_rot, proj, attn_sink[layer])
    gain3 = hg_out_norm[layer].reshape(cfg.HH // HGRN_HEADS_PER_STEP, 1, HGRN_HEADS_PER_STEP * HEAD)
    oa = _hgrn(cfg, proj, hg_lb_logits, gain3, layer)
    merged = _merge(cfg, oa, att, proj, w_branch_a[layer], w_branch_b[layer])
    wr_pad = jnp.pad(w_router[layer], ((0, 0), (0, LANES - E)))
    x1, h2, aff_tm = _outproj(cfg, merged, _to_bf16(w_out[layer]), x2, mod3, row(g_post_mix[layer]),
                              row(g_pre_ffn[layer]), wr_pad)
    posm, pos_tm, offs = _route(cfg, aff_tm)
    offs_flat = offs.reshape(-1)
    xe = _gather(cfg, offs_flat, posm.reshape(cfg.B * E, 1, cfg.S), h2)
    ye = _ffn(cfg, xe, w_exp_gate[layer], w_exp_up[layer], w_exp_down[layer])
    return _combine(cfg, offs_flat, pos_tm, aff_tm, ye, x1, mod3, row(g_post_ffn[layer]))


def _block(cfg, x, c, positions, w_ada, *rest):
    depth = w_ada.shape[0]
    x2 = x.reshape(cfg.T, cfg.D)
    c_pad = jnp.pad(c, ((0, 8 - cfg.B), (0, 0)))
    pos2 = positions.reshape(cfg.T, 1)
    half = HEAD // 2
    inv_half = ROPE_THETA ** (-jnp.arange(half, dtype=F32) / half)
    inv = jnp.concatenate([inv_half, inv_half]).reshape(1, HEAD)
    sgn = jnp.concatenate([-jnp.ones((half,), F32), jnp.ones((half,), F32)]).reshape(1, HEAD)
    for layer in range(depth):
        x2 = _layer(cfg, layer, x2, c_pad, pos2, inv, sgn, w_ada, *rest)
    return x2.reshape(cfg.B, cfg.S, cfg.D)


def kernel(x, c, positions, w_ada, b_ada, g_pre_mix, g_post_mix, g_pre_ffn, g_post_ffn, w_in,
           hg_lb_logits, hg_out_norm, attn_sink, w_branch_a, w_branch_b, w_out, w_router,
           w_exp_gate, w_exp_up, w_exp_down):
    B, S, D = x.shape
    cfg = Cfg(B=B, S=S, D=D,
              HH=hg_out_norm.shape[1], AH=attn_sink.shape[1],
              KVH=(w_in.shape[2] - 5 * hg_out_norm.shape[1] * HEAD - attn_sink.shape[1] * HEAD - 2 * D) // (2 * HEAD),
              E=w_router.shape[2], FF=w_exp_gate.shape[3])
    return _block(cfg, x, c, positions, w_ada, b_ada, g_pre_mix, g_post_mix, g_pre_ffn, g_post_ffn,
                  w_in, hg_lb_logits, hg_out_norm, attn_sink, w_branch_a, w_branch_b, w_out,
                  w_router, w_exp_gate, w_exp_up, w_exp_down)
```

```python
import functools
from typing import NamedTuple

import jax
import jax.numpy as jnp
from jax import lax
from jax.experimental import pallas as pl
from jax.experimental.pallas import tpu as pltpu

F32 = jnp.float32
BF16 = jnp.bfloat16

LANES = 128
HEAD = 128
HG_CHUNK = 64
WINDOW = 128
ROPE_THETA = 10000.0
NORM_EPS = 1e-6
CAPACITY_FACTOR = 2
VMEM_LIMIT_BYTES = 56 * 1024 * 1024
SEARCH_ROUNDS = 76
HGRN_HEADS_PER_STEP = 2
COMBINE_TOKENS = 256
COMBINE_WINDOW = 64
GATHER_EXPERTS = 8
BF16_ROWS = 16


class Cfg(NamedTuple):
    B: int
    S: int
    D: int
    HH: int
    AH: int
    KVH: int
    E: int
    FF: int

    @property
    def T(self):
        return self.B * self.S

    @property
    def C(self):
        return CAPACITY_FACTOR * self.S // self.E

    @property
    def offsets(self):
        hq = self.HH * HEAD
        widths = (hq, hq, hq, hq, hq, self.AH * HEAD, self.KVH * HEAD, self.KVH * HEAD, self.D, self.D)
        offs, o = [], 0
        for w in widths:
            offs.append(o)
            o += w
        return tuple(offs), o


def _params(*sem):
    return pltpu.CompilerParams(dimension_semantics=sem, vmem_limit_bytes=VMEM_LIMIT_BYTES)


def _pick(n, target):
    if n <= target:
        return n
    t = (target // LANES) * LANES
    while t > LANES and n % t:
        t -= LANES
    return t


def _dot(a, b):
    return jnp.dot(a, b, preferred_element_type=F32)


def _dot_nt(a, b):
    return lax.dot_general(a, b, (((1,), (1,)), ((), ())), preferred_element_type=F32)


def _dot_tn(a, b):
    return lax.dot_general(a, b, (((0,), (0,)), ((), ())), preferred_element_type=F32)


def _silu(x):
    return x * jax.nn.sigmoid(x)


def _rms(x):
    return x * lax.rsqrt(jnp.mean(x * x, axis=-1, keepdims=True) + NORM_EPS)


def _ada_kernel(c_ref, w_ref, b_ref, o_ref):
    a = _silu(c_ref[...]).astype(BF16)
    o_ref[...] = _dot(a, w_ref[...].astype(BF16)) + b_ref[...]


def _ada(c_pad, w, b):
    rows, d = c_pad.shape
    n = w.shape[1]
    tn = _pick(n, 1024)
    return pl.pallas_call(
        _ada_kernel,
        grid=(n // tn,),
        in_specs=[pl.BlockSpec((rows, d), lambda j: (0, 0)),
                  pl.BlockSpec((d, tn), lambda j: (0, j)),
                  pl.BlockSpec((1, tn), lambda j: (0, j))],
        out_specs=pl.BlockSpec((rows, tn), lambda j: (0, j)),
        out_shape=jax.ShapeDtypeStruct((rows, n), F32),
        compiler_params=_params("arbitrary"),
        name="ada",
    )(c_pad, w, b)


def _prenorm_kernel(x_ref, g_ref, mod_ref, o_ref):
    y = _rms(x_ref[...]) * g_ref[...]
    shift = mod_ref[0, 0:1, :]
    scale = mod_ref[0, 1:2, :]
    o_ref[...] = (y * (1.0 + scale) + shift).astype(BF16)


def _prenorm(cfg, x2, g, mod3):
    tm = _pick(cfg.S, 512)
    per_b = cfg.S // tm
    return pl.pallas_call(
        _prenorm_kernel,
        grid=(cfg.T // tm,),
        in_specs=[pl.BlockSpec((tm, cfg.D), lambda i: (i, 0)),
                  pl.BlockSpec((1, cfg.D), lambda i: (0, 0)),
                  pl.BlockSpec((1, 6, cfg.D), lambda i: (i // per_b, 0, 0))],
        out_specs=pl.BlockSpec((tm, cfg.D), lambda i: (i, 0)),
        out_shape=jax.ShapeDtypeStruct((cfg.T, cfg.D), BF16),
        compiler_params=_params("arbitrary"),
        name="prenorm",
    )(x2, g, mod3)


def _inproj_kernel(a_ref, w_ref, o_ref, wbf_ref):
    @pl.when(pl.program_id(1) == 0)
    def _():
        wbf_ref[...] = w_ref[...].astype(BF16)

    o_ref[...] = _dot(a_ref[...], wbf_ref[...])


def _inproj(a, w):
    m, k = a.shape
    n = w.shape[1]
    tn = _pick(n, 1536)
    tm = _pick(m, 1024)
    return pl.pallas_call(
        _inproj_kernel,
        grid=(n // tn, m // tm),
        in_specs=[pl.BlockSpec((tm, k), lambda j, i: (i, 0)),
                  pl.BlockSpec((k, tn), lambda j, i: (0, j))],
        out_specs=pl.BlockSpec((tm, tn), lambda j, i: (i, j)),
        out_shape=jax.ShapeDtypeStruct((m, n), F32),
        scratch_shapes=[pltpu.VMEM((k, tn), BF16)],
        compiler_params=_params("arbitrary", "arbitrary"),
        name="inproj",
    )(a, w)


def _rope_kernel(q_ref, k_ref, pos_ref, inv_ref, sgn_ref, qo_ref, ko_ref, *, q_scale):
    ang = pos_ref[...].astype(F32) * inv_ref[...]
    cos = jnp.cos(ang)
    sin = jnp.sin(ang) * sgn_ref[...]

    def rotate(x_ref, o_ref, scale):
        for h in range(x_ref.shape[1] // HEAD):
            cols = slice(h * HEAD, (h + 1) * HEAD)
            x = x_ref[:, cols]
            r = x * cos + pltpu.roll(x, HEAD // 2, 1) * sin
            o_ref[:, cols] = (r if scale is None else r * scale).astype(BF16)

    rotate(q_ref, qo_ref, q_scale)
    rotate(k_ref, ko_ref, None)


def _rope(cfg, proj, pos2, inv, sgn):
    offs, _ = cfg.offsets
    wq, wk = cfg.AH * HEAD, cfg.KVH * HEAD
    assert offs[5] % wq == 0 and offs[6] % wk == 0
    tm = _pick(cfg.T, 256)
    return pl.pallas_call(
        functools.partial(_rope_kernel, q_scale=HEAD ** -0.5),
        grid=(cfg.T // tm,),
        in_specs=[pl.BlockSpec((tm, wq), lambda i: (i, offs[5] // wq)),
                  pl.BlockSpec((tm, wk), lambda i: (i, offs[6] // wk)),
                  pl.BlockSpec((tm, 1), lambda i: (i, 0)),
                  pl.BlockSpec((1, HEAD), lambda i: (0, 0)),
                  pl.BlockSpec((1, HEAD), lambda i: (0, 0))],
        out_specs=[pl.BlockSpec((tm, wq), lambda i: (i, 0)),
                   pl.BlockSpec((tm, wk), lambda i: (i, 0))],
        out_shape=[jax.ShapeDtypeStruct((cfg.T, wq), BF16),
                   jax.ShapeDtypeStruct((cfg.T, wk), BF16)],
        compiler_params=_params("arbitrary"),
        name="rope",
    )(proj, proj, pos2, inv, sgn)


def _attn_kernel(sink_ref, q_ref, kp_ref, kc_ref, kn_ref, vp_ref, vc_ref, vn_ref, o_ref, *, G, NB):
    kvh = pl.program_id(1)
    n = pl.program_id(2)
    blk = WINDOW
    k = jnp.concatenate([kp_ref[...], kc_ref[...], kn_ref[...]], axis=0)
    v = jnp.concatenate([vp_ref[...], vc_ref[...], vn_ref[...]], axis=0).astype(BF16)
    q = jnp.concatenate([q_ref[:, g * HEAD:(g + 1) * HEAD] for g in range(G)], axis=0)
    row = lax.broadcasted_iota(jnp.int32, (G * blk, 3 * blk), 0)
    qi = row - (row // blk) * blk
    kj = lax.broadcasted_iota(jnp.int32, (G * blk, 3 * blk), 1)
    valid = jnp.abs(kj - blk - qi) <= WINDOW
    valid = valid & ((kj >= blk) | (n > 0)) & ((kj < 2 * blk) | (n < NB - 1))
    head = lax.broadcasted_iota(jnp.int32, (G * blk, 1), 0) // blk
    sk = jnp.zeros((G * blk, 1), F32)
    for g in range(G):
        sk = jnp.where(head == g, sink_ref[kvh * G + g], sk)
    s = jnp.where(valid, _dot_nt(q, k), -jnp.inf)
    m = jnp.maximum(jnp.max(s, axis=-1, keepdims=True), sk)
    p = jnp.exp(s - m)
    denom = jnp.sum(p, axis=-1, keepdims=True) + jnp.exp(sk - m)
    o = (_dot(p.astype(BF16), v) / denom).astype(BF16)
    for g in range(G):
        o_ref[:, g * HEAD:(g + 1) * HEAD] = o[g * blk:(g + 1) * blk]


def _attn(cfg, q_rot, k_rot, proj, sink):
    offs, _ = cfg.offsets
    G = cfg.AH // cfg.KVH
    blk = WINDOW
    NB = cfg.S // blk
    kcol = 0
    vcol = offs[7] // HEAD
    prev = lambda n: jnp.maximum(n - 1, 0)
    nxt = lambda n: jnp.minimum(n + 1, NB - 1)
    return pl.pallas_call(
        functools.partial(_attn_kernel, G=G, NB=NB),
        grid=(cfg.B, cfg.KVH, NB),
        in_specs=[pl.BlockSpec(memory_space=pltpu.SMEM),
                  pl.BlockSpec((blk, G * HEAD), lambda b, h, n: (b * NB + n, h)),
                  pl.BlockSpec((blk, HEAD), lambda b, h, n: (b * NB + prev(n), kcol + h)),
                  pl.BlockSpec((blk, HEAD), lambda b, h, n: (b * NB + n, kcol + h)),
                  pl.BlockSpec((blk, HEAD), lambda b, h, n: (b * NB + nxt(n), kcol + h)),
                  pl.BlockSpec((blk, HEAD), lambda b, h, n: (b * NB + prev(n), vcol + h)),
                  pl.BlockSpec((blk, HEAD), lambda b, h, n: (b * NB + n, vcol + h)),
                  pl.BlockSpec((blk, HEAD), lambda b, h, n: (b * NB + nxt(n), vcol + h))],
        out_specs=pl.BlockSpec((blk, G * HEAD), lambda b, h, n: (b * NB + n, h)),
        out_shape=jax.ShapeDtypeStruct((cfg.T, cfg.AH * HEAD), BF16),
        compiler_params=_params("arbitrary", "arbitrary", "arbitrary"),
        name="attn",
    )(sink, q_rot, k_rot, k_rot, k_rot, proj, proj, proj)


def _lower_bound(lbl_ref, d, layer):
    n = lbl_ref.shape[1]
    rows = [lbl_ref[d, j:j + 1, :] for j in range(n)]
    m = functools.reduce(jnp.maximum, rows)
    es = [jnp.exp(r - m) for r in rows]
    return sum(es[:layer + 1]) / sum(es)


def _hgrn_cumsum(logfs, tri2):
    P = 2 * HG_CHUNK
    npair = logfs[0].shape[0] // P
    cols = []
    for lf in logfs:
        hi = lf.astype(BF16)
        lo = (lf - hi.astype(F32)).astype(BF16)
        for p in range(npair):
            cols += [hi[p * P:(p + 1) * P], lo[p * P:(p + 1) * P]]
    both = _dot(tri2.astype(BF16), jnp.concatenate(cols, axis=1))
    outs = []
    for s in range(len(logfs)):
        parts = []
        for p in range(npair):
            c0 = (s * npair + p) * 2 * HEAD
            parts.append(both[:, c0:c0 + HEAD] + both[:, c0 + HEAD:c0 + 2 * HEAD])
        outs.append(jnp.concatenate(parts, axis=0))
    return outs


def _hgrn_tile(qs, k, b, v, st, mask, fwd):
    L = HG_CHUNK
    nc = qs.shape[0] // L
    ref_i, last_i = (L // 2, L - 1) if fwd else (L - 1 - L // 2, 0)

    def per_chunk(rows):
        return jnp.concatenate([jnp.broadcast_to(r, (L, HEAD)) for r in rows], axis=0)

    brefs = [b[c * L + ref_i:c * L + ref_i + 1, :] for c in range(nc)]
    blasts = [b[c * L + last_i:c * L + last_i + 1, :] for c in range(nc)]
    d = b - per_chunk(brefs)
    q_in = qs * jnp.exp(d)
    k_in = k * jnp.exp(-d)
    a = jnp.where(mask, _dot_nt(q_in.astype(BF16), k_in.astype(BF16)), 0.0)
    v_bf = v.astype(BF16)
    o_intra = _dot(a.astype(BF16), v_bf)
    q_dec = (q_in * per_chunk([jnp.exp(r) for r in brefs])).astype(BF16)
    k_dec = k_in * per_chunk([jnp.exp(l - r) for l, r in zip(blasts, brefs)])

    first = lax.broadcasted_iota(jnp.int32, (2 * L, HEAD), 0) < L
    cs = []
    for p in range(nc // 2):
        r = slice(2 * p * L, 2 * (p + 1) * L)
        kd = k_dec[r]
        zero = jnp.zeros_like(kd)
        k_blk = jnp.concatenate([jnp.where(first, kd, zero), jnp.where(first, zero, kd)], axis=1).astype(BF16)
        both = _dot_tn(v_bf[r], k_blk)
        cs += [both[:, :HEAD], both[:, HEAD:]]
    entering = [None] * nc
    for c in (range(nc) if fwd else reversed(range(nc))):
        entering[c] = st
        st = st * jnp.exp(blasts[c]) + cs[c]
    o_inter = []
    for p in range(nc // 2):
        r = slice(2 * p * L, 2 * (p + 1) * L)
        st2 = jnp.concatenate([entering[2 * p], entering[2 * p + 1]], axis=0).astype(BF16)
        both = _dot_nt(q_dec[r], st2)
        o_inter += [both[:L, :HEAD], both[L:, HEAD:]]
    return o_intra + jnp.concatenate(o_inter, axis=0), st


def _hgrn_kernel(lbl_ref, gain_ref, qf_ref, vf_ref, zf_ref, qb_ref, vb_ref, zb_ref, g_ref, o_ref,
                 st_ref, of_ref, ob_ref, *, NT, R, HP, layer):
    i = pl.program_id(2)

    def chunk_masks(n):
        li = lax.broadcasted_iota(jnp.int32, (n, n), 0)
        mi = lax.broadcasted_iota(jnp.int32, (n, n), 1)
        same = (li // HG_CHUNK) == (mi // HG_CHUNK)
        return same & (mi <= li), same & (mi >= li)

    tri_f, tri_b = (jnp.where(m, 1.0, 0.0).astype(F32) for m in chunk_masks(2 * HG_CHUNK))
    mask_f, mask_b = chunk_masks(R)

    @pl.when(i == 0)
    def _():
        st_ref[...] = jnp.zeros_like(st_ref)

    lanes = [slice(p * HEAD, (p + 1) * HEAD) for p in range(HP)]
    for d, (q_ref, v_ref, z_ref) in enumerate(((qf_ref, vf_ref, zf_ref), (qb_ref, vb_ref, zb_ref))):
        fwd = d == 0
        lb = _lower_bound(lbl_ref, d, layer)
        f = lb + (1.0 - lb) * jax.nn.sigmoid(z_ref[...])
        k = 1.0 - f
        qs = _silu(q_ref[...])
        cums = _hgrn_cumsum([jnp.log(f)[:, lanes[p]] for p in range(HP)], tri_f if fwd else tri_b)
        tile = i if fwd else NT - 1 - i
        rows = pl.ds(pl.multiple_of(tile * R, R), R)
        for p in range(HP):
            o, st = _hgrn_tile(qs[:, lanes[p]], k[:, lanes[p]], cums[p], v_ref[:, lanes[p]], st_ref[p, d],
                               mask_f if fwd else mask_b, fwd)
            (of_ref if fwd else ob_ref)[rows, lanes[p]] = o
            st_ref[p, d] = st

    @pl.when(i == NT - 1)
    def _():
        for t in range(NT):
            rows = slice(t * R, (t + 1) * R)
            for p in range(HP):
                o = of_ref[rows, lanes[p]] + ob_ref[rows, lanes[p]]
                gate = _silu(g_ref[rows, lanes[p]])
                o_ref[rows, lanes[p]] = (_rms(o) * gain_ref[0, :, lanes[p]] * gate).astype(BF16)


def _hgrn(cfg, proj, lb_logits, gain3, layer):
    offs, _ = cfg.offsets
    HP = HGRN_HEADS_PER_STEP
    W = HP * HEAD
    assert cfg.HH % HP == 0 and all(o % W == 0 for o in offs[:5])
    qc, fc, bc, ic, gc = (o // W for o in offs[:5])
    R = _pick(cfg.S, 512)
    NT = cfg.S // R
    nl = lb_logits.shape[1]
    fwd = lambda col: pl.BlockSpec((R, W), lambda b, h, i: (b * NT + i, col + h))
    bwd = lambda col: pl.BlockSpec((R, W), lambda b, h, i: (b * NT + NT - 1 - i, col + h))
    return pl.pallas_call(
        functools.partial(_hgrn_kernel, NT=NT, R=R, HP=HP, layer=layer),
        grid=(cfg.B, cfg.HH // HP, NT),
        in_specs=[pl.BlockSpec((2, nl, W), lambda b, h, i: (0, 0, h)),
                  pl.BlockSpec((1, 1, W), lambda b, h, i: (h, 0, 0)),
                  fwd(qc), fwd(ic), fwd(fc), bwd(qc), bwd(ic), bwd(bc),
                  pl.BlockSpec((cfg.S, W), lambda b, h, i: (b, gc + h))],
        out_specs=pl.BlockSpec((cfg.S, W), lambda b, h, i: (b, h)),
        out_shape=jax.ShapeDtypeStruct((cfg.T, cfg.HH * HEAD), BF16),
        scratch_shapes=[pltpu.VMEM((HP, 2, HEAD, HEAD), F32),
                        pltpu.VMEM((cfg.S, W), F32), pltpu.VMEM((cfg.S, W), F32)],
        compiler_params=_params("arbitrary", "arbitrary", "arbitrary"),
        name="hgrn",
    )(lb_logits, gain3, proj, proj, proj, proj, proj, proj, proj)


def _merge_kernel(oa_ref, ob_ref, *refs, nsub):
    ga_refs, gb_refs = refs[:nsub], refs[nsub:2 * nsub]
    wa_ref, wb_ref, o_ref, wabf_ref, wbbf_ref = refs[2 * nsub:]

    @pl.when(pl.program_id(1) == 0)
    def _():
        wabf_ref[...] = wa_ref[...].astype(BF16)
        wbbf_ref[...] = wb_ref[...].astype(BF16)

    ya = _dot(oa_ref[...], wabf_ref[...])
    yb = _dot(ob_ref[...], wbbf_ref[...])
    gw = ya.shape[1] // nsub
    for u in range(nsub):
        cols = slice(u * gw, (u + 1) * gw)
        o_ref[:, cols] = (jax.nn.sigmoid(ga_refs[u][...]) * ya[:, cols]
                          + jax.nn.sigmoid(gb_refs[u][...]) * yb[:, cols]).astype(BF16)


def _merge(cfg, oa, ob, proj, wa, wb):
    offs, _ = cfg.offsets
    gw = _pick(cfg.D, 512)
    assert offs[8] % gw == 0 and offs[9] % gw == 0
    nsub = 2 if cfg.D % (2 * gw) == 0 else 1
    tn = nsub * gw
    ga0, gb0 = offs[8] // gw, offs[9] // gw
    tm = _pick(cfg.T, 512)
    ka, kb = oa.shape[1], ob.shape[1]
    gate = lambda c0, u: pl.BlockSpec((tm, gw), lambda j, i: (i, c0 + nsub * j + u))
    return pl.pallas_call(
        functools.partial(_merge_kernel, nsub=nsub),
        grid=(cfg.D // tn, cfg.T // tm),
        in_specs=[pl.BlockSpec((tm, ka), lambda j, i: (i, 0)),
                  pl.BlockSpec((tm, kb), lambda j, i: (i, 0)),
                  *[gate(ga0, u) for u in range(nsub)],
                  *[gate(gb0, u) for u in range(nsub)],
                  pl.BlockSpec((ka, tn), lambda j, i: (0, j)),
                  pl.BlockSpec((kb, tn), lambda j, i: (0, j))],
        out_specs=pl.BlockSpec((tm, tn), lambda j, i: (i, j)),
        out_shape=jax.ShapeDtypeStruct((cfg.T, cfg.D), BF16),
        scratch_shapes=[pltpu.VMEM((ka, tn), BF16), pltpu.VMEM((kb, tn), BF16)],
        compiler_params=_params("arbitrary", "arbitrary"),
        name="merge",
    )(oa, ob, *([proj] * (2 * nsub)), wa, wb)


def _cast_kernel(w_ref, o_ref):
    o_ref[...] = w_ref[...].astype(BF16)


def _to_bf16(w):
    k, n = w.shape
    tk = _pick(k, 512)
    return pl.pallas_call(
        _cast_kernel,
        grid=(k // tk,),
        in_specs=[pl.BlockSpec((tk, n), lambda i: (i, 0))],
        out_specs=pl.BlockSpec((tk, n), lambda i: (i, 0)),
        out_shape=jax.ShapeDtypeStruct((k, n), BF16),
        compiler_params=_params("arbitrary"),
        name="wcast",
    )(w)


def _outproj_kernel(a_ref, w_ref, x_ref, mod_ref, gpost_ref, gpre_ref, wr_ref,
                    x1_ref, h_ref, aff_ref, *, E, SUB):
    wr = wr_ref[...]
    w_hi = wr.astype(BF16)
    w_lo = (wr - w_hi.astype(F32)).astype(BF16)
    for u in range(a_ref.shape[0] // SUB):
        rows = slice(u * SUB, (u + 1) * SUB)
        y = _dot(a_ref[rows, :], w_ref[...])
        x1 = x_ref[rows, :] + mod_ref[0, 2:3, :] * (_rms(y) * gpost_ref[...])
        x1_ref[rows, :] = x1
        h = _rms(x1) * gpre_ref[...] * (1.0 + mod_ref[0, 4:5, :]) + mod_ref[0, 3:4, :]
        h_hi = h.astype(BF16)
        h_ref[rows, :] = h_hi
        h_lo = (h - h_hi.astype(F32)).astype(BF16)
        logits = _dot(h_hi, w_hi) + _dot(h_lo, w_hi) + _dot(h_hi, w_lo)
        lane = lax.broadcasted_iota(jnp.int32, logits.shape, 1)
        logits = jnp.where(lane < E, logits, -jnp.inf)
        p = jnp.exp(logits - jnp.max(logits, axis=-1, keepdims=True))
        aff_ref[rows, :] = p / jnp.sum(p, axis=-1, keepdims=True)


def _outproj(cfg, merged, w_bf, x2, mod3, gpost, gpre, wr_pad):
    tm = _pick(cfg.S, 512)
    sub = _pick(tm, 256)
    per_b = cfg.S // tm
    D = cfg.D
    const = lambda i: (0, 0)
    row = lambda i: (i, 0)
    return pl.pallas_call(
        functools.partial(_outproj_kernel, E=cfg.E, SUB=sub),
        grid=(cfg.T // tm,),
        in_specs=[pl.BlockSpec((tm, D), row),
                  pl.BlockSpec((D, D), const, pipeline_mode=pl.Buffered(1)),
                  pl.BlockSpec((tm, D), row),
                  pl.BlockSpec((1, 6, D), lambda i: (i // per_b, 0, 0)),
                  pl.BlockSpec((1, D), const),
                  pl.BlockSpec((1, D), const),
                  pl.BlockSpec((D, LANES), const)],
        out_specs=[pl.BlockSpec((tm, D), row),
                   pl.BlockSpec((tm, D), row),
                   pl.BlockSpec((tm, LANES), row)],
        out_shape=[jax.ShapeDtypeStruct((cfg.T, D), F32),
                   jax.ShapeDtypeStruct((cfg.T, D), BF16),
                   jax.ShapeDtypeStruct((cfg.T, LANES), F32)],
        compiler_params=_params("arbitrary"),
        name="outproj",
    )(merged, w_bf, x2, mod3, gpost, gpre, wr_pad)


def _route_kernel(aff_ref, posm_ref, postm_ref, offs_ref, a_ref, *, E, S, C):
    nt = S // LANES
    for t in range(nt):
        a_ref[:, t * LANES:(t + 1) * LANES] = aff_ref[t * LANES:(t + 1) * LANES, :].T[:E, :]

    cap = jnp.float32(C)

    def narrow(_, carry):
        lo, hi = carry
        width = hi - lo
        a = a_ref[...]
        m1, m2, m3 = (lo + frac * width for frac in (0.25, 0.5, 0.75))
        g1, g2, g3 = (jnp.sum(jnp.where(a >= m, 1.0, 0.0), axis=1, keepdims=True) >= cap for m in (m1, m2, m3))
        new_lo = jnp.where(g3, m3, jnp.where(g2, m2, jnp.where(g1, m1, lo)))
        new_hi = jnp.where(g3, hi, jnp.where(g2, m3, jnp.where(g1, m2, m1)))
        return new_lo, new_hi

    lo, hi = lax.fori_loop(0, SEARCH_ROUNDS, narrow,
                           (jnp.zeros((E, 1), F32), jnp.full((E, 1), 2.0, F32)))
    need = cap - jnp.sum(jnp.where(a_ref[...] >= hi, 1.0, 0.0), axis=1, keepdims=True)

    ui = lax.broadcasted_iota(jnp.int32, (LANES, LANES), 0)
    uj = lax.broadcasted_iota(jnp.int32, (LANES, LANES), 1)
    upper = jnp.where(ui <= uj, 1.0, 0.0).astype(BF16)
    lane = lax.broadcasted_iota(jnp.int32, (E, LANES), 1)
    run_tie = jnp.zeros((E, 1), F32)
    run_sel = jnp.zeros((E, 1), F32)
    offs = jnp.zeros((E, LANES), F32)
    pad = jnp.full((LANES - E, LANES), -1.0, F32)
    for t in range(nt):
        cols = slice(t * LANES, (t + 1) * LANES)
        a = a_ref[:, cols]
        tie = (a >= lo) & (a < hi)
        tie_t = jnp.where(tie, 1.0, 0.0)
        tie_rank = _dot(tie_t.astype(BF16), upper) - tie_t + run_tie
        sel = (a >= hi) | (tie & (tie_rank < need))
        sel_t = jnp.where(sel, 1.0, 0.0)
        pos = _dot(sel_t.astype(BF16), upper) - sel_t + run_sel
        posm = jnp.where(sel, pos, -1.0)
        posm_ref[0, :, cols] = posm
        postm_ref[cols, :] = jnp.concatenate([posm, pad], axis=0).T
        offs = jnp.where(lane == t, run_sel, offs)
        run_tie = run_tie + jnp.sum(tie_t, axis=1, keepdims=True)
        run_sel = run_sel + jnp.sum(sel_t, axis=1, keepdims=True)
    offs = jnp.where(lane == nt, run_sel, offs)
    offs_ref[0] = offs.astype(jnp.int32)


def _route(cfg, aff_tm):
    E, S = cfg.E, cfg.S
    assert S // LANES < LANES and E % 8 == 0
    return pl.pallas_call(
        functools.partial(_route_kernel, E=E, S=S, C=cfg.C),
        grid=(cfg.B,),
        in_specs=[pl.BlockSpec((S, LANES), lambda b: (b, 0))],
        out_specs=[pl.BlockSpec((1, E, S), lambda b: (b, 0, 0)),
                   pl.BlockSpec((S, LANES), lambda b: (b, 0)),
                   pl.BlockSpec((1, E, LANES), lambda b: (b, 0, 0))],
        out_shape=[jax.ShapeDtypeStruct((cfg.B, E, S), F32),
                   jax.ShapeDtypeStruct((cfg.T, LANES), F32),
                   jax.ShapeDtypeStruct((cfg.B, E, LANES), jnp.int32)],
        scratch_shapes=[pltpu.VMEM((E, S), F32)],
        compiler_params=_params("arbitrary"),
        name="route",
    )(aff_tm)


def _slot_window(offs_ref, row, token_block, C):
    base = row * LANES + token_block * (COMBINE_TOKENS // LANES)
    first, end = offs_ref[base], offs_ref[base + COMBINE_TOKENS // LANES]
    start = jnp.minimum((first // BF16_ROWS) * BF16_ROWS, C - COMBINE_WINDOW)
    return pl.multiple_of(start, BF16_ROWS), end


def _extra_windows(start, end):
    return (jnp.maximum(end - start - COMBINE_WINDOW, 0) + COMBINE_WINDOW - 1) // COMBINE_WINDOW


def _gather_kernel(offs_ref, posm_ref, h_ref, x_ref, *, E, G, C):
    b, grp, kb = pl.program_id(0), pl.program_id(1), pl.program_id(2)
    W, TB = COMBINE_WINDOW, COMBINE_TOKENS

    @pl.when(kb == 0)
    def _():
        x_ref[...] = jnp.zeros_like(x_ref)

    slot0 = lax.broadcasted_iota(jnp.int32, (W, TB), 0)
    starts, pieces = [], []
    for j in range(G):
        start, _ = _slot_window(offs_ref, b * E + grp * G + j, kb, C)
        starts.append(start)
        pieces.append(jnp.where(posm_ref[j] == (start + slot0).astype(F32), 1.0, 0.0).astype(BF16))
    rows = _dot(jnp.concatenate(pieces, axis=0), h_ref[...]).astype(BF16)
    for j in range(G):
        win = pl.ds(starts[j], W)
        x_ref[0, j, win, :] = x_ref[0, j, win, :] + rows[j * W:(j + 1) * W]

    def spill(j, carry):
        start, end = _slot_window(offs_ref, b * E + grp * G + j, kb, C)

        def extra(w, c2):
            cur = start + (w + 1) * W
            s2 = pl.multiple_of(jnp.minimum(cur, C - W), BF16_ROWS)
            slot = s2 + slot0
            onehot = jnp.where((posm_ref[j] == slot.astype(F32)) & (slot >= cur), 1.0, 0.0).astype(BF16)
            win = pl.ds(s2, W)
            x_ref[0, j, win, :] = x_ref[0, j, win, :] + _dot(onehot, h_ref[...]).astype(BF16)
            return c2

        return lax.fori_loop(0, _extra_windows(start, end), extra, carry)

    lax.fori_loop(0, G, spill, 0)


def _gather(cfg, offs_flat, posm3, h2):
    B, E, S, C, D = cfg.B, cfg.E, cfg.S, cfg.C, cfg.D
    G, TB = min(E, GATHER_EXPERTS), COMBINE_TOKENS
    assert E % G == 0 and S % TB == 0 and C % COMBINE_WINDOW == 0
    nk = S // TB
    grid_spec = pltpu.PrefetchScalarGridSpec(
        num_scalar_prefetch=1,
        grid=(B, E // G, nk),
        in_specs=[pl.BlockSpec((G, 1, TB), lambda b, g, k, offs: (b * (E // G) + g, 0, k)),
                  pl.BlockSpec((TB, D), lambda b, g, k, offs: (b * nk + k, 0))],
        out_specs=pl.BlockSpec((1, G, C, D), lambda b, g, k, offs: (b, g, 0, 0)),
    )
    return pl.pallas_call(
        functools.partial(_gather_kernel, E=E, G=G, C=C),
        grid_spec=grid_spec,
        out_shape=jax.ShapeDtypeStruct((B, E, C, D), BF16),
        compiler_params=_params("arbitrary", "arbitrary", "arbitrary"),
        name="gather",
    )(offs_flat, posm3, h2)


def _ffn_kernel(x_ref, wg_ref, wu_ref, wd_ref, y_ref, acc_ref):
    f = pl.program_id(1)

    @pl.when(f == 0)
    def _():
        acc_ref[...] = jnp.zeros_like(acc_ref)

    wg = wg_ref[0].astype(BF16)
    wu = wu_ref[0].astype(BF16)
    wd = wd_ref[0].astype(BF16)
    for b in range(x_ref.shape[0]):
        x = x_ref[b, 0]
        a = (_silu(_dot(x, wg)) * _dot(x, wu)).astype(BF16)
        acc_ref[b] += _dot(a, wd)

    @pl.when(f == pl.num_programs(1) - 1)
    def _():
        y_ref[:, 0] = acc_ref[...].astype(BF16)


def _ffn(cfg, xe, wg, wu, wd):
    B, E, C, D, FF = cfg.B, cfg.E, cfg.C, cfg.D, cfg.FF
    tf = _pick(FF, 256)
    return pl.pallas_call(
        _ffn_kernel,
        grid=(E, FF // tf),
        in_specs=[pl.BlockSpec((B, 1, C, D), lambda e, f: (0, e, 0, 0)),
                  pl.BlockSpec((1, D, tf), lambda e, f: (e, 0, f)),
                  pl.BlockSpec((1, D, tf), lambda e, f: (e, 0, f)),
                  pl.BlockSpec((1, tf, D), lambda e, f: (e, f, 0))],
        out_specs=pl.BlockSpec((B, 1, C, D), lambda e, f: (0, e, 0, 0)),
        out_shape=jax.ShapeDtypeStruct((B, E, C, D), BF16),
        scratch_shapes=[pltpu.VMEM((B, C, D), F32)],
        compiler_params=_params("arbitrary", "arbitrary"),
        name="ffn",
    )(xe, wg, wu, wd)


def _combine_kernel(offs_ref, pos_ref, aff_ref, ppos_ref, paff_ref, y_ref, x1_ref, mod_ref, g_ref, o_ref,
                    z_ref, acc_ref, *, E, C):
    b, k = pl.program_id(0), pl.program_id(1)
    W, TB = COMBINE_WINDOW, COMBINE_TOKENS
    lane = lax.broadcasted_iota(jnp.int32, (TB, LANES), 1)

    @pl.when(k == 0)
    def _():
        acc_ref[...] = jnp.zeros_like(acc_ref)

    def spill(e, carry):
        start, end = _slot_window(offs_ref, b * E + e, k - 1, C)

        def extra(w, c2):
            mine = lane == e
            pos = jnp.sum(jnp.where(mine, ppos_ref[...], 0.0), axis=1, keepdims=True)
            aff = jnp.sum(jnp.where(mine, paff_ref[...], 0.0), axis=1, keepdims=True)
            cur = start + (w + 1) * W
            s2 = pl.multiple_of(jnp.minimum(cur, C - W), BF16_ROWS)
            slot = s2 + lax.broadcasted_iota(jnp.int32, (TB, W), 1)
            m = jnp.where((pos == slot.astype(F32)) & (slot >= cur), aff, 0.0).astype(BF16)
            acc_ref[(k + 1) % 2] += _dot(m, y_ref[0, e, pl.ds(s2, W), :])
            return c2

        return lax.fori_loop(0, _extra_windows(start, end), extra, carry)

    @pl.when(k > 0)
    def _():
        lax.fori_loop(0, E, spill, 0)

    def fused(cur, prev):
        o_ref[...] = x1_ref[...] + mod_ref[0, 5:6, :] * (_rms(acc_ref[prev]) * g_ref[...])

        starts = []
        for e in range(E):
            start, _ = _slot_window(offs_ref, b * E + e, k, C)
            z_ref[e * W:(e + 1) * W, :] = y_ref[0, e, pl.ds(start, W), :]
            starts.append(start)
        seg = lane // W
        off = lane - seg * W
        group = LANES // W
        pieces = []
        for p in range(E // group):
            slot = jnp.zeros((TB, LANES), jnp.int32)
            pos = jnp.full((TB, LANES), -1.0, F32)
            aff = jnp.zeros((TB, LANES), F32)
            for q in range(group):
                e = p * group + q
                mine = seg == q
                slot = jnp.where(mine, starts[e] + off, slot)
                pos = jnp.where(mine, pos_ref[:, e:e + 1], pos)
                aff = jnp.where(mine, aff_ref[:, e:e + 1], aff)
            pieces.append(jnp.where(pos == slot.astype(F32), aff, 0.0).astype(BF16))
        acc_ref[cur] = _dot(jnp.concatenate(pieces, axis=1), z_ref[...])

    @pl.when(k % 2 == 0)
    def _():
        fused(0, 1)

    @pl.when(k % 2 == 1)
    def _():
        fused(1, 0)


def _combine(cfg, offs_flat, pos_tm, aff_tm, y, x1, mod3, g):
    B, E, S, C, D = cfg.B, cfg.E, cfg.S, cfg.C, cfg.D
    W, TB = COMBINE_WINDOW, COMBINE_TOKENS
    assert LANES % W == 0 and E % (LANES // W) == 0 and C % W == 0 and W % BF16_ROWS == 0 and S % TB == 0
    nk = S // TB
    assert (nk + 2) * (TB // LANES) <= LANES
    cur = lambda b, k, offs: (b * nk + jnp.minimum(k, nk - 1), 0)
    lag = lambda b, k, offs: (b * nk + jnp.maximum(k - 1, 0), 0)
    grid_spec = pltpu.PrefetchScalarGridSpec(
        num_scalar_prefetch=1,
        grid=(B, nk + 1),
        in_specs=[pl.BlockSpec((TB, LANES), cur),
                  pl.BlockSpec((TB, LANES), cur),
                  pl.BlockSpec((TB, LANES), lag),
                  pl.BlockSpec((TB, LANES), lag),
                  pl.BlockSpec((1, E, C, D), lambda b, k, offs: (b, 0, 0, 0), pipeline_mode=pl.Buffered(1)),
                  pl.BlockSpec((TB, D), lag),
                  pl.BlockSpec((1, 6, D), lambda b, k, offs: (b, 0, 0)),
                  pl.BlockSpec((1, D), lambda b, k, offs: (0, 0))],
        out_specs=pl.BlockSpec((TB, D), lag),
        scratch_shapes=[pltpu.VMEM((E * W, D), BF16), pltpu.VMEM((2, TB, D), F32)],
    )
    return pl.pallas_call(
        functools.partial(_combine_kernel, E=E, C=C),
        grid_spec=grid_spec,
        out_shape=jax.ShapeDtypeStruct((cfg.T, D), F32),
        compiler_params=_params("arbitrary", "arbitrary"),
        name="combine",
    )(offs_flat, pos_tm, aff_tm, pos_tm, aff_tm, y, x1, mod3, g)


def _layer(cfg, layer, x2, c_pad, pos2, inv, sgn, w_ada, b_ada, g_pre_mix, g_post_mix, g_pre_ffn,
           g_post_ffn, w_in, hg_lb_logits, hg_out_norm, attn_sink, w_branch_a, w_branch_b, w_out,
           w_router, w_exp_gate, w_exp_up, w_exp_down):
    D, E = cfg.D, cfg.E
    row = lambda v: v.reshape(1, -1)
    mod = _ada(c_pad, w_ada[layer], row(b_ada[layer]))[:cfg.B]
    mod3 = mod.reshape(cfg.B, 6, D)
    h = _prenorm(cfg, x2, row(g_pre_mix[layer]), mod3)
    proj = _inproj(h, w_in[layer])
    q_rot, k_rot = _rope(cfg, proj, pos2, inv, sgn)
    att = _attn(cfg, q_rot, k_rot, proj, attn_sink[layer])
    gain3 = hg_out_norm[layer].reshape(cfg.HH // HGRN_HEADS_PER_STEP, 1, HGRN_HEADS_PER_STEP * HEAD)
    oa = _hgrn(cfg, proj, hg_lb_logits, gain3, layer)
    merged = _merge(cfg, oa, att, proj, w_branch_a[layer], w_branch_b[layer])
    wr_pad = jnp.pad(w_router[layer], ((0, 0), (0, LANES - E)))
    x1, h2, aff_tm = _outproj(cfg, merged, _to_bf16(w_out[layer]), x2, mod3, row(g_post_mix[layer]),
                              row(g_pre_ffn[layer]), wr_pad)
    posm, pos_tm, offs = _route(cfg, aff_tm)
    offs_flat = offs.reshape(-1)
    xe = _gather(cfg, offs_flat, posm.reshape(cfg.B * E, 1, cfg.S), h2)
    ye = _ffn(cfg, xe, w_exp_gate[layer], w_exp_up[layer], w_exp_down[layer])
    return _combine(cfg, offs_flat, pos_tm, aff_tm, ye, x1, mod3, row(g_post_ffn[layer]))


def _block(cfg, x, c, positions, w_ada, *rest):
    depth = w_ada.shape[0]
    x2 = x.reshape(cfg.T, cfg.D)
    c_pad = jnp.pad(c, ((0, 8 - cfg.B), (0, 0)))
    pos2 = positions.reshape(cfg.T, 1)
    half = HEAD // 2
    inv_half = ROPE_THETA ** (-jnp.arange(half, dtype=F32) / half)
    inv = jnp.concatenate([inv_half, inv_half]).reshape(1, HEAD)
    sgn = jnp.concatenate([-jnp.ones((half,), F32), jnp.ones((half,), F32)]).reshape(1, HEAD)
    for layer in range(depth):
        x2 = _layer(cfg, layer, x2, c_pad, pos2, inv, sgn, w_ada, *rest)
    return x2.reshape(cfg.B, cfg.S, cfg.D)


def kernel(x, c, positions, w_ada, b_ada, g_pre_mix, g_post_mix, g_pre_ffn, g_post_ffn, w_in,
           hg_lb_logits, hg_out_norm, attn_sink, w_branch_a, w_branch_b, w_out, w_router,
           w_exp_gate, w_exp_up, w_exp_down):
    B, S, D = x.shape
    cfg = Cfg(B=B, S=S, D=D,
              HH=hg_out_norm.shape[1], AH=attn_sink.shape[1],
              KVH=(w_in.shape[2] - 5 * hg_out_norm.shape[1] * HEAD - attn_sink.shape[1] * HEAD - 2 * D) // (2 * HEAD),
              E=w_router.shape[2], FF=w_exp_gate.shape[3])
    return _block(cfg, x, c, positions, w_ada, b_ada, g_pre_mix, g_post_mix, g_pre_ffn, g_post_ffn,
                  w_in, hg_lb_logits, hg_out_norm, attn_sink, w_branch_a, w_branch_b, w_out,
                  w_router, w_exp_gate, w_exp_up, w_exp_down)
```

```python
import functools
from typing import NamedTuple

import jax
import jax.numpy as jnp
from jax import lax
from jax.experimental import pallas as pl
from jax.experimental.pallas import tpu as pltpu

F32 = jnp.float32
BF16 = jnp.bfloat16

LANES = 128
HEAD = 128
HG_CHUNK = 64
WINDOW = 128
ROPE_THETA = 10000.0
NORM_EPS = 1e-6
CAPACITY_FACTOR = 2
VMEM_LIMIT_BYTES = 56 * 1024 * 1024
SEARCH_ROUNDS = 76
HGRN_HEADS_PER_STEP = 2
COMBINE_TOKENS = 256
COMBINE_WINDOW = 64
GATHER_EXPERTS = 8
BF16_ROWS = 16


class Cfg(NamedTuple):
    B: int
    S: int
    D: int
    HH: int
    AH: int
    KVH: int
    E: int
    FF: int

    @property
    def T(self):
        return self.B * self.S

    @property
    def C(self):
        return CAPACITY_FACTOR * self.S // self.E

    @property
    def offsets(self):
        hq = self.HH * HEAD
        widths = (hq, hq, hq, hq, hq, self.AH * HEAD, self.KVH * HEAD, self.KVH * HEAD, self.D, self.D)
        offs, o = [], 0
        for w in widths:
            offs.append(o)
            o += w
        return tuple(offs), o


def _params(*sem):
    return pltpu.CompilerParams(dimension_semantics=sem, vmem_limit_bytes=VMEM_LIMIT_BYTES)


def _pick(n, target):
    if n <= target:
        return n
    t = (target // LANES) * LANES
    while t > LANES and n % t:
        t -= LANES
    return t


def _dot(a, b):
    return jnp.dot(a, b, preferred_element_type=F32)


def _dot_nt(a, b):
    return lax.dot_general(a, b, (((1,), (1,)), ((), ())), preferred_element_type=F32)


def _dot_tn(a, b):
    return lax.dot_general(a, b, (((0,), (0,)), ((), ())), preferred_element_type=F32)


def _silu(x):
    return x * jax.nn.sigmoid(x)


def _rms(x):
    return x * lax.rsqrt(jnp.mean(x * x, axis=-1, keepdims=True) + NORM_EPS)


def _ada_kernel(c_ref, w_ref, b_ref, o_ref):
    a = _silu(c_ref[...]).astype(BF16)
    o_ref[...] = _dot(a, w_ref[...].astype(BF16)) + b_ref[...]


def _ada(c_pad, w, b):
    rows, d = c_pad.shape
    n = w.shape[1]
    tn = _pick(n, 1024)
    return pl.pallas_call(
        _ada_kernel,
        grid=(n // tn,),
        in_specs=[pl.BlockSpec((rows, d), lambda j: (0, 0)),
                  pl.BlockSpec((d, tn), lambda j: (0, j)),
                  pl.BlockSpec((1, tn), lambda j: (0, j))],
        out_specs=pl.BlockSpec((rows, tn), lambda j: (0, j)),
        out_shape=jax.ShapeDtypeStruct((rows, n), F32),
        compiler_params=_params("arbitrary"),
        name="ada",
    )(c_pad, w, b)


def _prenorm_kernel(x_ref, g_ref, mod_ref, o_ref):
    y = _rms(x_ref[...]) * g_ref[...]
    shift = mod_ref[0, 0:1, :]
    scale = mod_ref[0, 1:2, :]
    o_ref[...] = (y * (1.0 + scale) + shift).astype(BF16)


def _prenorm(cfg, x2, g, mod3):
    tm = _pick(cfg.S, 1024)
    per_b = cfg.S // tm
    return pl.pallas_call(
        _prenorm_kernel,
        grid=(cfg.T // tm,),
        in_specs=[pl.BlockSpec((tm, cfg.D), lambda i: (i, 0)),
                  pl.BlockSpec((1, cfg.D), lambda i: (0, 0)),
                  pl.BlockSpec((1, 6, cfg.D), lambda i: (i // per_b, 0, 0))],
        out_specs=pl.BlockSpec((tm, cfg.D), lambda i: (i, 0)),
        out_shape=jax.ShapeDtypeStruct((cfg.T, cfg.D), BF16),
        compiler_params=_params("arbitrary"),
        name="prenorm",
    )(x2, g, mod3)


def _inproj_kernel(a_ref, w_ref, o_ref, wbf_ref):
    @pl.when(pl.program_id(1) == 0)
    def _():
        wbf_ref[...] = w_ref[...].astype(BF16)

    o_ref[...] = _dot(a_ref[...], wbf_ref[...])


def _inproj(a, w):
    m, k = a.shape
    n = w.shape[1]
    tn = _pick(n, 1536)
    tm = _pick(m, 1024)
    return pl.pallas_call(
        _inproj_kernel,
        grid=(n // tn, m // tm),
        in_specs=[pl.BlockSpec((tm, k), lambda j, i: (i, 0)),
                  pl.BlockSpec((k, tn), lambda j, i: (0, j))],
        out_specs=pl.BlockSpec((tm, tn), lambda j, i: (i, j)),
        out_shape=jax.ShapeDtypeStruct((m, n), F32),
        scratch_shapes=[pltpu.VMEM((k, tn), BF16)],
        compiler_params=_params("arbitrary", "arbitrary"),
        name="inproj",
    )(a, w)


def _rope_kernel(q_ref, k_ref, pos_ref, inv_ref, sgn_ref, qo_ref, ko_ref, *, q_scale):
    ang = pos_ref[...].astype(F32) * inv_ref[...]
    cos = jnp.cos(ang)
    sin = jnp.sin(ang) * sgn_ref[...]

    def rotate(x_ref, o_ref, scale):
        for h in range(x_ref.shape[1] // HEAD):
            cols = slice(h * HEAD, (h + 1) * HEAD)
            x = x_ref[:, cols]
            r = x * cos + pltpu.roll(x, HEAD // 2, 1) * sin
            o_ref[:, cols] = (r if scale is None else r * scale).astype(BF16)

    rotate(q_ref, qo_ref, q_scale)
    rotate(k_ref, ko_ref, None)


def _rope(cfg, proj, pos2, inv, sgn):
    offs, _ = cfg.offsets
    wq, wk = cfg.AH * HEAD, cfg.KVH * HEAD
    assert offs[5] % wq == 0 and offs[6] % wk == 0
    tm = _pick(cfg.T, 256)
    return pl.pallas_call(
        functools.partial(_rope_kernel, q_scale=HEAD ** -0.5),
        grid=(cfg.T // tm,),
        in_specs=[pl.BlockSpec((tm, wq), lambda i: (i, offs[5] // wq)),
                  pl.BlockSpec((tm, wk), lambda i: (i, offs[6] // wk)),
                  pl.BlockSpec((tm, 1), lambda i: (i, 0)),
                  pl.BlockSpec((1, HEAD), lambda i: (0, 0)),
                  pl.BlockSpec((1, HEAD), lambda i: (0, 0))],
        out_specs=[pl.BlockSpec((tm, wq), lambda i: (i, 0)),
                   pl.BlockSpec((tm, wk), lambda i: (i, 0))],
        out_shape=[jax.ShapeDtypeStruct((cfg.T, wq), BF16),
                   jax.ShapeDtypeStruct((cfg.T, wk), BF16)],
        compiler_params=_params("arbitrary"),
        name="rope",
    )(proj, proj, pos2, inv, sgn)


def _attn_kernel(sink_ref, q_ref, kp_ref, kc_ref, kn_ref, vp_ref, vc_ref, vn_ref, o_ref, *, G, KVH, NB):
    n = pl.program_id(1)
    blk = WINDOW
    row = lax.broadcasted_iota(jnp.int32, (G * blk, 3 * blk), 0)
    qi = row - (row // blk) * blk
    kj = lax.broadcasted_iota(jnp.int32, (G * blk, 3 * blk), 1)
    valid = jnp.abs(kj - blk - qi) <= WINDOW
    valid = valid & ((kj >= blk) | (n > 0)) & ((kj < 2 * blk) | (n < NB - 1))
    head = lax.broadcasted_iota(jnp.int32, (G * blk, 1), 0) // blk
    for kvh in range(KVH):
        kv = slice(kvh * HEAD, (kvh + 1) * HEAD)
        k = jnp.concatenate([kp_ref[:, kv], kc_ref[:, kv], kn_ref[:, kv]], axis=0)
        v = jnp.concatenate([vp_ref[:, kv], vc_ref[:, kv], vn_ref[:, kv]], axis=0).astype(BF16)
        heads = [kvh * G + g for g in range(G)]
        q = jnp.concatenate([q_ref[:, h * HEAD:(h + 1) * HEAD] for h in heads], axis=0)
        sk = jnp.zeros((G * blk, 1), F32)
        for g, h in enumerate(heads):
            sk = jnp.where(head == g, sink_ref[h], sk)
        s = jnp.where(valid, _dot_nt(q, k), -jnp.inf)
        m = jnp.maximum(jnp.max(s, axis=-1, keepdims=True), sk)
        p = jnp.exp(s - m)
        denom = jnp.sum(p, axis=-1, keepdims=True) + jnp.exp(sk - m)
        o = (_dot(p.astype(BF16), v) / denom).astype(BF16)
        for g, h in enumerate(heads):
            o_ref[:, h * HEAD:(h + 1) * HEAD] = o[g * blk:(g + 1) * blk]


def _attn(cfg, q_rot, k_rot, proj, sink):
    offs, _ = cfg.offsets
    G = cfg.AH // cfg.KVH
    blk = WINDOW
    NB = cfg.S // blk
    wq, wkv = cfg.AH * HEAD, cfg.KVH * HEAD
    assert offs[7] % wkv == 0
    vcol = offs[7] // wkv
    prev = lambda n: jnp.maximum(n - 1, 0)
    nxt = lambda n: jnp.minimum(n + 1, NB - 1)
    return pl.pallas_call(
        functools.partial(_attn_kernel, G=G, KVH=cfg.KVH, NB=NB),
        grid=(cfg.B, NB),
        in_specs=[pl.BlockSpec(memory_space=pltpu.SMEM),
                  pl.BlockSpec((blk, wq), lambda b, n: (b * NB + n, 0)),
                  pl.BlockSpec((blk, wkv), lambda b, n: (b * NB + prev(n), 0)),
                  pl.BlockSpec((blk, wkv), lambda b, n: (b * NB + n, 0)),
                  pl.BlockSpec((blk, wkv), lambda b, n: (b * NB + nxt(n), 0)),
                  pl.BlockSpec((blk, wkv), lambda b, n: (b * NB + prev(n), vcol)),
                  pl.BlockSpec((blk, wkv), lambda b, n: (b * NB + n, vcol)),
                  pl.BlockSpec((blk, wkv), lambda b, n: (b * NB + nxt(n), vcol))],
        out_specs=pl.BlockSpec((blk, wq), lambda b, n: (b * NB + n, 0)),
        out_shape=jax.ShapeDtypeStruct((cfg.T, wq), BF16),
        compiler_params=_params("arbitrary", "arbitrary"),
        name="attn",
    )(sink, q_rot, k_rot, k_rot, k_rot, proj, proj, proj)


def _lower_bound(lbl_ref, d, layer):
    n = lbl_ref.shape[1]
    rows = [lbl_ref[d, j:j + 1, :] for j in range(n)]
    m = functools.reduce(jnp.maximum, rows)
    es = [jnp.exp(r - m) for r in rows]
    return sum(es[:layer + 1]) / sum(es)


def _hgrn_cumsum(logfs, tri2):
    P = 2 * HG_CHUNK
    npair = logfs[0].shape[0] // P
    cols = []
    for lf in logfs:
        hi = lf.astype(BF16)
        lo = (lf - hi.astype(F32)).astype(BF16)
        for p in range(npair):
            cols += [hi[p * P:(p + 1) * P], lo[p * P:(p + 1) * P]]
    both = _dot(tri2.astype(BF16), jnp.concatenate(cols, axis=1))
    outs = []
    for s in range(len(logfs)):
        parts = []
        for p in range(npair):
            c0 = (s * npair + p) * 2 * HEAD
            parts.append(both[:, c0:c0 + HEAD] + both[:, c0 + HEAD:c0 + 2 * HEAD])
        outs.append(jnp.concatenate(parts, axis=0))
    return outs


def _hgrn_tile(qs, k, b, v, st, mask, fwd):
    L = HG_CHUNK
    nc = qs.shape[0] // L
    ref_i, last_i = (L // 2, L - 1) if fwd else (L - 1 - L // 2, 0)

    def per_chunk(rows):
        return jnp.concatenate([jnp.broadcast_to(r, (L, HEAD)) for r in rows], axis=0)

    brefs = [b[c * L + ref_i:c * L + ref_i + 1, :] for c in range(nc)]
    blasts = [b[c * L + last_i:c * L + last_i + 1, :] for c in range(nc)]
    d = b - per_chunk(brefs)
    q_in = qs * jnp.exp(d)
    k_in = k * jnp.exp(-d)
    a = jnp.where(mask, _dot_nt(q_in.astype(BF16), k_in.astype(BF16)), 0.0)
    v_bf = v.astype(BF16)
    o_intra = _dot(a.astype(BF16), v_bf)
    q_dec = (q_in * per_chunk([jnp.exp(r) for r in brefs])).astype(BF16)
    k_dec = k_in * per_chunk([jnp.exp(l - r) for l, r in zip(blasts, brefs)])

    first = lax.broadcasted_iota(jnp.int32, (2 * L, HEAD), 0) < L
    cs = []
    for p in range(nc // 2):
        r = slice(2 * p * L, 2 * (p + 1) * L)
        kd = k_dec[r]
        zero = jnp.zeros_like(kd)
        k_blk = jnp.concatenate([jnp.where(first, kd, zero), jnp.where(first, zero, kd)], axis=1).astype(BF16)
        both = _dot_tn(v_bf[r], k_blk)
        cs += [both[:, :HEAD], both[:, HEAD:]]
    entering = [None] * nc
    for c in (range(nc) if fwd else reversed(range(nc))):
        entering[c] = st
        st = st * jnp.exp(blasts[c]) + cs[c]
    o_inter = []
    for p in range(nc // 2):
        r = slice(2 * p * L, 2 * (p + 1) * L)
        st2 = jnp.concatenate([entering[2 * p], entering[2 * p + 1]], axis=0).astype(BF16)
        both = _dot_nt(q_dec[r], st2)
        o_inter += [both[:L, :HEAD], both[L:, HEAD:]]
    return o_intra + jnp.concatenate(o_inter, axis=0), st


def _hgrn_kernel(lbl_ref, gain_ref, qf_ref, vf_ref, zf_ref, qb_ref, vb_ref, zb_ref, g_ref, o_ref,
                 st_ref, of_ref, ob_ref, *, NT, R, HP, layer):
    i = pl.program_id(2)

    def chunk_masks(n):
        li = lax.broadcasted_iota(jnp.int32, (n, n), 0)
        mi = lax.broadcasted_iota(jnp.int32, (n, n), 1)
        same = (li // HG_CHUNK) == (mi // HG_CHUNK)
        return same & (mi <= li), same & (mi >= li)

    tri_f, tri_b = (jnp.where(m, 1.0, 0.0).astype(F32) for m in chunk_masks(2 * HG_CHUNK))
    mask_f, mask_b = chunk_masks(R)

    @pl.when(i == 0)
    def _():
        st_ref[...] = jnp.zeros_like(st_ref)

    lanes = [slice(p * HEAD, (p + 1) * HEAD) for p in range(HP)]
    for d, (q_ref, v_ref, z_ref) in enumerate(((qf_ref, vf_ref, zf_ref), (qb_ref, vb_ref, zb_ref))):
        fwd = d == 0
        lb = _lower_bound(lbl_ref, d, layer)
        f = lb + (1.0 - lb) * jax.nn.sigmoid(z_ref[...])
        k = 1.0 - f
        qs = _silu(q_ref[...])
        cums = _hgrn_cumsum([jnp.log(f)[:, lanes[p]] for p in range(HP)], tri_f if fwd else tri_b)
        tile = i if fwd else NT - 1 - i
        rows = pl.ds(pl.multiple_of(tile * R, R), R)
        for p in range(HP):
            o, st = _hgrn_tile(qs[:, lanes[p]], k[:, lanes[p]], cums[p], v_ref[:, lanes[p]], st_ref[p, d],
                               mask_f if fwd else mask_b, fwd)
            (of_ref if fwd else ob_ref)[rows, lanes[p]] = o
            st_ref[p, d] = st

    @pl.when(i == NT - 1)
    def _():
        for t in range(NT):
            rows = slice(t * R, (t + 1) * R)
            for p in range(HP):
                o = of_ref[rows, lanes[p]] + ob_ref[rows, lanes[p]]
                gate = _silu(g_ref[rows, lanes[p]])
                o_ref[rows, lanes[p]] = (_rms(o) * gain_ref[0, :, lanes[p]] * gate).astype(BF16)


def _hgrn(cfg, proj, lb_logits, gain3, layer):
    offs, _ = cfg.offsets
    HP = HGRN_HEADS_PER_STEP
    W = HP * HEAD
    assert cfg.HH % HP == 0 and all(o % W == 0 for o in offs[:5])
    qc, fc, bc, ic, gc = (o // W for o in offs[:5])
    R = _pick(cfg.S, 512)
    NT = cfg.S // R
    nl = lb_logits.shape[1]
    fwd = lambda col: pl.BlockSpec((R, W), lambda b, h, i: (b * NT + i, col + h))
    bwd = lambda col: pl.BlockSpec((R, W), lambda b, h, i: (b * NT + NT - 1 - i, col + h))
    return pl.pallas_call(
        functools.partial(_hgrn_kernel, NT=NT, R=R, HP=HP, layer=layer),
        grid=(cfg.B, cfg.HH // HP, NT),
        in_specs=[pl.BlockSpec((2, nl, W), lambda b, h, i: (0, 0, h)),
                  pl.BlockSpec((1, 1, W), lambda b, h, i: (h, 0, 0)),
                  fwd(qc), fwd(ic), fwd(fc), bwd(qc), bwd(ic), bwd(bc),
                  pl.BlockSpec((cfg.S, W), lambda b, h, i: (b, gc + h))],
        out_specs=pl.BlockSpec((cfg.S, W), lambda b, h, i: (b, h)),
        out_shape=jax.ShapeDtypeStruct((cfg.T, cfg.HH * HEAD), BF16),
        scratch_shapes=[pltpu.VMEM((HP, 2, HEAD, HEAD), F32),
                        pltpu.VMEM((cfg.S, W), F32), pltpu.VMEM((cfg.S, W), F32)],
        compiler_params=_params("arbitrary", "arbitrary", "arbitrary"),
        name="hgrn",
    )(lb_logits, gain3, proj, proj, proj, proj, proj, proj, proj)


def _merge_kernel(oa_ref, ob_ref, *refs, nsub):
    ga_refs, gb_refs = refs[:nsub], refs[nsub:2 * nsub]
    wa_ref, wb_ref, o_ref, wabf_ref, wbbf_ref = refs[2 * nsub:]

    @pl.when(pl.program_id(1) == 0)
    def _():
        wabf_ref[...] = wa_ref[...].astype(BF16)
        wbbf_ref[...] = wb_ref[...].astype(BF16)

    ya = _dot(oa_ref[...], wabf_ref[...])
    yb = _dot(ob_ref[...], wbbf_ref[...])
    gw = ya.shape[1] // nsub
    for u in range(nsub):
        cols = slice(u * gw, (u + 1) * gw)
        o_ref[:, cols] = (jax.nn.sigmoid(ga_refs[u][...]) * ya[:, cols]
                          + jax.nn.sigmoid(gb_refs[u][...]) * yb[:, cols]).astype(BF16)


def _merge(cfg, oa, ob, proj, wa, wb):
    offs, _ = cfg.offsets
    gw = _pick(cfg.D, 512)
    assert offs[8] % gw == 0 and offs[9] % gw == 0
    nsub = 2 if cfg.D % (2 * gw) == 0 else 1
    tn = nsub * gw
    ga0, gb0 = offs[8] // gw, offs[9] // gw
    tm = _pick(cfg.T, 512)
    ka, kb = oa.shape[1], ob.shape[1]
    gate = lambda c0, u: pl.BlockSpec((tm, gw), lambda j, i: (i, c0 + nsub * j + u))
    return pl.pallas_call(
        functools.partial(_merge_kernel, nsub=nsub),
        grid=(cfg.D // tn, cfg.T // tm),
        in_specs=[pl.BlockSpec((tm, ka), lambda j, i: (i, 0)),
                  pl.BlockSpec((tm, kb), lambda j, i: (i, 0)),
                  *[gate(ga0, u) for u in range(nsub)],
                  *[gate(gb0, u) for u in range(nsub)],
                  pl.BlockSpec((ka, tn), lambda j, i: (0, j)),
                  pl.BlockSpec((kb, tn), lambda j, i: (0, j))],
        out_specs=pl.BlockSpec((tm, tn), lambda j, i: (i, j)),
        out_shape=jax.ShapeDtypeStruct((cfg.T, cfg.D), BF16),
        scratch_shapes=[pltpu.VMEM((ka, tn), BF16), pltpu.VMEM((kb, tn), BF16)],
        compiler_params=_params("arbitrary", "arbitrary"),
        name="merge",
    )(oa, ob, *([proj] * (2 * nsub)), wa, wb)


def _cast_kernel(w_ref, o_ref):
    o_ref[...] = w_ref[...].astype(BF16)


def _to_bf16(w):
    k, n = w.shape
    tk = _pick(k, 512)
    return pl.pallas_call(
        _cast_kernel,
        grid=(k // tk,),
        in_specs=[pl.BlockSpec((tk, n), lambda i: (i, 0))],
        out_specs=pl.BlockSpec((tk, n), lambda i: (i, 0)),
        out_shape=jax.ShapeDtypeStruct((k, n), BF16),
        compiler_params=_params("arbitrary"),
        name="wcast",
    )(w)


def _outproj_kernel(a_ref, w_ref, x_ref, mod_ref, gpost_ref, gpre_ref, wr_ref,
                    x1_ref, h_ref, aff_ref, y_ref, *, E, SUB):
    i = pl.program_id(0)

    @pl.when(i == 0)
    def _():
        y_ref[...] = jnp.zeros_like(y_ref)

    def fused(cur, prev):
        y_ref[cur] = _dot(a_ref[...], w_ref[...])
        wr = wr_ref[...]
        w_hi = wr.astype(BF16)
        w_pair = jnp.concatenate([w_hi, (wr - w_hi.astype(F32)).astype(BF16)], axis=1)
        post_gain = mod_ref[0, 2:3, :] * gpost_ref[...]
        pre_gain = gpre_ref[...] * (1.0 + mod_ref[0, 4:5, :])
        for u in range(a_ref.shape[0] // SUB):
            rows = slice(u * SUB, (u + 1) * SUB)
            x1 = x_ref[rows, :] + _rms(y_ref[prev, rows, :]) * post_gain
            x1_ref[rows, :] = x1
            h = _rms(x1) * pre_gain + mod_ref[0, 3:4, :]
            h_bf = h.astype(BF16)
            h_ref[rows, :] = h_bf
            both = _dot(h_bf, w_pair)
            logits = both[:, :LANES] + both[:, LANES:]
            lane = lax.broadcasted_iota(jnp.int32, logits.shape, 1)
            logits = jnp.where(lane < E, logits, -jnp.inf)
            p = jnp.exp(logits - jnp.max(logits, axis=-1, keepdims=True))
            aff_ref[rows, :] = p / jnp.sum(p, axis=-1, keepdims=True)

    @pl.when(i % 2 == 0)
    def _():
        fused(0, 1)

    @pl.when(i % 2 == 1)
    def _():
        fused(1, 0)


def _outproj(cfg, merged, w_bf, x2, mod3, gpost, gpre, wr_pad):
    tm = _pick(cfg.S, 512)
    sub = _pick(tm, 256)
    per_b = cfg.S // tm
    n = cfg.T // tm
    D = cfg.D
    const = lambda i: (0, 0)
    cur = lambda i: (jnp.minimum(i, n - 1), 0)
    lag = lambda i: (jnp.maximum(i - 1, 0), 0)
    return pl.pallas_call(
        functools.partial(_outproj_kernel, E=cfg.E, SUB=sub),
        grid=(n + 1,),
        in_specs=[pl.BlockSpec((tm, D), cur),
                  pl.BlockSpec((D, D), const, pipeline_mode=pl.Buffered(1)),
                  pl.BlockSpec((tm, D), lag),
                  pl.BlockSpec((1, 6, D), lambda i: (jnp.maximum(i - 1, 0) // per_b, 0, 0)),
                  pl.BlockSpec((1, D), const),
                  pl.BlockSpec((1, D), const),
                  pl.BlockSpec((D, LANES), const)],
        out_specs=[pl.BlockSpec((tm, D), lag),
                   pl.BlockSpec((tm, D), lag),
                   pl.BlockSpec((tm, LANES), lag)],
        out_shape=[jax.ShapeDtypeStruct((cfg.T, D), F32),
                   jax.ShapeDtypeStruct((cfg.T, D), BF16),
                   jax.ShapeDtypeStruct((cfg.T, LANES), F32)],
        scratch_shapes=[pltpu.VMEM((2, tm, D), F32)],
        compiler_params=_params("arbitrary"),
        name="outproj",
    )(merged, w_bf, x2, mod3, gpost, gpre, wr_pad)


def _route_kernel(aff_ref, posm_ref, postm_ref, offs_ref, a_ref, *, E, S, C):
    nt = S // LANES
    for t in range(nt):
        a_ref[:, t * LANES:(t + 1) * LANES] = aff_ref[t * LANES:(t + 1) * LANES, :].T[:E, :]

    cap = jnp.float32(C)

    def narrow(_, carry):
        lo, hi = carry
        width = hi - lo
        a = a_ref[...]
        m1, m2, m3 = (lo + frac * width for frac in (0.25, 0.5, 0.75))
        g1, g2, g3 = (jnp.sum(jnp.where(a >= m, 1.0, 0.0), axis=1, keepdims=True) >= cap for m in (m1, m2, m3))
        new_lo = jnp.where(g3, m3, jnp.where(g2, m2, jnp.where(g1, m1, lo)))
        new_hi = jnp.where(g3, hi, jnp.where(g2, m3, jnp.where(g1, m2, m1)))
        return new_lo, new_hi

    lo, hi = lax.fori_loop(0, SEARCH_ROUNDS, narrow,
                           (jnp.zeros((E, 1), F32), jnp.full((E, 1), 2.0, F32)))
    need = cap - jnp.sum(jnp.where(a_ref[...] >= hi, 1.0, 0.0), axis=1, keepdims=True)

    ui = lax.broadcasted_iota(jnp.int32, (LANES, LANES), 0)
    uj = lax.broadcasted_iota(jnp.int32, (LANES, LANES), 1)
    upper = jnp.where(ui <= uj, 1.0, 0.0).astype(BF16)
    lane = lax.broadcasted_iota(jnp.int32, (E, LANES), 1)
    run_tie = jnp.zeros((E, 1), F32)
    run_sel = jnp.zeros((E, 1), F32)
    offs = jnp.zeros((E, LANES), F32)
    pad = jnp.full((LANES - E, LANES), -1.0, F32)
    for t in range(nt):
        cols = slice(t * LANES, (t + 1) * LANES)
        a = a_ref[:, cols]
        tie = (a >= lo) & (a < hi)
        tie_t = jnp.where(tie, 1.0, 0.0)
        tie_rank = _dot(tie_t.astype(BF16), upper) - tie_t + run_tie
        sel = (a >= hi) | (tie & (tie_rank < need))
        sel_t = jnp.where(sel, 1.0, 0.0)
        pos = _dot(sel_t.astype(BF16), upper) - sel_t + run_sel
        posm = jnp.where(sel, pos, -1.0)
        posm_ref[0, :, cols] = posm
        postm_ref[cols, :] = jnp.concatenate([posm, pad], axis=0).T
        offs = jnp.where(lane == t, run_sel, offs)
        run_tie = run_tie + jnp.sum(tie_t, axis=1, keepdims=True)
        run_sel = run_sel + jnp.sum(sel_t, axis=1, keepdims=True)
    offs = jnp.where(lane == nt, run_sel, offs)
    offs_ref[0] = offs.astype(jnp.int32)


def _route(cfg, aff_tm):
    E, S = cfg.E, cfg.S
    assert S // LANES < LANES and E % 8 == 0
    return pl.pallas_call(
        functools.partial(_route_kernel, E=E, S=S, C=cfg.C),
        grid=(cfg.B,),
        in_specs=[pl.BlockSpec((S, LANES), lambda b: (b, 0))],
        out_specs=[pl.BlockSpec((1, E, S), lambda b: (b, 0, 0)),
                   pl.BlockSpec((S, LANES), lambda b: (b, 0)),
                   pl.BlockSpec((1, E, LANES), lambda b: (b, 0, 0))],
        out_shape=[jax.ShapeDtypeStruct((cfg.B, E, S), F32),
                   jax.ShapeDtypeStruct((cfg.T, LANES), F32),
                   jax.ShapeDtypeStruct((cfg.B, E, LANES), jnp.int32)],
        scratch_shapes=[pltpu.VMEM((E, S), F32)],
        compiler_params=_params("arbitrary"),
        name="route",
    )(aff_tm)


def _slot_window(offs_ref, row, token_block, C):
    base = row * LANES + token_block * (COMBINE_TOKENS // LANES)
    first, end = offs_ref[base], offs_ref[base + COMBINE_TOKENS // LANES]
    start = jnp.minimum((first // BF16_ROWS) * BF16_ROWS, C - COMBINE_WINDOW)
    return pl.multiple_of(start, BF16_ROWS), end


def _extra_windows(start, end):
    return (jnp.maximum(end - start - COMBINE_WINDOW, 0) + COMBINE_WINDOW - 1) // COMBINE_WINDOW


def _gather_kernel(offs_ref, posm_ref, h_ref, x_ref, *, E, G, C):
    b, grp, kb = pl.program_id(0), pl.program_id(1), pl.program_id(2)
    W, TB = COMBINE_WINDOW, COMBINE_TOKENS

    @pl.when(kb == 0)
    def _():
        x_ref[...] = jnp.zeros_like(x_ref)

    slot0 = lax.broadcasted_iota(jnp.int32, (W, TB), 0)
    starts, pieces = [], []
    for j in range(G):
        start, _ = _slot_window(offs_ref, b * E + grp * G + j, kb, C)
        starts.append(start)
        pieces.append(jnp.where(posm_ref[j] == (start + slot0).astype(F32), 1.0, 0.0).astype(BF16))
    rows = _dot(jnp.concatenate(pieces, axis=0), h_ref[...]).astype(BF16)
    for j in range(G):
        win = pl.ds(starts[j], W)
        x_ref[j, win, :] = x_ref[j, win, :] + rows[j * W:(j + 1) * W]

    def spill(j, carry):
        start, end = _slot_window(offs_ref, b * E + grp * G + j, kb, C)

        def extra(w, c2):
            cur = start + (w + 1) * W
            s2 = pl.multiple_of(jnp.minimum(cur, C - W), BF16_ROWS)
            slot = s2 + slot0
            onehot = jnp.where((posm_ref[j] == slot.astype(F32)) & (slot >= cur), 1.0, 0.0).astype(BF16)
            win = pl.ds(s2, W)
            x_ref[j, win, :] = x_ref[j, win, :] + _dot(onehot, h_ref[...]).astype(BF16)
            return c2

        return lax.fori_loop(0, _extra_windows(start, end), extra, carry)

    lax.fori_loop(0, G, spill, 0)


def _gather(cfg, offs_flat, posm3, h2):
    B, E, S, C, D = cfg.B, cfg.E, cfg.S, cfg.C, cfg.D
    G, TB = min(E, GATHER_EXPERTS), COMBINE_TOKENS
    assert E % G == 0 and S % TB == 0 and C % COMBINE_WINDOW == 0
    nk = S // TB
    grid_spec = pltpu.PrefetchScalarGridSpec(
        num_scalar_prefetch=1,
        grid=(B, E // G, nk),
        in_specs=[pl.BlockSpec((G, 1, TB), lambda b, g, k, offs: (b * (E // G) + g, 0, k)),
                  pl.BlockSpec((TB, D), lambda b, g, k, offs: (b * nk + k, 0))],
        out_specs=pl.BlockSpec((G, C, D), lambda b, g, k, offs: (g, b, 0)),
    )
    return pl.pallas_call(
        functools.partial(_gather_kernel, E=E, G=G, C=C),
        grid_spec=grid_spec,
        out_shape=jax.ShapeDtypeStruct((E, B * C, D), BF16),
        compiler_params=_params("arbitrary", "arbitrary", "arbitrary"),
        name="gather",
    )(offs_flat, posm3, h2)


def _ffn_kernel(x_ref, wg_ref, wu_ref, wd_ref, y_ref, acc_ref):
    f = pl.program_id(1)

    @pl.when(f == 0)
    def _():
        acc_ref[...] = jnp.zeros_like(acc_ref)

    x = x_ref[0]
    a = (_silu(_dot(x, wg_ref[0].astype(BF16))) * _dot(x, wu_ref[0].astype(BF16))).astype(BF16)
    acc_ref[...] += _dot(a, wd_ref[0].astype(BF16))

    @pl.when(f == pl.num_programs(1) - 1)
    def _():
        y_ref[0] = acc_ref[...].astype(BF16)


def _ffn(cfg, xe, wg, wu, wd):
    B, E, C, D, FF = cfg.B, cfg.E, cfg.C, cfg.D, cfg.FF
    tf = _pick(FF, 256)
    return pl.pallas_call(
        _ffn_kernel,
        grid=(E, FF // tf),
        in_specs=[pl.BlockSpec((1, B * C, D), lambda e, f: (e, 0, 0)),
                  pl.BlockSpec((1, D, tf), lambda e, f: (e, 0, f)),
                  pl.BlockSpec((1, D, tf), lambda e, f: (e, 0, f)),
                  pl.BlockSpec((1, tf, D), lambda e, f: (e, f, 0))],
        out_specs=pl.BlockSpec((1, B * C, D), lambda e, f: (e, 0, 0)),
        out_shape=jax.ShapeDtypeStruct((E, B * C, D), BF16),
        scratch_shapes=[pltpu.VMEM((B * C, D), F32)],
        compiler_params=_params("arbitrary", "arbitrary"),
        name="ffn",
    )(xe, wg, wu, wd)


def _combine_kernel(offs_ref, pos_ref, aff_ref, ppos_ref, paff_ref, y_ref, x1_ref, mod_ref, g_ref, o_ref,
                    z_ref, acc_ref, *, E, C):
    b, k = pl.program_id(0), pl.program_id(1)
    W, TB = COMBINE_WINDOW, COMBINE_TOKENS
    lane = lax.broadcasted_iota(jnp.int32, (TB, LANES), 1)

    @pl.when(k == 0)
    def _():
        acc_ref[...] = jnp.zeros_like(acc_ref)

    def spill(e, carry):
        start, end = _slot_window(offs_ref, b * E + e, k - 1, C)

        def extra(w, c2):
            mine = lane == e
            pos = jnp.sum(jnp.where(mine, ppos_ref[...], 0.0), axis=1, keepdims=True)
            aff = jnp.sum(jnp.where(mine, paff_ref[...], 0.0), axis=1, keepdims=True)
            cur = start + (w + 1) * W
            s2 = pl.multiple_of(jnp.minimum(cur, C - W), BF16_ROWS)
            slot = s2 + lax.broadcasted_iota(jnp.int32, (TB, W), 1)
            m = jnp.where((pos == slot.astype(F32)) & (slot >= cur), aff, 0.0).astype(BF16)
            acc_ref[(k + 1) % 2] += _dot(m, y_ref[e, pl.ds(s2, W), :])
            return c2

        return lax.fori_loop(0, _extra_windows(start, end), extra, carry)

    @pl.when(k > 0)
    def _():
        lax.fori_loop(0, E, spill, 0)

    def fused(cur, prev):
        o_ref[...] = x1_ref[...] + mod_ref[0, 5:6, :] * (_rms(acc_ref[prev]) * g_ref[...])

        starts = []
        for e in range(E):
            start, _ = _slot_window(offs_ref, b * E + e, k, C)
            z_ref[e * W:(e + 1) * W, :] = y_ref[e, pl.ds(start, W), :]
            starts.append(start)
        seg = lane // W
        off = lane - seg * W
        group = LANES // W
        pieces = []
        for p in range(E // group):
            slot = jnp.zeros((TB, LANES), jnp.int32)
            pos = jnp.full((TB, LANES), -1.0, F32)
            aff = jnp.zeros((TB, LANES), F32)
            for q in range(group):
                e = p * group + q
                mine = seg == q
                slot = jnp.where(mine, starts[e] + off, slot)
                pos = jnp.where(mine, pos_ref[:, e:e + 1], pos)
                aff = jnp.where(mine, aff_ref[:, e:e + 1], aff)
            pieces.append(jnp.where(pos == slot.astype(F32), aff, 0.0).astype(BF16))
        acc_ref[cur] = _dot(jnp.concatenate(pieces, axis=1), z_ref[...])

    @pl.when(k % 2 == 0)
    def _():
        fused(0, 1)

    @pl.when(k % 2 == 1)
    def _():
        fused(1, 0)


def _combine(cfg, offs_flat, pos_tm, aff_tm, y, x1, mod3, g):
    B, E, S, C, D = cfg.B, cfg.E, cfg.S, cfg.C, cfg.D
    W, TB = COMBINE_WINDOW, COMBINE_TOKENS
    assert LANES % W == 0 and E % (LANES // W) == 0 and C % W == 0 and W % BF16_ROWS == 0 and S % TB == 0
    nk = S // TB
    assert (nk + 2) * (TB // LANES) <= LANES
    cur = lambda b, k, offs: (b * nk + jnp.minimum(k, nk - 1), 0)
    lag = lambda b, k, offs: (b * nk + jnp.maximum(k - 1, 0), 0)
    grid_spec = pltpu.PrefetchScalarGridSpec(
        num_scalar_prefetch=1,
        grid=(B, nk + 1),
        in_specs=[pl.BlockSpec((TB, LANES), cur),
                  pl.BlockSpec((TB, LANES), cur),
                  pl.BlockSpec((TB, LANES), lag),
                  pl.BlockSpec((TB, LANES), lag),
                  pl.BlockSpec((E, C, D), lambda b, k, offs: (0, b, 0), pipeline_mode=pl.Buffered(1)),
                  pl.BlockSpec((TB, D), lag),
                  pl.BlockSpec((1, 6, D), lambda b, k, offs: (b, 0, 0)),
                  pl.BlockSpec((1, D), lambda b, k, offs: (0, 0))],
        out_specs=pl.BlockSpec((TB, D), lag),
        scratch_shapes=[pltpu.VMEM((E * W, D), BF16), pltpu.VMEM((2, TB, D), F32)],
    )
    return pl.pallas_call(
        functools.partial(_combine_kernel, E=E, C=C),
        grid_spec=grid_spec,
        out_shape=jax.ShapeDtypeStruct((cfg.T, D), F32),
        compiler_params=_params("arbitrary", "arbitrary"),
        name="combine",
    )(offs_flat, pos_tm, aff_tm, pos_tm, aff_tm, y, x1, mod3, g)


def _layer(cfg, layer, x2, c_pad, pos2, inv, sgn, w_ada, b_ada, g_pre_mix, g_post_mix, g_pre_ffn,
           g_post_ffn, w_in, hg_lb_logits, hg_out_norm, attn_sink, w_branch_a, w_branch_b, w_out,
           w_router, w_exp_gate, w_exp_up, w_exp_down):
    D, E = cfg.D, cfg.E
    row = lambda v: v.reshape(1, -1)
    mod = _ada(c_pad, w_ada[layer], row(b_ada[layer]))[:cfg.B]
    mod3 = mod.reshape(cfg.B, 6, D)
    h = _prenorm(cfg, x2, row(g_pre_mix[layer]), mod3)
    proj = _inproj(h, w_in[layer])
    q_rot, k_rot = _rope(cfg, proj, pos2, inv, sgn)
    att = _attn(cfg, q_rot, k_rot, proj, attn_sink[layer])
    gain3 = hg_out_norm[layer].reshape(cfg.HH // HGRN_HEADS_PER_STEP, 1, HGRN_HEADS_PER_STEP * HEAD)
    oa = _hgrn(cfg, proj, hg_lb_logits, gain3, layer)
    merged = _merge(cfg, oa, att, proj, w_branch_a[layer], w_branch_b[layer])
    wr_pad = jnp.pad(w_router[layer], ((0, 0), (0, LANES - E)))
    x1, h2, aff_tm = _outproj(cfg, merged, _to_bf16(w_out[layer]), x2, mod3, row(g_post_mix[layer]),
                              row(g_pre_ffn[layer]), wr_pad)
    posm, pos_tm, offs = _route(cfg, aff_tm)
    offs_flat = offs.reshape(-1)
    xe = _gather(cfg, offs_flat, posm.reshape(cfg.B * E, 1, cfg.S), h2)
    ye = _ffn(cfg, xe, w_exp_gate[layer], w_exp_up[layer], w_exp_down[layer])
    return _combine(cfg, offs_flat, pos_tm, aff_tm, ye, x1, mod3, row(g_post_ffn[layer]))


def _block(cfg, x, c, positions, w_ada, *rest):
    depth = w_ada.shape[0]
    x2 = x.reshape(cfg.T, cfg.D)
    c_pad = jnp.pad(c, ((0, 8 - cfg.B), (0, 0)))
    pos2 = positions.reshape(cfg.T, 1)
    half = HEAD // 2
    inv_half = ROPE_THETA ** (-jnp.arange(half, dtype=F32) / half)
    inv = jnp.concatenate([inv_half, inv_half]).reshape(1, HEAD)
    sgn = jnp.concatenate([-jnp.ones((half,), F32), jnp.ones((half,), F32)]).reshape(1, HEAD)
    for layer in range(depth):
        x2 = _layer(cfg, layer, x2, c_pad, pos2, inv, sgn, w_ada, *rest)
    return x2.reshape(cfg.B, cfg.S, cfg.D)


def kernel(x, c, positions, w_ada, b_ada, g_pre_mix, g_post_mix, g_pre_ffn, g_post_ffn, w_in,
           hg_lb_logits, hg_out_norm, attn_sink, w_branch_a, w_branch_b, w_out, w_router,
           w_exp_gate, w_exp_up, w_exp_down):
    B, S, D = x.shape
    cfg = Cfg(B=B, S=S, D=D,
              HH=hg_out_norm.shape[1], AH=attn_sink.shape[1],
              KVH=(w_in.shape[2] - 5 * hg_out_norm.shape[1] * HEAD - attn_sink.shape[1] * HEAD - 2 * D) // (2 * HEAD),
              E=w_router.shape[2], FF=w_exp_gate.shape[3])
    return _block(cfg, x, c, positions, w_ada, b_ada, g_pre_mix, g_post_mix, g_pre_ffn, g_post_ffn,
                  w_in, hg_lb_logits, hg_out_norm, attn_sink, w_branch_a, w_branch_b, w_out,
                  w_router, w_exp_gate, w_exp_up, w_exp_down)
```

```python
import functools
from typing import NamedTuple

import jax
import jax.numpy as jnp
from jax import lax
from jax.experimental import pallas as pl
from jax.experimental.pallas import tpu as pltpu

F32 = jnp.float32
BF16 = jnp.bfloat16

LANES = 128
HEAD = 128
HG_CHUNK = 64
WINDOW = 128
ROPE_THETA = 10000.0
NORM_EPS = 1e-6
CAPACITY_FACTOR = 2
VMEM_LIMIT_BYTES = 56 * 1024 * 1024
SEARCH_PROBES = 7
SEARCH_ROUNDS = 51
HGRN_HEADS_PER_STEP = 2
COMBINE_TOKENS = 256
COMBINE_WINDOW = 64
GATHER_EXPERTS = 8
BF16_ROWS = 16


class Cfg(NamedTuple):
    B: int
    S: int
    D: int
    HH: int
    AH: int
    KVH: int
    E: int
    FF: int

    @property
    def T(self):
        return self.B * self.S

    @property
    def C(self):
        return CAPACITY_FACTOR * self.S // self.E

    @property
    def offsets(self):
        hq = self.HH * HEAD
        widths = (hq, hq, hq, hq, hq, self.AH * HEAD, self.KVH * HEAD, self.KVH * HEAD, self.D, self.D)
        offs, o = [], 0
        for w in widths:
            offs.append(o)
            o += w
        return tuple(offs), o


def _params(*sem):
    return pltpu.CompilerParams(dimension_semantics=sem, vmem_limit_bytes=VMEM_LIMIT_BYTES)


def _pick(n, target):
    if n <= target:
        return n
    t = (target // LANES) * LANES
    while t > LANES and n % t:
        t -= LANES
    return t


def _dot(a, b):
    return jnp.dot(a, b, preferred_element_type=F32)


def _dot_nt(a, b):
    return lax.dot_general(a, b, (((1,), (1,)), ((), ())), preferred_element_type=F32)


def _dot_tn(a, b):
    return lax.dot_general(a, b, (((0,), (0,)), ((), ())), preferred_element_type=F32)


def _silu(x):
    return x * jax.nn.sigmoid(x)


def _rms(x):
    return x * lax.rsqrt(jnp.mean(x * x, axis=-1, keepdims=True) + NORM_EPS)


def _ada_kernel(c_ref, w_ref, b_ref, o_ref):
    a = _silu(c_ref[...]).astype(BF16)
    o_ref[...] = _dot(a, w_ref[...].astype(BF16)) + b_ref[...]


def _ada(c_pad, w, b):
    rows, d = c_pad.shape
    n = w.shape[1]
    tn = _pick(n, 1024)
    return pl.pallas_call(
        _ada_kernel,
        grid=(n // tn,),
        in_specs=[pl.BlockSpec((rows, d), lambda j: (0, 0)),
                  pl.BlockSpec((d, tn), lambda j: (0, j)),
                  pl.BlockSpec((1, tn), lambda j: (0, j))],
        out_specs=pl.BlockSpec((rows, tn), lambda j: (0, j)),
        out_shape=jax.ShapeDtypeStruct((rows, n), F32),
        compiler_params=_params("arbitrary"),
        name="ada",
    )(c_pad, w, b)


def _prenorm_kernel(x_ref, g_ref, mod_ref, o_ref):
    y = _rms(x_ref[...]) * g_ref[...]
    shift = mod_ref[0, 0:1, :]
    scale = mod_ref[0, 1:2, :]
    o_ref[...] = (y * (1.0 + scale) + shift).astype(BF16)


def _prenorm(cfg, x2, g, mod3):
    tm = _pick(cfg.S, 1024)
    per_b = cfg.S // tm
    return pl.pallas_call(
        _prenorm_kernel,
        grid=(cfg.T // tm,),
        in_specs=[pl.BlockSpec((tm, cfg.D), lambda i: (i, 0)),
                  pl.BlockSpec((1, cfg.D), lambda i: (0, 0)),
                  pl.BlockSpec((1, 6, cfg.D), lambda i: (i // per_b, 0, 0))],
        out_specs=pl.BlockSpec((tm, cfg.D), lambda i: (i, 0)),
        out_shape=jax.ShapeDtypeStruct((cfg.T, cfg.D), BF16),
        compiler_params=_params("arbitrary"),
        name="prenorm",
    )(x2, g, mod3)


def _inproj_kernel(a_ref, w_ref, o_ref, wbf_ref):
    @pl.when(pl.program_id(1) == 0)
    def _():
        wbf_ref[...] = w_ref[...].astype(BF16)

    o_ref[...] = _dot(a_ref[...], wbf_ref[...]).astype(o_ref.dtype)


def _inproj(a, w):
    m, k = a.shape
    n = w.shape[1]
    tn = _pick(n, 1536)
    tm = _pick(m, 1024)
    return pl.pallas_call(
        _inproj_kernel,
        grid=(n // tn, m // tm),
        in_specs=[pl.BlockSpec((tm, k), lambda j, i: (i, 0)),
                  pl.BlockSpec((k, tn), lambda j, i: (0, j))],
        out_specs=pl.BlockSpec((tm, tn), lambda j, i: (i, j)),
        out_shape=jax.ShapeDtypeStruct((m, n), BF16),
        scratch_shapes=[pltpu.VMEM((k, tn), BF16)],
        compiler_params=_params("arbitrary", "arbitrary"),
        name="inproj",
    )(a, w)


def _rope_kernel(q_ref, k_ref, pos_ref, inv_ref, sgn_ref, qo_ref, ko_ref, *, q_scale):
    ang = pos_ref[...].astype(F32) * inv_ref[...]
    cos = jnp.cos(ang)
    sin = jnp.sin(ang) * sgn_ref[...]

    def rotate(x_ref, o_ref, scale):
        for h in range(x_ref.shape[1] // HEAD):
            cols = slice(h * HEAD, (h + 1) * HEAD)
            x = x_ref[:, cols].astype(F32)
            r = x * cos + pltpu.roll(x, HEAD // 2, 1) * sin
            o_ref[:, cols] = (r if scale is None else r * scale).astype(BF16)

    rotate(q_ref, qo_ref, q_scale)
    rotate(k_ref, ko_ref, None)


def _rope(cfg, proj, pos2, inv, sgn):
    offs, _ = cfg.offsets
    wq, wk = cfg.AH * HEAD, cfg.KVH * HEAD
    assert offs[5] % wq == 0 and offs[6] % wk == 0
    tm = _pick(cfg.T, 512)
    return pl.pallas_call(
        functools.partial(_rope_kernel, q_scale=HEAD ** -0.5),
        grid=(cfg.T // tm,),
        in_specs=[pl.BlockSpec((tm, wq), lambda i: (i, offs[5] // wq)),
                  pl.BlockSpec((tm, wk), lambda i: (i, offs[6] // wk)),
                  pl.BlockSpec((tm, 1), lambda i: (i, 0)),
                  pl.BlockSpec((1, HEAD), lambda i: (0, 0)),
                  pl.BlockSpec((1, HEAD), lambda i: (0, 0))],
        out_specs=[pl.BlockSpec((tm, wq), lambda i: (i, 0)),
                   pl.BlockSpec((tm, wk), lambda i: (i, 0))],
        out_shape=[jax.ShapeDtypeStruct((cfg.T, wq), BF16),
                   jax.ShapeDtypeStruct((cfg.T, wk), BF16)],
        compiler_params=_params("arbitrary"),
        name="rope",
    )(proj, proj, pos2, inv, sgn)


def _attn_kernel(sink_ref, q_ref, kp_ref, kc_ref, kn_ref, vp_ref, vc_ref, vn_ref, o_ref, *, G, KVH, NB):
    n = pl.program_id(1)
    blk = WINDOW
    row = lax.broadcasted_iota(jnp.int32, (G * blk, 3 * blk), 0)
    qi = row - (row // blk) * blk
    kj = lax.broadcasted_iota(jnp.int32, (G * blk, 3 * blk), 1)
    valid = jnp.abs(kj - blk - qi) <= WINDOW
    valid = valid & ((kj >= blk) | (n > 0)) & ((kj < 2 * blk) | (n < NB - 1))
    head = lax.broadcasted_iota(jnp.int32, (G * blk, 1), 0) // blk
    for kvh in range(KVH):
        kv = slice(kvh * HEAD, (kvh + 1) * HEAD)
        k = jnp.concatenate([kp_ref[:, kv], kc_ref[:, kv], kn_ref[:, kv]], axis=0)
        v = jnp.concatenate([vp_ref[:, kv], vc_ref[:, kv], vn_ref[:, kv]], axis=0).astype(BF16)
        heads = [kvh * G + g for g in range(G)]
        q = jnp.concatenate([q_ref[:, h * HEAD:(h + 1) * HEAD] for h in heads], axis=0)
        sk = jnp.zeros((G * blk, 1), F32)
        for g, h in enumerate(heads):
            sk = jnp.where(head == g, sink_ref[h], sk)
        s = jnp.where(valid, _dot_nt(q, k), -jnp.inf)
        m = jnp.maximum(jnp.max(s, axis=-1, keepdims=True), sk)
        p = jnp.exp(s - m)
        denom = jnp.sum(p, axis=-1, keepdims=True) + jnp.exp(sk - m)
        o = (_dot(p.astype(BF16), v) / denom).astype(BF16)
        for g, h in enumerate(heads):
            o_ref[:, h * HEAD:(h + 1) * HEAD] = o[g * blk:(g + 1) * blk]


def _attn(cfg, q_rot, k_rot, proj, sink):
    offs, _ = cfg.offsets
    G = cfg.AH // cfg.KVH
    blk = WINDOW
    NB = cfg.S // blk
    wq, wkv = cfg.AH * HEAD, cfg.KVH * HEAD
    assert offs[7] % wkv == 0
    vcol = offs[7] // wkv
    prev = lambda n: jnp.maximum(n - 1, 0)
    nxt = lambda n: jnp.minimum(n + 1, NB - 1)
    return pl.pallas_call(
        functools.partial(_attn_kernel, G=G, KVH=cfg.KVH, NB=NB),
        grid=(cfg.B, NB),
        in_specs=[pl.BlockSpec(memory_space=pltpu.SMEM),
                  pl.BlockSpec((blk, wq), lambda b, n: (b * NB + n, 0)),
                  pl.BlockSpec((blk, wkv), lambda b, n: (b * NB + prev(n), 0)),
                  pl.BlockSpec((blk, wkv), lambda b, n: (b * NB + n, 0)),
                  pl.BlockSpec((blk, wkv), lambda b, n: (b * NB + nxt(n), 0)),
                  pl.BlockSpec((blk, wkv), lambda b, n: (b * NB + prev(n), vcol)),
                  pl.BlockSpec((blk, wkv), lambda b, n: (b * NB + n, vcol)),
                  pl.BlockSpec((blk, wkv), lambda b, n: (b * NB + nxt(n), vcol))],
        out_specs=pl.BlockSpec((blk, wq), lambda b, n: (b * NB + n, 0)),
        out_shape=jax.ShapeDtypeStruct((cfg.T, wq), BF16),
        compiler_params=_params("arbitrary", "arbitrary"),
        name="attn",
    )(sink, q_rot, k_rot, k_rot, k_rot, proj, proj, proj)


def _lower_bound(lbl_ref, d, layer):
    n = lbl_ref.shape[1]
    rows = [lbl_ref[d, j:j + 1, :] for j in range(n)]
    m = functools.reduce(jnp.maximum, rows)
    es = [jnp.exp(r - m) for r in rows]
    return sum(es[:layer + 1]) / sum(es)


def _hgrn_cumsum(logfs, tri2):
    P = 2 * HG_CHUNK
    npair = logfs[0].shape[0] // P
    cols = []
    for lf in logfs:
        hi = lf.astype(BF16)
        lo = (lf - hi.astype(F32)).astype(BF16)
        for p in range(npair):
            cols += [hi[p * P:(p + 1) * P], lo[p * P:(p + 1) * P]]
    both = _dot(tri2.astype(BF16), jnp.concatenate(cols, axis=1))
    outs = []
    for s in range(len(logfs)):
        parts = []
        for p in range(npair):
            c0 = (s * npair + p) * 2 * HEAD
            parts.append(both[:, c0:c0 + HEAD] + both[:, c0 + HEAD:c0 + 2 * HEAD])
        outs.append(jnp.concatenate(parts, axis=0))
    return outs


def _hgrn_tile(qs, k, b, v, st, mask, fwd):
    L = HG_CHUNK
    nc = qs.shape[0] // L
    ref_i, last_i = (L // 2, L - 1) if fwd else (L - 1 - L // 2, 0)

    def per_chunk(rows):
        return jnp.concatenate([jnp.broadcast_to(r, (L, HEAD)) for r in rows], axis=0)

    brefs = [b[c * L + ref_i:c * L + ref_i + 1, :] for c in range(nc)]
    blasts = [b[c * L + last_i:c * L + last_i + 1, :] for c in range(nc)]
    d = b - per_chunk(brefs)
    q_in = qs * jnp.exp(d)
    k_in = k * jnp.exp(-d)
    a = jnp.where(mask, _dot_nt(q_in.astype(BF16), k_in.astype(BF16)), 0.0)
    v_bf = v.astype(BF16)
    o_intra = _dot(a.astype(BF16), v_bf)
    q_dec = (q_in * per_chunk([jnp.exp(r) for r in brefs])).astype(BF16)
    k_dec = k_in * per_chunk([jnp.exp(l - r) for l, r in zip(blasts, brefs)])

    first = lax.broadcasted_iota(jnp.int32, (2 * L, HEAD), 0) < L
    cs = []
    for p in range(nc // 2):
        r = slice(2 * p * L, 2 * (p + 1) * L)
        kd = k_dec[r]
        zero = jnp.zeros_like(kd)
        k_blk = jnp.concatenate([jnp.where(first, kd, zero), jnp.where(first, zero, kd)], axis=1).astype(BF16)
        both = _dot_tn(v_bf[r], k_blk)
        cs += [both[:, :HEAD], both[:, HEAD:]]
    entering = [None] * nc
    for c in (range(nc) if fwd else reversed(range(nc))):
        entering[c] = st
        st = st * jnp.exp(blasts[c]) + cs[c]
    o_inter = []
    for p in range(nc // 2):
        r = slice(2 * p * L, 2 * (p + 1) * L)
        st2 = jnp.concatenate([entering[2 * p], entering[2 * p + 1]], axis=0).astype(BF16)
        both = _dot_nt(q_dec[r], st2)
        o_inter += [both[:L, :HEAD], both[L:, HEAD:]]
    return o_intra + jnp.concatenate(o_inter, axis=0), st


def _hgrn_kernel(lbl_ref, gain_ref, qf_ref, vf_ref, zf_ref, qb_ref, vb_ref, zb_ref, g_ref, o_ref,
                 st_ref, of_ref, ob_ref, *, NT, R, HP, layer):
    i = pl.program_id(2)

    def chunk_masks(n):
        li = lax.broadcasted_iota(jnp.int32, (n, n), 0)
        mi = lax.broadcasted_iota(jnp.int32, (n, n), 1)
        same = (li // HG_CHUNK) == (mi // HG_CHUNK)
        return same & (mi <= li), same & (mi >= li)

    tri_f, tri_b = (jnp.where(m, 1.0, 0.0).astype(F32) for m in chunk_masks(2 * HG_CHUNK))
    mask_f, mask_b = chunk_masks(R)

    @pl.when(i == 0)
    def _():
        st_ref[...] = jnp.zeros_like(st_ref)

    lanes = [slice(p * HEAD, (p + 1) * HEAD) for p in range(HP)]
    for d, (q_ref, v_ref, z_ref) in enumerate(((qf_ref, vf_ref, zf_ref), (qb_ref, vb_ref, zb_ref))):
        fwd = d == 0
        lb = _lower_bound(lbl_ref, d, layer)
        f = lb + (1.0 - lb) * jax.nn.sigmoid(z_ref[...].astype(F32))
        k = 1.0 - f
        qs = _silu(q_ref[...].astype(F32))
        cums = _hgrn_cumsum([jnp.log(f)[:, lanes[p]] for p in range(HP)], tri_f if fwd else tri_b)
        tile = i if fwd else NT - 1 - i
        rows = pl.ds(pl.multiple_of(tile * R, R), R)
        for p in range(HP):
            o, st = _hgrn_tile(qs[:, lanes[p]], k[:, lanes[p]], cums[p], v_ref[:, lanes[p]], st_ref[p, d],
                               mask_f if fwd else mask_b, fwd)
            (of_ref if fwd else ob_ref)[rows, lanes[p]] = o
            st_ref[p, d] = st

    @pl.when(i == NT - 1)
    def _():
        for t in range(NT):
            rows = slice(t * R, (t + 1) * R)
            for p in range(HP):
                o = of_ref[rows, lanes[p]] + ob_ref[rows, lanes[p]]
                gate = _silu(g_ref[rows, lanes[p]].astype(F32))
                o_ref[rows, lanes[p]] = (_rms(o) * gain_ref[0, :, lanes[p]] * gate).astype(BF16)


def _hgrn(cfg, proj, lb_logits, gain3, layer):
    offs, _ = cfg.offsets
    HP = HGRN_HEADS_PER_STEP
    W = HP * HEAD
    assert cfg.HH % HP == 0 and all(o % W == 0 for o in offs[:5])
    qc, fc, bc, ic, gc = (o // W for o in offs[:5])
    R = _pick(cfg.S, 512)
    NT = cfg.S // R
    nl = lb_logits.shape[1]
    fwd = lambda col: pl.BlockSpec((R, W), lambda b, h, i: (b * NT + i, col + h))
    bwd = lambda col: pl.BlockSpec((R, W), lambda b, h, i: (b * NT + NT - 1 - i, col + h))
    return pl.pallas_call(
        functools.partial(_hgrn_kernel, NT=NT, R=R, HP=HP, layer=layer),
        grid=(cfg.B, cfg.HH // HP, NT),
        in_specs=[pl.BlockSpec((2, nl, W), lambda b, h, i: (0, 0, h)),
                  pl.BlockSpec((1, 1, W), lambda b, h, i: (h, 0, 0)),
                  fwd(qc), fwd(ic), fwd(fc), bwd(qc), bwd(ic), bwd(bc),
                  pl.BlockSpec((cfg.S, W), lambda b, h, i: (b, gc + h))],
        out_specs=pl.BlockSpec((cfg.S, W), lambda b, h, i: (b, h)),
        out_shape=jax.ShapeDtypeStruct((cfg.T, cfg.HH * HEAD), BF16),
        scratch_shapes=[pltpu.VMEM((HP, 2, HEAD, HEAD), F32),
                        pltpu.VMEM((cfg.S, W), F32), pltpu.VMEM((cfg.S, W), F32)],
        compiler_params=_params("arbitrary", "arbitrary", "arbitrary"),
        name="hgrn",
    )(lb_logits, gain3, proj, proj, proj, proj, proj, proj, proj)


def _merge_kernel(oa_ref, ob_ref, *refs, nsub):
    ga_refs, gb_refs = refs[:nsub], refs[nsub:2 * nsub]
    wa_ref, wb_ref, o_ref, wabf_ref, wbbf_ref = refs[2 * nsub:]

    @pl.when(pl.program_id(1) == 0)
    def _():
        wabf_ref[...] = wa_ref[...].astype(BF16)
        wbbf_ref[...] = wb_ref[...].astype(BF16)

    ya = _dot(oa_ref[...], wabf_ref[...])
    yb = _dot(ob_ref[...], wbbf_ref[...])
    gw = ya.shape[1] // nsub
    for u in range(nsub):
        cols = slice(u * gw, (u + 1) * gw)
        o_ref[:, cols] = (jax.nn.sigmoid(ga_refs[u][...].astype(F32)) * ya[:, cols]
                          + jax.nn.sigmoid(gb_refs[u][...].astype(F32)) * yb[:, cols]).astype(BF16)


def _merge(cfg, oa, ob, proj, wa, wb):
    offs, _ = cfg.offsets
    gw = _pick(cfg.D, 512)
    assert offs[8] % gw == 0 and offs[9] % gw == 0
    nsub = 2 if cfg.D % (2 * gw) == 0 else 1
    tn = nsub * gw
    ga0, gb0 = offs[8] // gw, offs[9] // gw
    tm = _pick(cfg.T, 1024)
    ka, kb = oa.shape[1], ob.shape[1]
    gate = lambda c0, u: pl.BlockSpec((tm, gw), lambda j, i: (i, c0 + nsub * j + u))
    return pl.pallas_call(
        functools.partial(_merge_kernel, nsub=nsub),
        grid=(cfg.D // tn, cfg.T // tm),
        in_specs=[pl.BlockSpec((tm, ka), lambda j, i: (i, 0)),
                  pl.BlockSpec((tm, kb), lambda j, i: (i, 0)),
                  *[gate(ga0, u) for u in range(nsub)],
                  *[gate(gb0, u) for u in range(nsub)],
                  pl.BlockSpec((ka, tn), lambda j, i: (0, j)),
                  pl.BlockSpec((kb, tn), lambda j, i: (0, j))],
        out_specs=pl.BlockSpec((tm, tn), lambda j, i: (i, j)),
        out_shape=jax.ShapeDtypeStruct((cfg.T, cfg.D), BF16),
        scratch_shapes=[pltpu.VMEM((ka, tn), BF16), pltpu.VMEM((kb, tn), BF16)],
        compiler_params=_params("arbitrary", "arbitrary"),
        name="merge",
    )(oa, ob, *([proj] * (2 * nsub)), wa, wb)


def _cast_kernel(w_ref, o_ref):
    o_ref[...] = w_ref[...].astype(BF16)


def _to_bf16(w):
    k, n = w.shape
    tk = _pick(k, 512)
    return pl.pallas_call(
        _cast_kernel,
        grid=(k // tk,),
        in_specs=[pl.BlockSpec((tk, n), lambda i: (i, 0))],
        out_specs=pl.BlockSpec((tk, n), lambda i: (i, 0)),
        out_shape=jax.ShapeDtypeStruct((k, n), BF16),
        compiler_params=_params("arbitrary"),
        name="wcast",
    )(w)


def _outproj_kernel(a_ref, w_ref, x_ref, mod_ref, gpost_ref, gpre_ref, wr_ref,
                    x1_ref, h_ref, aff_ref, y_ref, *, E, SUB):
    i = pl.program_id(0)

    @pl.when(i == 0)
    def _():
        y_ref[...] = jnp.zeros_like(y_ref)

    def fused(cur, prev):
        y_ref[cur] = _dot(a_ref[...], w_ref[...])
        wr = wr_ref[...]
        w_hi = wr.astype(BF16)
        w_pair = jnp.concatenate([w_hi, (wr - w_hi.astype(F32)).astype(BF16)], axis=1)
        post_gain = mod_ref[0, 2:3, :] * gpost_ref[...]
        pre_gain = gpre_ref[...] * (1.0 + mod_ref[0, 4:5, :])
        for u in range(a_ref.shape[0] // SUB):
            rows = slice(u * SUB, (u + 1) * SUB)
            x1 = x_ref[rows, :] + _rms(y_ref[prev, rows, :]) * post_gain
            x1_ref[rows, :] = x1
            h = _rms(x1) * pre_gain + mod_ref[0, 3:4, :]
            h_bf = h.astype(BF16)
            h_ref[rows, :] = h_bf
            both = _dot(h_bf, w_pair)
            logits = both[:, :LANES] + both[:, LANES:]
            lane = lax.broadcasted_iota(jnp.int32, logits.shape, 1)
            logits = jnp.where(lane < E, logits, -jnp.inf)
            p = jnp.exp(logits - jnp.max(logits, axis=-1, keepdims=True))
            aff_ref[rows, :] = p / jnp.sum(p, axis=-1, keepdims=True)

    @pl.when(i % 2 == 0)
    def _():
        fused(0, 1)

    @pl.when(i % 2 == 1)
    def _():
        fused(1, 0)


def _outproj(cfg, merged, w_bf, x2, mod3, gpost, gpre, wr_pad):
    tm = _pick(cfg.S, 512)
    sub = _pick(tm, 256)
    per_b = cfg.S // tm
    n = cfg.T // tm
    D = cfg.D
    const = lambda i: (0, 0)
    cur = lambda i: (jnp.minimum(i, n - 1), 0)
    lag = lambda i: (jnp.maximum(i - 1, 0), 0)
    return pl.pallas_call(
        functools.partial(_outproj_kernel, E=cfg.E, SUB=sub),
        grid=(n + 1,),
        in_specs=[pl.BlockSpec((tm, D), cur),
                  pl.BlockSpec((D, D), const, pipeline_mode=pl.Buffered(1)),
                  pl.BlockSpec((tm, D), lag),
                  pl.BlockSpec((1, 6, D), lambda i: (jnp.maximum(i - 1, 0) // per_b, 0, 0)),
                  pl.BlockSpec((1, D), const),
                  pl.BlockSpec((1, D), const),
                  pl.BlockSpec((D, LANES), const)],
        out_specs=[pl.BlockSpec((tm, D), lag),
                   pl.BlockSpec((tm, D), lag),
                   pl.BlockSpec((tm, LANES), lag)],
        out_shape=[jax.ShapeDtypeStruct((cfg.T, D), F32),
                   jax.ShapeDtypeStruct((cfg.T, D), BF16),
                   jax.ShapeDtypeStruct((cfg.T, LANES), F32)],
        scratch_shapes=[pltpu.VMEM((2, tm, D), F32)],
        compiler_params=_params("arbitrary"),
        name="outproj",
    )(merged, w_bf, x2, mod3, gpost, gpre, wr_pad)


def _route_kernel(aff_ref, posm_ref, postm_ref, offs_ref, a_ref, *, E, S, C):
    nt = S // LANES
    for t in range(nt):
        a_ref[:, t * LANES:(t + 1) * LANES] = aff_ref[t * LANES:(t + 1) * LANES, :].T[:E, :]

    cap = jnp.float32(C)

    def narrow(_, carry):
        lo, hi = carry
        width = hi - lo
        a = a_ref[...]
        new_lo, new_hi = lo, hi
        found = jnp.zeros(lo.shape, jnp.bool_)
        for j in range(SEARCH_PROBES, 0, -1):
            m = lo + (j / (SEARCH_PROBES + 1)) * width
            keep = jnp.sum(jnp.where(a >= m, 1.0, 0.0), axis=1, keepdims=True) >= cap
            new_lo = jnp.where(keep & ~found, m, new_lo)
            new_hi = jnp.where(keep | found, new_hi, m)
            found = found | keep
        return new_lo, new_hi

    lo, hi = lax.fori_loop(0, SEARCH_ROUNDS, narrow,
                           (jnp.zeros((E, 1), F32), jnp.full((E, 1), 2.0, F32)))
    need = cap - jnp.sum(jnp.where(a_ref[...] >= hi, 1.0, 0.0), axis=1, keepdims=True)

    ui = lax.broadcasted_iota(jnp.int32, (LANES, LANES), 0)
    uj = lax.broadcasted_iota(jnp.int32, (LANES, LANES), 1)
    upper = jnp.where(ui <= uj, 1.0, 0.0).astype(BF16)
    lane = lax.broadcasted_iota(jnp.int32, (E, LANES), 1)
    run_tie = jnp.zeros((E, 1), F32)
    run_sel = jnp.zeros((E, 1), F32)
    offs = jnp.zeros((E, LANES), F32)
    pad = jnp.full((LANES - E, LANES), -1.0, F32)
    for t in range(nt):
        cols = slice(t * LANES, (t + 1) * LANES)
        a = a_ref[:, cols]
        tie = (a >= lo) & (a < hi)
        tie_t = jnp.where(tie, 1.0, 0.0)
        tie_rank = _dot(tie_t.astype(BF16), upper) - tie_t + run_tie
        sel = (a >= hi) | (tie & (tie_rank < need))
        sel_t = jnp.where(sel, 1.0, 0.0)
        pos = _dot(sel_t.astype(BF16), upper) - sel_t + run_sel
        posm = jnp.where(sel, pos, -1.0)
        posm_ref[0, :, cols] = posm
        postm_ref[cols, :] = jnp.concatenate([posm, pad], axis=0).T
        offs = jnp.where(lane == t, run_sel, offs)
        run_tie = run_tie + jnp.sum(tie_t, axis=1, keepdims=True)
        run_sel = run_sel + jnp.sum(sel_t, axis=1, keepdims=True)
    offs = jnp.where(lane == nt, run_sel, offs)
    offs_ref[0] = offs.astype(jnp.int32)


def _route(cfg, aff_tm):
    E, S = cfg.E, cfg.S
    assert S // LANES < LANES and E % 8 == 0
    return pl.pallas_call(
        functools.partial(_route_kernel, E=E, S=S, C=cfg.C),
        grid=(cfg.B,),
        in_specs=[pl.BlockSpec((S, LANES), lambda b: (b, 0))],
        out_specs=[pl.BlockSpec((1, E, S), lambda b: (b, 0, 0)),
                   pl.BlockSpec((S, LANES), lambda b: (b, 0)),
                   pl.BlockSpec((1, E, LANES), lambda b: (b, 0, 0))],
        out_shape=[jax.ShapeDtypeStruct((cfg.B, E, S), F32),
                   jax.ShapeDtypeStruct((cfg.T, LANES), F32),
                   jax.ShapeDtypeStruct((cfg.B, E, LANES), jnp.int32)],
        scratch_shapes=[pltpu.VMEM((E, S), F32)],
        compiler_params=_params("arbitrary"),
        name="route",
    )(aff_tm)


def _slot_window(offs_ref, row, token_block, C):
    base = row * LANES + token_block * (COMBINE_TOKENS // LANES)
    first, end = offs_ref[base], offs_ref[base + COMBINE_TOKENS // LANES]
    start = jnp.minimum((first // BF16_ROWS) * BF16_ROWS, C - COMBINE_WINDOW)
    return pl.multiple_of(start, BF16_ROWS), end


def _extra_windows(start, end):
    return (jnp.maximum(end - start - COMBINE_WINDOW, 0) + COMBINE_WINDOW - 1) // COMBINE_WINDOW


def _gather_kernel(offs_ref, posm_ref, h_ref, x_ref, *, E, G, C):
    b, grp, kb = pl.program_id(0), pl.program_id(1), pl.program_id(2)
    W, TB = COMBINE_WINDOW, COMBINE_TOKENS

    @pl.when(kb == 0)
    def _():
        x_ref[...] = jnp.zeros_like(x_ref)

    slot0 = lax.broadcasted_iota(jnp.int32, (W, TB), 0)
    starts, pieces = [], []
    for j in range(G):
        start, _ = _slot_window(offs_ref, b * E + grp * G + j, kb, C)
        starts.append(start)
        pieces.append(jnp.where(posm_ref[j] == (start + slot0).astype(F32), 1.0, 0.0).astype(BF16))
    rows = _dot(jnp.concatenate(pieces, axis=0), h_ref[...]).astype(BF16)
    for j in range(G):
        win = pl.ds(starts[j], W)
        x_ref[j, win, :] = x_ref[j, win, :] + rows[j * W:(j + 1) * W]

    def spill(j, carry):
        start, end = _slot_window(offs_ref, b * E + grp * G + j, kb, C)

        def extra(w, c2):
            cur = start + (w + 1) * W
            s2 = pl.multiple_of(jnp.minimum(cur, C - W), BF16_ROWS)
            slot = s2 + slot0
            onehot = jnp.where((posm_ref[j] == slot.astype(F32)) & (slot >= cur), 1.0, 0.0).astype(BF16)
            win = pl.ds(s2, W)
            x_ref[j, win, :] = x_ref[j, win, :] + _dot(onehot, h_ref[...]).astype(BF16)
            return c2

        return lax.fori_loop(0, _extra_windows(start, end), extra, carry)

    lax.fori_loop(0, G, spill, 0)


def _gather(cfg, offs_flat, posm3, h2):
    B, E, S, C, D = cfg.B, cfg.E, cfg.S, cfg.C, cfg.D
    G, TB = min(E, GATHER_EXPERTS), COMBINE_TOKENS
    assert E % G == 0 and S % TB == 0 and C % COMBINE_WINDOW == 0
    nk = S // TB
    grid_spec = pltpu.PrefetchScalarGridSpec(
        num_scalar_prefetch=1,
        grid=(B, E // G, nk),
        in_specs=[pl.BlockSpec((G, 1, TB), lambda b, g, k, offs: (b * (E // G) + g, 0, k)),
                  pl.BlockSpec((TB, D), lambda b, g, k, offs: (b * nk + k, 0))],
        out_specs=pl.BlockSpec((G, C, D), lambda b, g, k, offs: (g, b, 0)),
    )
    return pl.pallas_call(
        functools.partial(_gather_kernel, E=E, G=G, C=C),
        grid_spec=grid_spec,
        out_shape=jax.ShapeDtypeStruct((E, B * C, D), BF16),
        compiler_params=_params("arbitrary", "arbitrary", "arbitrary"),
        name="gather",
    )(offs_flat, posm3, h2)


def _ffn_kernel(x_ref, wg_ref, wu_ref, wd_ref, y_ref, acc_ref):
    f = pl.program_id(1)

    @pl.when(f == 0)
    def _():
        acc_ref[...] = jnp.zeros_like(acc_ref)

    x = x_ref[0]
    a = (_silu(_dot(x, wg_ref[0].astype(BF16))) * _dot(x, wu_ref[0].astype(BF16))).astype(BF16)
    acc_ref[...] += _dot(a, wd_ref[0].astype(BF16))

    @pl.when(f == pl.num_programs(1) - 1)
    def _():
        y_ref[0] = acc_ref[...].astype(BF16)


def _ffn(cfg, xe, wg, wu, wd):
    B, E, C, D, FF = cfg.B, cfg.E, cfg.C, cfg.D, cfg.FF
    tf = _pick(FF, 256)
    return pl.pallas_call(
        _ffn_kernel,
        grid=(E, FF // tf),
        in_specs=[pl.BlockSpec((1, B * C, D), lambda e, f: (e, 0, 0)),
                  pl.BlockSpec((1, D, tf), lambda e, f: (e, 0, f)),
                  pl.BlockSpec((1, D, tf), lambda e, f: (e, 0, f)),
                  pl.BlockSpec((1, tf, D), lambda e, f: (e, f, 0))],
        out_specs=pl.BlockSpec((1, B * C, D), lambda e, f: (e, 0, 0)),
        out_shape=jax.ShapeDtypeStruct((E, B * C, D), BF16),
        scratch_shapes=[pltpu.VMEM((B * C, D), F32)],
        compiler_params=_params("arbitrary", "arbitrary"),
        name="ffn",
    )(xe, wg, wu, wd)


def _combine_kernel(offs_ref, pos_ref, aff_ref, ppos_ref, paff_ref, y_ref, x1_ref, mod_ref, g_ref, o_ref,
                    z_ref, acc_ref, *, E, C):
    b, k = pl.program_id(0), pl.program_id(1)
    W, TB = COMBINE_WINDOW, COMBINE_TOKENS
    lane = lax.broadcasted_iota(jnp.int32, (TB, LANES), 1)

    @pl.when(k == 0)
    def _():
        acc_ref[...] = jnp.zeros_like(acc_ref)

    def spill(e, carry):
        start, end = _slot_window(offs_ref, b * E + e, k - 1, C)

        def extra(w, c2):
            mine = lane == e
            pos = jnp.sum(jnp.where(mine, ppos_ref[...], 0.0), axis=1, keepdims=True)
            aff = jnp.sum(jnp.where(mine, paff_ref[...], 0.0), axis=1, keepdims=True)
            cur = start + (w + 1) * W
            s2 = pl.multiple_of(jnp.minimum(cur, C - W), BF16_ROWS)
            slot = s2 + lax.broadcasted_iota(jnp.int32, (TB, W), 1)
            m = jnp.where((pos == slot.astype(F32)) & (slot >= cur), aff, 0.0).astype(BF16)
            acc_ref[(k + 1) % 2] += _dot(m, y_ref[e, pl.ds(s2, W), :])
            return c2

        return lax.fori_loop(0, _extra_windows(start, end), extra, carry)

    @pl.when(k > 0)
    def _():
        lax.fori_loop(0, E, spill, 0)

    def fused(cur, prev):
        o_ref[...] = x1_ref[...] + mod_ref[0, 5:6, :] * (_rms(acc_ref[prev]) * g_ref[...])

        starts = []
        for e in range(E):
            start, _ = _slot_window(offs_ref, b * E + e, k, C)
            z_ref[e * W:(e + 1) * W, :] = y_ref[e, pl.ds(start, W), :]
            starts.append(start)
        seg = lane // W
        off = lane - seg * W
        group = LANES // W
        pieces = []
        for p in range(E // group):
            slot = jnp.zeros((TB, LANES), jnp.int32)
            pos = jnp.full((TB, LANES), -1.0, F32)
            aff = jnp.zeros((TB, LANES), F32)
            for q in range(group):
                e = p * group + q
                mine = seg == q
                slot = jnp.where(mine, starts[e] + off, slot)
                pos = jnp.where(mine, pos_ref[:, e:e + 1], pos)
                aff = jnp.where(mine, aff_ref[:, e:e + 1], aff)
            pieces.append(jnp.where(pos == slot.astype(F32), aff, 0.0).astype(BF16))
        acc_ref[cur] = _dot(jnp.concatenate(pieces, axis=1), z_ref[...])

    @pl.when(k % 2 == 0)
    def _():
        fused(0, 1)

    @pl.when(k % 2 == 1)
    def _():
        fused(1, 0)


def _combine(cfg, offs_flat, pos_tm, aff_tm, y, x1, mod3, g):
    B, E, S, C, D = cfg.B, cfg.E, cfg.S, cfg.C, cfg.D
    W, TB = COMBINE_WINDOW, COMBINE_TOKENS
    assert LANES % W == 0 and E % (LANES // W) == 0 and C % W == 0 and W % BF16_ROWS == 0 and S % TB == 0
    nk = S // TB
    assert (nk + 2) * (TB // LANES) <= LANES
    cur = lambda b, k, offs: (b * nk + jnp.minimum(k, nk - 1), 0)
    lag = lambda b, k, offs: (b * nk + jnp.maximum(k - 1, 0), 0)
    grid_spec = pltpu.PrefetchScalarGridSpec(
        num_scalar_prefetch=1,
        grid=(B, nk + 1),
        in_specs=[pl.BlockSpec((TB, LANES), cur),
                  pl.BlockSpec((TB, LANES), cur),
                  pl.BlockSpec((TB, LANES), lag),
                  pl.BlockSpec((TB, LANES), lag),
                  pl.BlockSpec((E, C, D), lambda b, k, offs: (0, b, 0), pipeline_mode=pl.Buffered(1)),
                  pl.BlockSpec((TB, D), lag),
                  pl.BlockSpec((1, 6, D), lambda b, k, offs: (b, 0, 0)),
                  pl.BlockSpec((1, D), lambda b, k, offs: (0, 0))],
        out_specs=pl.BlockSpec((TB, D), lag),
        scratch_shapes=[pltpu.VMEM((E * W, D), BF16), pltpu.VMEM((2, TB, D), F32)],
    )
    return pl.pallas_call(
        functools.partial(_combine_kernel, E=E, C=C),
        grid_spec=grid_spec,
        out_shape=jax.ShapeDtypeStruct((cfg.T, D), F32),
        compiler_params=_params("arbitrary", "arbitrary"),
        name="combine",
    )(offs_flat, pos_tm, aff_tm, pos_tm, aff_tm, y, x1, mod3, g)


def _layer(cfg, layer, x2, c_pad, pos2, inv, sgn, w_ada, b_ada, g_pre_mix, g_post_mix, g_pre_ffn,
           g_post_ffn, w_in, hg_lb_logits, hg_out_norm, attn_sink, w_branch_a, w_branch_b, w_out,
           w_router, w_exp_gate, w_exp_up, w_exp_down):
    D, E = cfg.D, cfg.E
    row = lambda v: v.reshape(1, -1)
    mod = _ada(c_pad, w_ada[layer], row(b_ada[layer]))[:cfg.B]
    mod3 = mod.reshape(cfg.B, 6, D)
    h = _prenorm(cfg, x2, row(g_pre_mix[layer]), mod3)
    proj = _inproj(h, w_in[layer])
    q_rot, k_rot = _rope(cfg, proj, pos2, inv, sgn)
    att = _attn(cfg, q_rot, k_rot, proj, attn_sink[layer])
    gain3 = hg_out_norm[layer].reshape(cfg.HH // HGRN_HEADS_PER_STEP, 1, HGRN_HEADS_PER_STEP * HEAD)
    oa = _hgrn(cfg, proj, hg_lb_logits, gain3, layer)
    merged = _merge(cfg, oa, att, proj, w_branch_a[layer], w_branch_b[layer])
    wr_pad = jnp.pad(w_router[layer], ((0, 0), (0, LANES - E)))
    x1, h2, aff_tm = _outproj(cfg, merged, _to_bf16(w_out[layer]), x2, mod3, row(g_post_mix[layer]),
                              row(g_pre_ffn[layer]), wr_pad)
    posm, pos_tm, offs = _route(cfg, aff_tm)
    offs_flat = offs.reshape(-1)
    xe = _gather(cfg, offs_flat, posm.reshape(cfg.B * E, 1, cfg.S), h2)
    ye = _ffn(cfg, xe, w_exp_gate[layer], w_exp_up[layer], w_exp_down[layer])
    return _combine(cfg, offs_flat, pos_tm, aff_tm, ye, x1, mod3, row(g_post_ffn[layer]))


def _block(cfg, x, c, positions, w_ada, *rest):
    depth = w_ada.shape[0]
    x2 = x.reshape(cfg.T, cfg.D)
    c_pad = jnp.pad(c, ((0, 8 - cfg.B), (0, 0)))
    pos2 = positions.reshape(cfg.T, 1)
    half = HEAD // 2
    inv_half = ROPE_THETA ** (-jnp.arange(half, dtype=F32) / half)
    inv = jnp.concatenate([inv_half, inv_half]).reshape(1, HEAD)
    sgn = jnp.concatenate([-jnp.ones((half,), F32), jnp.ones((half,), F32)]).reshape(1, HEAD)
    for layer in range(depth):
        x2 = _layer(cfg, layer, x2, c_pad, pos2, inv, sgn, w_ada, *rest)
    return x2.reshape(cfg.B, cfg.S, cfg.D)


def kernel(x, c, positions, w_ada, b_ada, g_pre_mix, g_post_mix, g_pre_ffn, g_post_ffn, w_in,
           hg_lb_logits, hg_out_norm, attn_sink, w_branch_a, w_branch_b, w_out, w_router,
           w_exp_gate, w_exp_up, w_exp_down):
    B, S, D = x.shape
    cfg = Cfg(B=B, S=S, D=D,
              HH=hg_out_norm.shape[1], AH=attn_sink.shape[1],
              KVH=(w_in.shape[2] - 5 * hg_out_norm.shape[1] * HEAD - attn_sink.shape[1] * HEAD - 2 * D) // (2 * HEAD),
              E=w_router.shape[2], FF=w_exp_gate.shape[3])
    return _block(cfg, x, c, positions, w_ada, b_ada, g_pre_mix, g_post_mix, g_pre_ffn, g_post_ffn,
                  w_in, hg_lb_logits, hg_out_norm, attn_sink, w_branch_a, w_branch_b, w_out,
                  w_router, w_exp_gate, w_exp_up, w_exp_down)
```

```python
import functools
from typing import NamedTuple

import jax
import jax.numpy as jnp
from jax import lax
from jax.experimental import pallas as pl
from jax.experimental.pallas import tpu as pltpu

F32 = jnp.float32
BF16 = jnp.bfloat16

LANES = 128
HEAD = 128
HG_CHUNK = 64
WINDOW = 128
ROPE_THETA = 10000.0
NORM_EPS = 1e-6
CAPACITY_FACTOR = 2
VMEM_LIMIT_BYTES = 56 * 1024 * 1024
SEARCH_PROBES = 3
SEARCH_ROUNDS = 76
HGRN_HEADS_PER_STEP = 2
HGRN_SCORE_ROWS = 256
COMBINE_TOKENS = 256
COMBINE_WINDOW = 64
GATHER_EXPERTS = 8
BF16_ROWS = 16


class Cfg(NamedTuple):
    B: int
    S: int
    D: int
    HH: int
    AH: int
    KVH: int
    E: int
    FF: int

    @property
    def T(self):
        return self.B * self.S

    @property
    def C(self):
        return CAPACITY_FACTOR * self.S // self.E

    @property
    def offsets(self):
        hq = self.HH * HEAD
        widths = (hq, hq, hq, hq, hq, self.AH * HEAD, self.KVH * HEAD, self.KVH * HEAD, self.D, self.D)
        offs, o = [], 0
        for w in widths:
            offs.append(o)
            o += w
        return tuple(offs), o


def _params(*sem):
    return pltpu.CompilerParams(dimension_semantics=sem, vmem_limit_bytes=VMEM_LIMIT_BYTES)


def _pick(n, target):
    if n <= target:
        return n
    t = (target // LANES) * LANES
    while t > LANES and n % t:
        t -= LANES
    return t


def _dot(a, b):
    return jnp.dot(a, b, preferred_element_type=F32)


def _dot_nt(a, b):
    return lax.dot_general(a, b, (((1,), (1,)), ((), ())), preferred_element_type=F32)


def _dot_tn(a, b):
    return lax.dot_general(a, b, (((0,), (0,)), ((), ())), preferred_element_type=F32)


def _silu(x):
    return x * jax.nn.sigmoid(x)


def _rms(x):
    return x * lax.rsqrt(jnp.mean(x * x, axis=-1, keepdims=True) + NORM_EPS)


def _ada_kernel(c_ref, w_ref, b_ref, o_ref):
    a = _silu(c_ref[...]).astype(BF16)
    o_ref[...] = _dot(a, w_ref[...].astype(BF16)) + b_ref[...]


def _ada(c_pad, w, b):
    rows, d = c_pad.shape
    n = w.shape[1]
    tn = _pick(n, 1024)
    return pl.pallas_call(
        _ada_kernel,
        grid=(n // tn,),
        in_specs=[pl.BlockSpec((rows, d), lambda j: (0, 0)),
                  pl.BlockSpec((d, tn), lambda j: (0, j)),
                  pl.BlockSpec((1, tn), lambda j: (0, j))],
        out_specs=pl.BlockSpec((rows, tn), lambda j: (0, j)),
        out_shape=jax.ShapeDtypeStruct((rows, n), F32),
        compiler_params=_params("arbitrary"),
        name="ada",
    )(c_pad, w, b)


def _prenorm_kernel(x_ref, g_ref, mod_ref, o_ref):
    y = _rms(x_ref[...]) * g_ref[...]
    shift = mod_ref[0, 0:1, :]
    scale = mod_ref[0, 1:2, :]
    o_ref[...] = (y * (1.0 + scale) + shift).astype(BF16)


def _prenorm(cfg, x2, g, mod3):
    tm = _pick(cfg.S, 1024)
    per_b = cfg.S // tm
    return pl.pallas_call(
        _prenorm_kernel,
        grid=(cfg.T // tm,),
        in_specs=[pl.BlockSpec((tm, cfg.D), lambda i: (i, 0)),
                  pl.BlockSpec((1, cfg.D), lambda i: (0, 0)),
                  pl.BlockSpec((1, 6, cfg.D), lambda i: (i // per_b, 0, 0))],
        out_specs=pl.BlockSpec((tm, cfg.D), lambda i: (i, 0)),
        out_shape=jax.ShapeDtypeStruct((cfg.T, cfg.D), BF16),
        compiler_params=_params("arbitrary"),
        name="prenorm",
    )(x2, g, mod3)


def _inproj_kernel(a_ref, w_ref, o_ref, wbf_ref):
    @pl.when(pl.program_id(1) == 0)
    def _():
        wbf_ref[...] = w_ref[...].astype(BF16)

    o_ref[...] = _dot(a_ref[...], wbf_ref[...]).astype(o_ref.dtype)


def _inproj(a, w):
    m, k = a.shape
    n = w.shape[1]
    tn = _pick(n, 1536)
    tm = _pick(m, 1024)
    return pl.pallas_call(
        _inproj_kernel,
        grid=(n // tn, m // tm),
        in_specs=[pl.BlockSpec((tm, k), lambda j, i: (i, 0)),
                  pl.BlockSpec((k, tn), lambda j, i: (0, j))],
        out_specs=pl.BlockSpec((tm, tn), lambda j, i: (i, j)),
        out_shape=jax.ShapeDtypeStruct((m, n), BF16),
        scratch_shapes=[pltpu.VMEM((k, tn), BF16)],
        compiler_params=_params("arbitrary", "arbitrary"),
        name="inproj",
    )(a, w)


def _rope_kernel(q_ref, k_ref, pos_ref, inv_ref, sgn_ref, qo_ref, ko_ref, *, q_scale):
    ang = pos_ref[...].astype(F32) * inv_ref[...]
    cos = jnp.cos(ang)
    sin = jnp.sin(ang) * sgn_ref[...]

    def rotate(x_ref, o_ref, scale):
        for h in range(x_ref.shape[1] // HEAD):
            cols = slice(h * HEAD, (h + 1) * HEAD)
            x = x_ref[:, cols].astype(F32)
            r = x * cos + pltpu.roll(x, HEAD // 2, 1) * sin
            o_ref[:, cols] = (r if scale is None else r * scale).astype(BF16)

    rotate(q_ref, qo_ref, q_scale)
    rotate(k_ref, ko_ref, None)


def _rope(cfg, proj, pos2, inv, sgn):
    offs, _ = cfg.offsets
    wq, wk = cfg.AH * HEAD, cfg.KVH * HEAD
    assert offs[5] % wq == 0 and offs[6] % wk == 0
    tm = _pick(cfg.T, 512)
    return pl.pallas_call(
        functools.partial(_rope_kernel, q_scale=HEAD ** -0.5),
        grid=(cfg.T // tm,),
        in_specs=[pl.BlockSpec((tm, wq), lambda i: (i, offs[5] // wq)),
                  pl.BlockSpec((tm, wk), lambda i: (i, offs[6] // wk)),
                  pl.BlockSpec((tm, 1), lambda i: (i, 0)),
                  pl.BlockSpec((1, HEAD), lambda i: (0, 0)),
                  pl.BlockSpec((1, HEAD), lambda i: (0, 0))],
        out_specs=[pl.BlockSpec((tm, wq), lambda i: (i, 0)),
                   pl.BlockSpec((tm, wk), lambda i: (i, 0))],
        out_shape=[jax.ShapeDtypeStruct((cfg.T, wq), BF16),
                   jax.ShapeDtypeStruct((cfg.T, wk), BF16)],
        compiler_params=_params("arbitrary"),
        name="rope",
    )(proj, proj, pos2, inv, sgn)


def _attn_kernel(sink_ref, q_ref, kp_ref, kc_ref, kn_ref, vp_ref, vc_ref, vn_ref, o_ref, *, G, KVH, NB):
    n = pl.program_id(1)
    blk = WINDOW
    row = lax.broadcasted_iota(jnp.int32, (G * blk, 3 * blk), 0)
    qi = row - (row // blk) * blk
    kj = lax.broadcasted_iota(jnp.int32, (G * blk, 3 * blk), 1)
    valid = jnp.abs(kj - blk - qi) <= WINDOW
    valid = valid & ((kj >= blk) | (n > 0)) & ((kj < 2 * blk) | (n < NB - 1))
    head = lax.broadcasted_iota(jnp.int32, (G * blk, 1), 0) // blk
    for kvh in range(KVH):
        kv = slice(kvh * HEAD, (kvh + 1) * HEAD)
        k = jnp.concatenate([kp_ref[:, kv], kc_ref[:, kv], kn_ref[:, kv]], axis=0)
        v = jnp.concatenate([vp_ref[:, kv], vc_ref[:, kv], vn_ref[:, kv]], axis=0).astype(BF16)
        heads = [kvh * G + g for g in range(G)]
        q = jnp.concatenate([q_ref[:, h * HEAD:(h + 1) * HEAD] for h in heads], axis=0)
        sk = jnp.zeros((G * blk, 1), F32)
        for g, h in enumerate(heads):
            sk = jnp.where(head == g, sink_ref[h], sk)
        s = jnp.where(valid, _dot_nt(q, k), -jnp.inf)
        m = jnp.maximum(jnp.max(s, axis=-1, keepdims=True), sk)
        p = jnp.exp(s - m)
        denom = jnp.sum(p, axis=-1, keepdims=True) + jnp.exp(sk - m)
        o = (_dot(p.astype(BF16), v) / denom).astype(BF16)
        for g, h in enumerate(heads):
            o_ref[:, h * HEAD:(h + 1) * HEAD] = o[g * blk:(g + 1) * blk]


def _attn(cfg, q_rot, k_rot, proj, sink):
    offs, _ = cfg.offsets
    G = cfg.AH // cfg.KVH
    blk = WINDOW
    NB = cfg.S // blk
    wq, wkv = cfg.AH * HEAD, cfg.KVH * HEAD
    assert offs[7] % wkv == 0
    vcol = offs[7] // wkv
    prev = lambda n: jnp.maximum(n - 1, 0)
    nxt = lambda n: jnp.minimum(n + 1, NB - 1)
    return pl.pallas_call(
        functools.partial(_attn_kernel, G=G, KVH=cfg.KVH, NB=NB),
        grid=(cfg.B, NB),
        in_specs=[pl.BlockSpec(memory_space=pltpu.SMEM),
                  pl.BlockSpec((blk, wq), lambda b, n: (b * NB + n, 0)),
                  pl.BlockSpec((blk, wkv), lambda b, n: (b * NB + prev(n), 0)),
                  pl.BlockSpec((blk, wkv), lambda b, n: (b * NB + n, 0)),
                  pl.BlockSpec((blk, wkv), lambda b, n: (b * NB + nxt(n), 0)),
                  pl.BlockSpec((blk, wkv), lambda b, n: (b * NB + prev(n), vcol)),
                  pl.BlockSpec((blk, wkv), lambda b, n: (b * NB + n, vcol)),
                  pl.BlockSpec((blk, wkv), lambda b, n: (b * NB + nxt(n), vcol))],
        out_specs=pl.BlockSpec((blk, wq), lambda b, n: (b * NB + n, 0)),
        out_shape=jax.ShapeDtypeStruct((cfg.T, wq), BF16),
        compiler_params=_params("arbitrary", "arbitrary"),
        name="attn",
    )(sink, q_rot, k_rot, k_rot, k_rot, proj, proj, proj)


def _lower_bound(lbl_ref, d, layer):
    n = lbl_ref.shape[1]
    rows = [lbl_ref[d, j:j + 1, :] for j in range(n)]
    m = functools.reduce(jnp.maximum, rows)
    es = [jnp.exp(r - m) for r in rows]
    return sum(es[:layer + 1]) / sum(es)


def _hgrn_cumsum(logfs, tri2):
    P = 2 * HG_CHUNK
    npair = logfs[0].shape[0] // P
    cols = []
    for lf in logfs:
        hi = lf.astype(BF16)
        lo = (lf - hi.astype(F32)).astype(BF16)
        for p in range(npair):
            cols += [hi[p * P:(p + 1) * P], lo[p * P:(p + 1) * P]]
    both = _dot(tri2.astype(BF16), jnp.concatenate(cols, axis=1))
    outs = []
    for s in range(len(logfs)):
        parts = []
        for p in range(npair):
            c0 = (s * npair + p) * 2 * HEAD
            parts.append(both[:, c0:c0 + HEAD] + both[:, c0 + HEAD:c0 + 2 * HEAD])
        outs.append(jnp.concatenate(parts, axis=0))
    return outs


def _hgrn_tile(qs, k, b, v, st, mask, fwd):
    L = HG_CHUNK
    nc = qs.shape[0] // L
    ref_i, last_i = (L // 2, L - 1) if fwd else (L - 1 - L // 2, 0)

    def per_chunk(rows):
        return jnp.concatenate([jnp.broadcast_to(r, (L, HEAD)) for r in rows], axis=0)

    brefs = [b[c * L + ref_i:c * L + ref_i + 1, :] for c in range(nc)]
    blasts = [b[c * L + last_i:c * L + last_i + 1, :] for c in range(nc)]
    d = b - per_chunk(brefs)
    q_in = qs * jnp.exp(d)
    k_in = k * jnp.exp(-d)
    q_bf, k_bf, v_bf = q_in.astype(BF16), k_in.astype(BF16), v.astype(BF16)
    nb = mask.shape[0]
    o_intra = []
    for r0 in range(0, qs.shape[0], nb):
        r = slice(r0, r0 + nb)
        a = jnp.where(mask, _dot_nt(q_bf[r], k_bf[r]), 0.0)
        o_intra.append(_dot(a.astype(BF16), v_bf[r]))
    o_intra = jnp.concatenate(o_intra, axis=0)
    q_dec = (q_in * per_chunk([jnp.exp(r) for r in brefs])).astype(BF16)
    k_dec = k_in * per_chunk([jnp.exp(l - r) for l, r in zip(blasts, brefs)])

    first = lax.broadcasted_iota(jnp.int32, (2 * L, HEAD), 0) < L
    cs = []
    for p in range(nc // 2):
        r = slice(2 * p * L, 2 * (p + 1) * L)
        kd = k_dec[r]
        zero = jnp.zeros_like(kd)
        k_blk = jnp.concatenate([jnp.where(first, kd, zero), jnp.where(first, zero, kd)], axis=1).astype(BF16)
        both = _dot_tn(v_bf[r], k_blk)
        cs += [both[:, :HEAD], both[:, HEAD:]]
    entering = [None] * nc
    for c in (range(nc) if fwd else reversed(range(nc))):
        entering[c] = st
        st = st * jnp.exp(blasts[c]) + cs[c]
    o_inter = []
    for p in range(nc // 2):
        r = slice(2 * p * L, 2 * (p + 1) * L)
        st2 = jnp.concatenate([entering[2 * p], entering[2 * p + 1]], axis=0).astype(BF16)
        both = _dot_nt(q_dec[r], st2)
        o_inter += [both[:L, :HEAD], both[L:, HEAD:]]
    return o_intra + jnp.concatenate(o_inter, axis=0), st


def _hgrn_kernel(lbl_ref, gain_ref, qf_ref, vf_ref, zf_ref, qb_ref, vb_ref, zb_ref, g_ref, o_ref,
                 st_ref, of_ref, ob_ref, *, NT, R, HP, layer):
    i = pl.program_id(2)

    def chunk_masks(n):
        li = lax.broadcasted_iota(jnp.int32, (n, n), 0)
        mi = lax.broadcasted_iota(jnp.int32, (n, n), 1)
        same = (li // HG_CHUNK) == (mi // HG_CHUNK)
        return same & (mi <= li), same & (mi >= li)

    tri_f, tri_b = (jnp.where(m, 1.0, 0.0).astype(F32) for m in chunk_masks(2 * HG_CHUNK))
    mask_f, mask_b = chunk_masks(min(R, HGRN_SCORE_ROWS))

    @pl.when(i == 0)
    def _():
        st_ref[...] = jnp.zeros_like(st_ref)

    lanes = [slice(p * HEAD, (p + 1) * HEAD) for p in range(HP)]
    for d, (q_ref, v_ref, z_ref) in enumerate(((qf_ref, vf_ref, zf_ref), (qb_ref, vb_ref, zb_ref))):
        fwd = d == 0
        lb = _lower_bound(lbl_ref, d, layer)
        f = lb + (1.0 - lb) * jax.nn.sigmoid(z_ref[...].astype(F32))
        k = 1.0 - f
        qs = _silu(q_ref[...].astype(F32))
        cums = _hgrn_cumsum([jnp.log(f)[:, lanes[p]] for p in range(HP)], tri_f if fwd else tri_b)
        tile = i if fwd else NT - 1 - i
        rows = pl.ds(pl.multiple_of(tile * R, R), R)
        for p in range(HP):
            o, st = _hgrn_tile(qs[:, lanes[p]], k[:, lanes[p]], cums[p], v_ref[:, lanes[p]], st_ref[p, d],
                               mask_f if fwd else mask_b, fwd)
            (of_ref if fwd else ob_ref)[rows, lanes[p]] = o
            st_ref[p, d] = st

    @pl.when(i == NT - 1)
    def _():
        for t in range(NT):
            rows = slice(t * R, (t + 1) * R)
            for p in range(HP):
                o = of_ref[rows, lanes[p]] + ob_ref[rows, lanes[p]]
                gate = _silu(g_ref[rows, lanes[p]].astype(F32))
                o_ref[rows, lanes[p]] = (_rms(o) * gain_ref[0, :, lanes[p]] * gate).astype(BF16)


def _hgrn(cfg, proj, lb_logits, gain3, layer):
    offs, _ = cfg.offsets
    HP = HGRN_HEADS_PER_STEP
    W = HP * HEAD
    assert cfg.HH % HP == 0 and all(o % W == 0 for o in offs[:5])
    qc, fc, bc, ic, gc = (o // W for o in offs[:5])
    R = _pick(cfg.S, 512)
    NT = cfg.S // R
    nl = lb_logits.shape[1]
    fwd = lambda col: pl.BlockSpec((R, W), lambda b, h, i: (b * NT + i, col + h))
    bwd = lambda col: pl.BlockSpec((R, W), lambda b, h, i: (b * NT + NT - 1 - i, col + h))
    return pl.pallas_call(
        functools.partial(_hgrn_kernel, NT=NT, R=R, HP=HP, layer=layer),
        grid=(cfg.B, cfg.HH // HP, NT),
        in_specs=[pl.BlockSpec((2, nl, W), lambda b, h, i: (0, 0, h)),
                  pl.BlockSpec((1, 1, W), lambda b, h, i: (h, 0, 0)),
                  fwd(qc), fwd(ic), fwd(fc), bwd(qc), bwd(ic), bwd(bc),
                  pl.BlockSpec((cfg.S, W), lambda b, h, i: (b, gc + h))],
        out_specs=pl.BlockSpec((cfg.S, W), lambda b, h, i: (b, h)),
        out_shape=jax.ShapeDtypeStruct((cfg.T, cfg.HH * HEAD), BF16),
        scratch_shapes=[pltpu.VMEM((HP, 2, HEAD, HEAD), F32),
                        pltpu.VMEM((cfg.S, W), F32), pltpu.VMEM((cfg.S, W), F32)],
        compiler_params=_params("arbitrary", "arbitrary", "arbitrary"),
        name="hgrn",
    )(lb_logits, gain3, proj, proj, proj, proj, proj, proj, proj)


def _merge_kernel(oa_ref, ob_ref, *refs, nsub):
    ga_refs, gb_refs = refs[:nsub], refs[nsub:2 * nsub]
    wa_ref, wb_ref, o_ref, wabf_ref, wbbf_ref = refs[2 * nsub:]

    @pl.when(pl.program_id(1) == 0)
    def _():
        wabf_ref[...] = wa_ref[...].astype(BF16)
        wbbf_ref[...] = wb_ref[...].astype(BF16)

    ya = _dot(oa_ref[...], wabf_ref[...])
    yb = _dot(ob_ref[...], wbbf_ref[...])
    gw = ya.shape[1] // nsub
    for u in range(nsub):
        cols = slice(u * gw, (u + 1) * gw)
        o_ref[:, cols] = (jax.nn.sigmoid(ga_refs[u][...].astype(F32)) * ya[:, cols]
                          + jax.nn.sigmoid(gb_refs[u][...].astype(F32)) * yb[:, cols]).astype(BF16)


def _merge(cfg, oa, ob, proj, wa, wb):
    offs, _ = cfg.offsets
    gw = _pick(cfg.D, 512)
    assert offs[8] % gw == 0 and offs[9] % gw == 0
    nsub = 2 if cfg.D % (2 * gw) == 0 else 1
    tn = nsub * gw
    ga0, gb0 = offs[8] // gw, offs[9] // gw
    tm = _pick(cfg.T, 1024)
    ka, kb = oa.shape[1], ob.shape[1]
    gate = lambda c0, u: pl.BlockSpec((tm, gw), lambda j, i: (i, c0 + nsub * j + u))
    return pl.pallas_call(
        functools.partial(_merge_kernel, nsub=nsub),
        grid=(cfg.D // tn, cfg.T // tm),
        in_specs=[pl.BlockSpec((tm, ka), lambda j, i: (i, 0)),
                  pl.BlockSpec((tm, kb), lambda j, i: (i, 0)),
                  *[gate(ga0, u) for u in range(nsub)],
                  *[gate(gb0, u) for u in range(nsub)],
                  pl.BlockSpec((ka, tn), lambda j, i: (0, j)),
                  pl.BlockSpec((kb, tn), lambda j, i: (0, j))],
        out_specs=pl.BlockSpec((tm, tn), lambda j, i: (i, j)),
        out_shape=jax.ShapeDtypeStruct((cfg.T, cfg.D), BF16),
        scratch_shapes=[pltpu.VMEM((ka, tn), BF16), pltpu.VMEM((kb, tn), BF16)],
        compiler_params=_params("arbitrary", "arbitrary"),
        name="merge",
    )(oa, ob, *([proj] * (2 * nsub)), wa, wb)


def _cast_kernel(w_ref, o_ref):
    o_ref[...] = w_ref[...].astype(BF16)


def _to_bf16(w):
    k, n = w.shape
    tk = _pick(k, 512)
    return pl.pallas_call(
        _cast_kernel,
        grid=(k // tk,),
        in_specs=[pl.BlockSpec((tk, n), lambda i: (i, 0))],
        out_specs=pl.BlockSpec((tk, n), lambda i: (i, 0)),
        out_shape=jax.ShapeDtypeStruct((k, n), BF16),
        compiler_params=_params("arbitrary"),
        name="wcast",
    )(w)


def _outproj_kernel(a_ref, w_ref, x_ref, mod_ref, gpost_ref, gpre_ref, wr_ref,
                    x1_ref, h_ref, aff_ref, y_ref, *, E, SUB):
    i = pl.program_id(0)

    @pl.when(i == 0)
    def _():
        y_ref[...] = jnp.zeros_like(y_ref)

    def fused(cur, prev):
        y_ref[cur] = _dot(a_ref[...], w_ref[...])
        wr = wr_ref[...]
        w_hi = wr.astype(BF16)
        w_pair = jnp.concatenate([w_hi, (wr - w_hi.astype(F32)).astype(BF16)], axis=1)
        post_gain = mod_ref[0, 2:3, :] * gpost_ref[...]
        pre_gain = gpre_ref[...] * (1.0 + mod_ref[0, 4:5, :])
        for u in range(a_ref.shape[0] // SUB):
            rows = slice(u * SUB, (u + 1) * SUB)
            x1 = x_ref[rows, :] + _rms(y_ref[prev, rows, :]) * post_gain
            x1_ref[rows, :] = x1
            h = _rms(x1) * pre_gain + mod_ref[0, 3:4, :]
            h_bf = h.astype(BF16)
            h_ref[rows, :] = h_bf
            both = _dot(h_bf, w_pair)
            logits = both[:, :LANES] + both[:, LANES:]
            lane = lax.broadcasted_iota(jnp.int32, logits.shape, 1)
            logits = jnp.where(lane < E, logits, -jnp.inf)
            p = jnp.exp(logits - jnp.max(logits, axis=-1, keepdims=True))
            aff_ref[rows, :] = p / jnp.sum(p, axis=-1, keepdims=True)

    @pl.when(i % 2 == 0)
    def _():
        fused(0, 1)

    @pl.when(i % 2 == 1)
    def _():
        fused(1, 0)


def _outproj(cfg, merged, w_bf, x2, mod3, gpost, gpre, wr_pad):
    tm = _pick(cfg.S, 512)
    sub = _pick(tm, 256)
    per_b = cfg.S // tm
    n = cfg.T // tm
    D = cfg.D
    const = lambda i: (0, 0)
    cur = lambda i: (jnp.minimum(i, n - 1), 0)
    lag = lambda i: (jnp.maximum(i - 1, 0), 0)
    return pl.pallas_call(
        functools.partial(_outproj_kernel, E=cfg.E, SUB=sub),
        grid=(n + 1,),
        in_specs=[pl.BlockSpec((tm, D), cur),
                  pl.BlockSpec((D, D), const, pipeline_mode=pl.Buffered(1)),
                  pl.BlockSpec((tm, D), lag),
                  pl.BlockSpec((1, 6, D), lambda i: (jnp.maximum(i - 1, 0) // per_b, 0, 0)),
                  pl.BlockSpec((1, D), const),
                  pl.BlockSpec((1, D), const),
                  pl.BlockSpec((D, LANES), const)],
        out_specs=[pl.BlockSpec((tm, D), lag),
                   pl.BlockSpec((tm, D), lag),
                   pl.BlockSpec((tm, LANES), lag)],
        out_shape=[jax.ShapeDtypeStruct((cfg.T, D), F32),
                   jax.ShapeDtypeStruct((cfg.T, D), BF16),
                   jax.ShapeDtypeStruct((cfg.T, LANES), F32)],
        scratch_shapes=[pltpu.VMEM((2, tm, D), F32)],
        compiler_params=_params("arbitrary"),
        name="outproj",
    )(merged, w_bf, x2, mod3, gpost, gpre, wr_pad)


def _route_kernel(aff_ref, posm_ref, postm_ref, offs_ref, a_ref, *, E, S, C):
    nt = S // LANES
    for t in range(nt):
        a_ref[:, t * LANES:(t + 1) * LANES] = aff_ref[t * LANES:(t + 1) * LANES, :].T[:E, :]

    cap = jnp.float32(C)

    def narrow(_, carry):
        lo, hi = carry
        width = hi - lo
        a = a_ref[...]
        new_lo, new_hi = lo, hi
        found = jnp.zeros(lo.shape, jnp.bool_)
        for j in range(SEARCH_PROBES, 0, -1):
            m = lo + (j / (SEARCH_PROBES + 1)) * width
            keep = jnp.sum(jnp.where(a >= m, 1.0, 0.0), axis=1, keepdims=True) >= cap
            new_lo = jnp.where(keep & ~found, m, new_lo)
            new_hi = jnp.where(keep | found, new_hi, m)
            found = found | keep
        return new_lo, new_hi

    lo, hi = lax.fori_loop(0, SEARCH_ROUNDS, narrow,
                           (jnp.zeros((E, 1), F32), jnp.full((E, 1), 2.0, F32)))
    need = cap - jnp.sum(jnp.where(a_ref[...] >= hi, 1.0, 0.0), axis=1, keepdims=True)

    ui = lax.broadcasted_iota(jnp.int32, (LANES, LANES), 0)
    uj = lax.broadcasted_iota(jnp.int32, (LANES, LANES), 1)
    upper = jnp.where(ui <= uj, 1.0, 0.0).astype(BF16)
    lane = lax.broadcasted_iota(jnp.int32, (E, LANES), 1)
    run_tie = jnp.zeros((E, 1), F32)
    run_sel = jnp.zeros((E, 1), F32)
    offs = jnp.zeros((E, LANES), F32)
    pad = jnp.full((LANES - E, LANES), -1.0, F32)
    for t in range(nt):
        cols = slice(t * LANES, (t + 1) * LANES)
        a = a_ref[:, cols]
        tie = (a >= lo) & (a < hi)
        tie_t = jnp.where(tie, 1.0, 0.0)
        tie_rank = _dot(tie_t.astype(BF16), upper) - tie_t + run_tie
        sel = (a >= hi) | (tie & (tie_rank < need))
        sel_t = jnp.where(sel, 1.0, 0.0)
        pos = _dot(sel_t.astype(BF16), upper) - sel_t + run_sel
        posm = jnp.where(sel, pos, -1.0)
        posm_ref[0, :, cols] = posm
        postm_ref[cols, :] = jnp.concatenate([posm, pad], axis=0).T
        offs = jnp.where(lane == t, run_sel, offs)
        run_tie = run_tie + jnp.sum(tie_t, axis=1, keepdims=True)
        run_sel = run_sel + jnp.sum(sel_t, axis=1, keepdims=True)
    offs = jnp.where(lane == nt, run_sel, offs)
    offs_ref[0] = offs.astype(jnp.int32)


def _route(cfg, aff_tm):
    E, S = cfg.E, cfg.S
    assert S // LANES < LANES and E % 8 == 0
    return pl.pallas_call(
        functools.partial(_route_kernel, E=E, S=S, C=cfg.C),
        grid=(cfg.B,),
        in_specs=[pl.BlockSpec((S, LANES), lambda b: (b, 0))],
        out_specs=[pl.BlockSpec((1, E, S), lambda b: (b, 0, 0)),
                   pl.BlockSpec((S, LANES), lambda b: (b, 0)),
                   pl.BlockSpec((1, E, LANES), lambda b: (b, 0, 0))],
        out_shape=[jax.ShapeDtypeStruct((cfg.B, E, S), F32),
                   jax.ShapeDtypeStruct((cfg.T, LANES), F32),
                   jax.ShapeDtypeStruct((cfg.B, E, LANES), jnp.int32)],
        scratch_shapes=[pltpu.VMEM((E, S), F32)],
        compiler_params=_params("arbitrary"),
        name="route",
    )(aff_tm)


def _slot_window(offs_ref, row, token_block, C):
    base = row * LANES + token_block * (COMBINE_TOKENS // LANES)
    first, end = offs_ref[base], offs_ref[base + COMBINE_TOKENS // LANES]
    start = jnp.minimum((first // BF16_ROWS) * BF16_ROWS, C - COMBINE_WINDOW)
    return pl.multiple_of(start, BF16_ROWS), end


def _extra_windows(start, end):
    return (jnp.maximum(end - start - COMBINE_WINDOW, 0) + COMBINE_WINDOW - 1) // COMBINE_WINDOW


def _gather_kernel(offs_ref, posm_ref, h_ref, x_ref, *, E, G, C):
    b, grp, kb = pl.program_id(0), pl.program_id(1), pl.program_id(2)
    W, TB = COMBINE_WINDOW, COMBINE_TOKENS

    @pl.when(kb == 0)
    def _():
        x_ref[...] = jnp.zeros_like(x_ref)

    slot0 = lax.broadcasted_iota(jnp.int32, (W, TB), 0)
    starts, pieces = [], []
    for j in range(G):
        start, _ = _slot_window(offs_ref, b * E + grp * G + j, kb, C)
        starts.append(start)
        pieces.append(jnp.where(posm_ref[j] == (start + slot0).astype(F32), 1.0, 0.0).astype(BF16))
    rows = _dot(jnp.concatenate(pieces, axis=0), h_ref[...]).astype(BF16)
    for j in range(G):
        win = pl.ds(starts[j], W)
        x_ref[j, win, :] = x_ref[j, win, :] + rows[j * W:(j + 1) * W]

    def spill(j, carry):
        start, end = _slot_window(offs_ref, b * E + grp * G + j, kb, C)

        def extra(w, c2):
            cur = start + (w + 1) * W
            s2 = pl.multiple_of(jnp.minimum(cur, C - W), BF16_ROWS)
            slot = s2 + slot0
            onehot = jnp.where((posm_ref[j] == slot.astype(F32)) & (slot >= cur), 1.0, 0.0).astype(BF16)
            win = pl.ds(s2, W)
            x_ref[j, win, :] = x_ref[j, win, :] + _dot(onehot, h_ref[...]).astype(BF16)
            return c2

        return lax.fori_loop(0, _extra_windows(start, end), extra, carry)

    lax.fori_loop(0, G, spill, 0)


def _gather(cfg, offs_flat, posm3, h2):
    B, E, S, C, D = cfg.B, cfg.E, cfg.S, cfg.C, cfg.D
    G, TB = min(E, GATHER_EXPERTS), COMBINE_TOKENS
    assert E % G == 0 and S % TB == 0 and C % COMBINE_WINDOW == 0
    nk = S // TB
    grid_spec = pltpu.PrefetchScalarGridSpec(
        num_scalar_prefetch=1,
        grid=(B, E // G, nk),
        in_specs=[pl.BlockSpec((G, 1, TB), lambda b, g, k, offs: (b * (E // G) + g, 0, k)),
                  pl.BlockSpec((TB, D), lambda b, g, k, offs: (b * nk + k, 0))],
        out_specs=pl.BlockSpec((G, C, D), lambda b, g, k, offs: (g, b, 0)),
    )
    return pl.pallas_call(
        functools.partial(_gather_kernel, E=E, G=G, C=C),
        grid_spec=grid_spec,
        out_shape=jax.ShapeDtypeStruct((E, B * C, D), BF16),
        compiler_params=_params("arbitrary", "arbitrary", "arbitrary"),
        name="gather",
    )(offs_flat, posm3, h2)


def _ffn_kernel(x_ref, wg_ref, wu_ref, wd_ref, y_ref, acc_ref):
    f = pl.program_id(1)

    @pl.when(f == 0)
    def _():
        acc_ref[...] = jnp.zeros_like(acc_ref)

    x = x_ref[0]
    a = (_silu(_dot(x, wg_ref[0].astype(BF16))) * _dot(x, wu_ref[0].astype(BF16))).astype(BF16)
    acc_ref[...] += _dot(a, wd_ref[0].astype(BF16))

    @pl.when(f == pl.num_programs(1) - 1)
    def _():
        y_ref[0] = acc_ref[...].astype(BF16)


def _ffn(cfg, xe, wg, wu, wd):
    B, E, C, D, FF = cfg.B, cfg.E, cfg.C, cfg.D, cfg.FF
    tf = _pick(FF, 256)
    return pl.pallas_call(
        _ffn_kernel,
        grid=(E, FF // tf),
        in_specs=[pl.BlockSpec((1, B * C, D), lambda e, f: (e, 0, 0)),
                  pl.BlockSpec((1, D, tf), lambda e, f: (e, 0, f)),
                  pl.BlockSpec((1, D, tf), lambda e, f: (e, 0, f)),
                  pl.BlockSpec((1, tf, D), lambda e, f: (e, f, 0))],
        out_specs=pl.BlockSpec((1, B * C, D), lambda e, f: (e, 0, 0)),
        out_shape=jax.ShapeDtypeStruct((E, B * C, D), BF16),
        scratch_shapes=[pltpu.VMEM((B * C, D), F32)],
        compiler_params=_params("arbitrary", "arbitrary"),
        name="ffn",
    )(xe, wg, wu, wd)


def _combine_kernel(offs_ref, pos_ref, aff_ref, ppos_ref, paff_ref, y_ref, x1_ref, mod_ref, g_ref, o_ref,
                    z_ref, acc_ref, *, E, C):
    b, k = pl.program_id(0), pl.program_id(1)
    W, TB = COMBINE_WINDOW, COMBINE_TOKENS
    lane = lax.broadcasted_iota(jnp.int32, (TB, LANES), 1)

    @pl.when(k == 0)
    def _():
        acc_ref[...] = jnp.zeros_like(acc_ref)

    def spill(e, carry):
        start, end = _slot_window(offs_ref, b * E + e, k - 1, C)

        def extra(w, c2):
            mine = lane == e
            pos = jnp.sum(jnp.where(mine, ppos_ref[...], 0.0), axis=1, keepdims=True)
            aff = jnp.sum(jnp.where(mine, paff_ref[...], 0.0), axis=1, keepdims=True)
            cur = start + (w + 1) * W
            s2 = pl.multiple_of(jnp.minimum(cur, C - W), BF16_ROWS)
            slot = s2 + lax.broadcasted_iota(jnp.int32, (TB, W), 1)
            m = jnp.where((pos == slot.astype(F32)) & (slot >= cur), aff, 0.0).astype(BF16)
            acc_ref[(k + 1) % 2] += _dot(m, y_ref[e, pl.ds(s2, W), :])
            return c2

        return lax.fori_loop(0, _extra_windows(start, end), extra, carry)

    @pl.when(k > 0)
    def _():
        lax.fori_loop(0, E, spill, 0)

    def fused(cur, prev):
        o_ref[...] = x1_ref[...] + mod_ref[0, 5:6, :] * (_rms(acc_ref[prev]) * g_ref[...])

        starts = []
        for e in range(E):
            start, _ = _slot_window(offs_ref, b * E + e, k, C)
            z_ref[e * W:(e + 1) * W, :] = y_ref[e, pl.ds(start, W), :]
            starts.append(start)
        seg = lane // W
        off = lane - seg * W
        group = LANES // W
        pieces = []
        for p in range(E // group):
            slot = jnp.zeros((TB, LANES), jnp.int32)
            pos = jnp.full((TB, LANES), -1.0, F32)
            aff = jnp.zeros((TB, LANES), F32)
            for q in range(group):
                e = p * group + q
                mine = seg == q
                slot = jnp.where(mine, starts[e] + off, slot)
                pos = jnp.where(mine, pos_ref[:, e:e + 1], pos)
                aff = jnp.where(mine, aff_ref[:, e:e + 1], aff)
            pieces.append(jnp.where(pos == slot.astype(F32), aff, 0.0).astype(BF16))
        acc_ref[cur] = _dot(jnp.concatenate(pieces, axis=1), z_ref[...])

    @pl.when(k % 2 == 0)
    def _():
        fused(0, 1)

    @pl.when(k % 2 == 1)
    def _():
        fused(1, 0)


def _combine(cfg, offs_flat, pos_tm, aff_tm, y, x1, mod3, g):
    B, E, S, C, D = cfg.B, cfg.E, cfg.S, cfg.C, cfg.D
    W, TB = COMBINE_WINDOW, COMBINE_TOKENS
    assert LANES % W == 0 and E % (LANES // W) == 0 and C % W == 0 and W % BF16_ROWS == 0 and S % TB == 0
    nk = S // TB
    assert (nk + 2) * (TB // LANES) <= LANES
    cur = lambda b, k, offs: (b * nk + jnp.minimum(k, nk - 1), 0)
    lag = lambda b, k, offs: (b * nk + jnp.maximum(k - 1, 0), 0)
    grid_spec = pltpu.PrefetchScalarGridSpec(
        num_scalar_prefetch=1,
        grid=(B, nk + 1),
        in_specs=[pl.BlockSpec((TB, LANES), cur),
                  pl.BlockSpec((TB, LANES), cur),
                  pl.BlockSpec((TB, LANES), lag),
                  pl.BlockSpec((TB, LANES), lag),
                  pl.BlockSpec((E, C, D), lambda b, k, offs: (0, b, 0), pipeline_mode=pl.Buffered(1)),
                  pl.BlockSpec((TB, D), lag),
                  pl.BlockSpec((1, 6, D), lambda b, k, offs: (b, 0, 0)),
                  pl.BlockSpec((1, D), lambda b, k, offs: (0, 0))],
        out_specs=pl.BlockSpec((TB, D), lag),
        scratch_shapes=[pltpu.VMEM((E * W, D), BF16), pltpu.VMEM((2, TB, D), F32)],
    )
    return pl.pallas_call(
        functools.partial(_combine_kernel, E=E, C=C),
        grid_spec=grid_spec,
        out_shape=jax.ShapeDtypeStruct((cfg.T, D), F32),
        compiler_params=_params("arbitrary", "arbitrary"),
        name="combine",
    )(offs_flat, pos_tm, aff_tm, pos_tm, aff_tm, y, x1, mod3, g)


def _layer(cfg, layer, x2, c_pad, pos2, inv, sgn, w_ada, b_ada, g_pre_mix, g_post_mix, g_pre_ffn,
           g_post_ffn, w_in, hg_lb_logits, hg_out_norm, attn_sink, w_branch_a, w_branch_b, w_out,
           w_router, w_exp_gate, w_exp_up, w_exp_down):
    D, E = cfg.D, cfg.E
    row = lambda v: v.reshape(1, -1)
    mod = _ada(c_pad, w_ada[layer], row(b_ada[layer]))[:cfg.B]
    mod3 = mod.reshape(cfg.B, 6, D)
    h = _prenorm(cfg, x2, row(g_pre_mix[layer]), mod3)
    proj = _inproj(h, w_in[layer])
    q_rot, k_rot = _rope(cfg, proj, pos2, inv, sgn)
    att = _attn(cfg, q_rot, k_rot, proj, attn_sink[layer])
    gain3 = hg_out_norm[layer].reshape(cfg.HH // HGRN_HEADS_PER_STEP, 1, HGRN_HEADS_PER_STEP * HEAD)
    oa = _hgrn(cfg, proj, hg_lb_logits, gain3, layer)
    merged = _merge(cfg, oa, att, proj, w_branch_a[layer], w_branch_b[layer])
    wr_pad = jnp.pad(w_router[layer], ((0, 0), (0, LANES - E)))
    x1, h2, aff_tm = _outproj(cfg, merged, _to_bf16(w_out[layer]), x2, mod3, row(g_post_mix[layer]),
                              row(g_pre_ffn[layer]), wr_pad)
    posm, pos_tm, offs = _route(cfg, aff_tm)
    offs_flat = offs.reshape(-1)
    xe = _gather(cfg, offs_flat, posm.reshape(cfg.B * E, 1, cfg.S), h2)
    ye = _ffn(cfg, xe, w_exp_gate[layer], w_exp_up[layer], w_exp_down[layer])
    return _combine(cfg, offs_flat, pos_tm, aff_tm, ye, x1, mod3, row(g_post_ffn[layer]))


def _block(cfg, x, c, positions, w_ada, *rest):
    depth = w_ada.shape[0]
    x2 = x.reshape(cfg.T, cfg.D)
    c_pad = jnp.pad(c, ((0, 8 - cfg.B), (0, 0)))
    pos2 = positions.reshape(cfg.T, 1)
    half = HEAD // 2
    inv_half = ROPE_THETA ** (-jnp.arange(half, dtype=F32) / half)
    inv = jnp.concatenate([inv_half, inv_half]).reshape(1, HEAD)
    sgn = jnp.concatenate([-jnp.ones((half,), F32), jnp.ones((half,), F32)]).reshape(1, HEAD)
    for layer in range(depth):
        x2 = _layer(cfg, layer, x2, c_pad, pos2, inv, sgn, w_ada, *rest)
    return x2.reshape(cfg.B, cfg.S, cfg.D)


def kernel(x, c, positions, w_ada, b_ada, g_pre_mix, g_post_mix, g_pre_ffn, g_post_ffn, w_in,
           hg_lb_logits, hg_out_norm, attn_sink, w_branch_a, w_branch_b, w_out, w_router,
           w_exp_gate, w_exp_up, w_exp_down):
    B, S, D = x.shape
    cfg = Cfg(B=B, S=S, D=D,
              HH=hg_out_norm.shape[1], AH=attn_sink.shape[1],
              KVH=(w_in.shape[2] - 5 * hg_out_norm.shape[1] * HEAD - attn_sink.shape[1] * HEAD - 2 * D) // (2 * HEAD),
              E=w_router.shape[2], FF=w_exp_gate.shape[3])
    return _block(cfg, x, c, positions, w_ada, b_ada, g_pre_mix, g_post_mix, g_pre_ffn, g_post_ffn,
                  w_in, hg_lb_logits, hg_out_norm, attn_sink, w_branch_a, w_branch_b, w_out,
                  w_router, w_exp_gate, w_exp_up, w_exp_down)
```

```python
import functools
from typing import NamedTuple

import jax
import jax.numpy as jnp
from jax import lax
from jax.experimental import pallas as pl
from jax.experimental.pallas import tpu as pltpu

F32 = jnp.float32
BF16 = jnp.bfloat16

LANES = 128
HEAD = 128
HG_CHUNK = 64
WINDOW = 128
ROPE_THETA = 10000.0
NORM_EPS = 1e-6
CAPACITY_FACTOR = 2
VMEM_LIMIT_BYTES = 56 * 1024 * 1024
SEARCH_PROBES = 3
SEARCH_ROUNDS = 76
HGRN_HEADS_PER_STEP = 2
HGRN_SCORE_ROWS = 256
COMBINE_TOKENS = 256
COMBINE_WINDOW = 64
GATHER_EXPERTS = 8
BF16_ROWS = 16


class Cfg(NamedTuple):
    B: int
    S: int
    D: int
    HH: int
    AH: int
    KVH: int
    E: int
    FF: int

    @property
    def T(self):
        return self.B * self.S

    @property
    def C(self):
        return CAPACITY_FACTOR * self.S // self.E

    @property
    def offsets(self):
        hq = self.HH * HEAD
        widths = (hq, hq, hq, hq, hq, self.AH * HEAD, self.KVH * HEAD, self.KVH * HEAD, self.D, self.D)
        offs, o = [], 0
        for w in widths:
            offs.append(o)
            o += w
        return tuple(offs), o


def _params(*sem):
    return pltpu.CompilerParams(dimension_semantics=sem, vmem_limit_bytes=VMEM_LIMIT_BYTES)


def _pick(n, target):
    if n <= target:
        return n
    t = (target // LANES) * LANES
    while t > LANES and n % t:
        t -= LANES
    return t


def _dot(a, b):
    return jnp.dot(a, b, preferred_element_type=F32)


def _dot_nt(a, b):
    return lax.dot_general(a, b, (((1,), (1,)), ((), ())), preferred_element_type=F32)


def _dot_tn(a, b):
    return lax.dot_general(a, b, (((0,), (0,)), ((), ())), preferred_element_type=F32)


def _silu(x):
    return x * jax.nn.sigmoid(x)


def _rms(x):
    return x * lax.rsqrt(jnp.mean(x * x, axis=-1, keepdims=True) + NORM_EPS)


def _ada_kernel(c_ref, w_ref, b_ref, o_ref):
    a = _silu(c_ref[...]).astype(BF16)
    o_ref[...] = _dot(a, w_ref[...].astype(BF16)) + b_ref[...]


def _ada(c_pad, w, b):
    rows, d = c_pad.shape
    n = w.shape[1]
    tn = _pick(n, 1024)
    return pl.pallas_call(
        _ada_kernel,
        grid=(n // tn,),
        in_specs=[pl.BlockSpec((rows, d), lambda j: (0, 0)),
                  pl.BlockSpec((d, tn), lambda j: (0, j)),
                  pl.BlockSpec((1, tn), lambda j: (0, j))],
        out_specs=pl.BlockSpec((rows, tn), lambda j: (0, j)),
        out_shape=jax.ShapeDtypeStruct((rows, n), F32),
        compiler_params=_params("arbitrary"),
        name="ada",
    )(c_pad, w, b)


def _prenorm_kernel(x_ref, g_ref, mod_ref, o_ref):
    y = _rms(x_ref[...]) * g_ref[...]
    shift = mod_ref[0, 0:1, :]
    scale = mod_ref[0, 1:2, :]
    o_ref[...] = (y * (1.0 + scale) + shift).astype(BF16)


def _prenorm(cfg, x2, g, mod3):
    tm = _pick(cfg.S, 1024)
    per_b = cfg.S // tm
    return pl.pallas_call(
        _prenorm_kernel,
        grid=(cfg.T // tm,),
        in_specs=[pl.BlockSpec((tm, cfg.D), lambda i: (i, 0)),
                  pl.BlockSpec((1, cfg.D), lambda i: (0, 0)),
                  pl.BlockSpec((1, 6, cfg.D), lambda i: (i // per_b, 0, 0))],
        out_specs=pl.BlockSpec((tm, cfg.D), lambda i: (i, 0)),
        out_shape=jax.ShapeDtypeStruct((cfg.T, cfg.D), BF16),
        compiler_params=_params("arbitrary"),
        name="prenorm",
    )(x2, g, mod3)


def _inproj_kernel(a_ref, w_ref, o_ref, wbf_ref):
    @pl.when(pl.program_id(1) == 0)
    def _():
        wbf_ref[...] = w_ref[...].astype(BF16)

    o_ref[...] = _dot(a_ref[...], wbf_ref[...]).astype(o_ref.dtype)


def _inproj(a, w):
    m, k = a.shape
    n = w.shape[1]
    tn = _pick(n, 1536)
    tm = _pick(m, 1024)
    return pl.pallas_call(
        _inproj_kernel,
        grid=(n // tn, m // tm),
        in_specs=[pl.BlockSpec((tm, k), lambda j, i: (i, 0)),
                  pl.BlockSpec((k, tn), lambda j, i: (0, j))],
        out_specs=pl.BlockSpec((tm, tn), lambda j, i: (i, j)),
        out_shape=jax.ShapeDtypeStruct((m, n), BF16),
        scratch_shapes=[pltpu.VMEM((k, tn), BF16)],
        compiler_params=_params("arbitrary", "arbitrary"),
        name="inproj",
    )(a, w)


def _rope_kernel(q_ref, k_ref, pos_ref, inv_ref, sgn_ref, qo_ref, ko_ref, *, q_scale):
    ang = pos_ref[...].astype(F32) * inv_ref[...]
    cos = jnp.cos(ang)
    sin = jnp.sin(ang) * sgn_ref[...]

    def rotate(x_ref, o_ref, scale):
        for h in range(x_ref.shape[1] // HEAD):
            cols = slice(h * HEAD, (h + 1) * HEAD)
            x = x_ref[:, cols].astype(F32)
            r = x * cos + pltpu.roll(x, HEAD // 2, 1) * sin
            o_ref[:, cols] = (r if scale is None else r * scale).astype(BF16)

    rotate(q_ref, qo_ref, q_scale)
    rotate(k_ref, ko_ref, None)


def _rope(cfg, proj, pos2, inv, sgn):
    offs, _ = cfg.offsets
    wq, wk = cfg.AH * HEAD, cfg.KVH * HEAD
    assert offs[5] % wq == 0 and offs[6] % wk == 0
    tm = _pick(cfg.T, 512)
    return pl.pallas_call(
        functools.partial(_rope_kernel, q_scale=HEAD ** -0.5),
        grid=(cfg.T // tm,),
        in_specs=[pl.BlockSpec((tm, wq), lambda i: (i, offs[5] // wq)),
                  pl.BlockSpec((tm, wk), lambda i: (i, offs[6] // wk)),
                  pl.BlockSpec((tm, 1), lambda i: (i, 0)),
                  pl.BlockSpec((1, HEAD), lambda i: (0, 0)),
                  pl.BlockSpec((1, HEAD), lambda i: (0, 0))],
        out_specs=[pl.BlockSpec((tm, wq), lambda i: (i, 0)),
                   pl.BlockSpec((tm, wk), lambda i: (i, 0))],
        out_shape=[jax.ShapeDtypeStruct((cfg.T, wq), BF16),
                   jax.ShapeDtypeStruct((cfg.T, wk), BF16)],
        compiler_params=_params("arbitrary"),
        name="rope",
    )(proj, proj, pos2, inv, sgn)


def _attn_kernel(sink_ref, q_ref, kp_ref, kc_ref, kn_ref, vp_ref, vc_ref, vn_ref, o_ref, *, G, KVH, NB):
    n = pl.program_id(1)
    blk = WINDOW
    row = lax.broadcasted_iota(jnp.int32, (G * blk, 3 * blk), 0)
    qi = row - (row // blk) * blk
    kj = lax.broadcasted_iota(jnp.int32, (G * blk, 3 * blk), 1)
    valid = jnp.abs(kj - blk - qi) <= WINDOW
    valid = valid & ((kj >= blk) | (n > 0)) & ((kj < 2 * blk) | (n < NB - 1))
    head = lax.broadcasted_iota(jnp.int32, (G * blk, 1), 0) // blk
    for kvh in range(KVH):
        kv = slice(kvh * HEAD, (kvh + 1) * HEAD)
        k = jnp.concatenate([kp_ref[:, kv], kc_ref[:, kv], kn_ref[:, kv]], axis=0)
        v = jnp.concatenate([vp_ref[:, kv], vc_ref[:, kv], vn_ref[:, kv]], axis=0).astype(BF16)
        heads = [kvh * G + g for g in range(G)]
        q = jnp.concatenate([q_ref[:, h * HEAD:(h + 1) * HEAD] for h in heads], axis=0)
        sk = jnp.zeros((G * blk, 1), F32)
        for g, h in enumerate(heads):
            sk = jnp.where(head == g, sink_ref[h], sk)
        s = jnp.where(valid, _dot_nt(q, k), -jnp.inf)
        m = jnp.maximum(jnp.max(s, axis=-1, keepdims=True), sk)
        p = jnp.exp(s - m)
        denom = jnp.sum(p, axis=-1, keepdims=True) + jnp.exp(sk - m)
        o = (_dot(p.astype(BF16), v) / denom).astype(BF16)
        for g, h in enumerate(heads):
            o_ref[:, h * HEAD:(h + 1) * HEAD] = o[g * blk:(g + 1) * blk]


def _attn(cfg, q_rot, k_rot, proj, sink):
    offs, _ = cfg.offsets
    G = cfg.AH // cfg.KVH
    blk = WINDOW
    NB = cfg.S // blk
    wq, wkv = cfg.AH * HEAD, cfg.KVH * HEAD
    assert offs[7] % wkv == 0
    vcol = offs[7] // wkv
    prev = lambda n: jnp.maximum(n - 1, 0)
    nxt = lambda n: jnp.minimum(n + 1, NB - 1)
    return pl.pallas_call(
        functools.partial(_attn_kernel, G=G, KVH=cfg.KVH, NB=NB),
        grid=(cfg.B, NB),
        in_specs=[pl.BlockSpec(memory_space=pltpu.SMEM),
                  pl.BlockSpec((blk, wq), lambda b, n: (b * NB + n, 0)),
                  pl.BlockSpec((blk, wkv), lambda b, n: (b * NB + prev(n), 0)),
                  pl.BlockSpec((blk, wkv), lambda b, n: (b * NB + n, 0)),
                  pl.BlockSpec((blk, wkv), lambda b, n: (b * NB + nxt(n), 0)),
                  pl.BlockSpec((blk, wkv), lambda b, n: (b * NB + prev(n), vcol)),
                  pl.BlockSpec((blk, wkv), lambda b, n: (b * NB + n, vcol)),
                  pl.BlockSpec((blk, wkv), lambda b, n: (b * NB + nxt(n), vcol))],
        out_specs=pl.BlockSpec((blk, wq), lambda b, n: (b * NB + n, 0)),
        out_shape=jax.ShapeDtypeStruct((cfg.T, wq), BF16),
        compiler_params=_params("arbitrary", "arbitrary"),
        name="attn",
    )(sink, q_rot, k_rot, k_rot, k_rot, proj, proj, proj)


def _lower_bound(lbl_ref, d, layer):
    n = lbl_ref.shape[1]
    rows = [lbl_ref[d, j:j + 1, :] for j in range(n)]
    m = functools.reduce(jnp.maximum, rows)
    es = [jnp.exp(r - m) for r in rows]
    return sum(es[:layer + 1]) / sum(es)


def _hgrn_cumsum(logfs, tri2):
    P = 2 * HG_CHUNK
    npair = logfs[0].shape[0] // P
    cols = []
    for lf in logfs:
        hi = lf.astype(BF16)
        lo = (lf - hi.astype(F32)).astype(BF16)
        for p in range(npair):
            cols += [hi[p * P:(p + 1) * P], lo[p * P:(p + 1) * P]]
    both = _dot(tri2.astype(BF16), jnp.concatenate(cols, axis=1))
    outs = []
    for s in range(len(logfs)):
        parts = []
        for p in range(npair):
            c0 = (s * npair + p) * 2 * HEAD
            parts.append(both[:, c0:c0 + HEAD] + both[:, c0 + HEAD:c0 + 2 * HEAD])
        outs.append(jnp.concatenate(parts, axis=0))
    return outs


def _hgrn_tile(qs, k, b, v, st, mask, fwd):
    L = HG_CHUNK
    nc = qs.shape[0] // L
    ref_i, last_i = (L // 2, L - 1) if fwd else (L - 1 - L // 2, 0)

    def per_chunk(rows):
        return jnp.concatenate([jnp.broadcast_to(r, (L, HEAD)) for r in rows], axis=0)

    brefs = [b[c * L + ref_i:c * L + ref_i + 1, :] for c in range(nc)]
    blasts = [b[c * L + last_i:c * L + last_i + 1, :] for c in range(nc)]
    d = b - per_chunk(brefs)
    q_in = qs * jnp.exp(d)
    k_in = k * jnp.exp(-d)
    q_bf, k_bf, v_bf = q_in.astype(BF16), k_in.astype(BF16), v.astype(BF16)
    nb = mask.shape[0]
    o_intra = []
    for r0 in range(0, qs.shape[0], nb):
        r = slice(r0, r0 + nb)
        a = jnp.where(mask, _dot_nt(q_bf[r], k_bf[r]), 0.0)
        o_intra.append(_dot(a.astype(BF16), v_bf[r]))
    o_intra = jnp.concatenate(o_intra, axis=0)
    q_dec = (q_in * per_chunk([jnp.exp(r) for r in brefs])).astype(BF16)
    k_dec = k_in * per_chunk([jnp.exp(l - r) for l, r in zip(blasts, brefs)])

    zero = jnp.zeros((L, HEAD), F32)
    cs = []
    for p in range(nc // 2):
        r = slice(2 * p * L, 2 * (p + 1) * L)
        kd = k_dec[r]
        k_blk = jnp.concatenate([jnp.concatenate([kd[:L], zero], axis=1),
                                 jnp.concatenate([zero, kd[L:]], axis=1)], axis=0).astype(BF16)
        both = _dot_tn(v_bf[r], k_blk)
        cs += [both[:, :HEAD], both[:, HEAD:]]
    entering = [None] * nc
    for c in (range(nc) if fwd else reversed(range(nc))):
        entering[c] = st
        st = st * jnp.exp(blasts[c]) + cs[c]
    o_inter = []
    for p in range(nc // 2):
        r = slice(2 * p * L, 2 * (p + 1) * L)
        st2 = jnp.concatenate([entering[2 * p], entering[2 * p + 1]], axis=0).astype(BF16)
        both = _dot_nt(q_dec[r], st2)
        o_inter += [both[:L, :HEAD], both[L:, HEAD:]]
    return o_intra + jnp.concatenate(o_inter, axis=0), st


def _hgrn_kernel(lbl_ref, gain_ref, qf_ref, vf_ref, zf_ref, qb_ref, vb_ref, zb_ref, g_ref, o_ref,
                 st_ref, of_ref, ob_ref, *, NT, R, HP, layer):
    i = pl.program_id(2)

    def chunk_masks(n):
        li = lax.broadcasted_iota(jnp.int32, (n, n), 0)
        mi = lax.broadcasted_iota(jnp.int32, (n, n), 1)
        same = (li // HG_CHUNK) == (mi // HG_CHUNK)
        return same & (mi <= li), same & (mi >= li)

    tri_f, tri_b = (jnp.where(m, 1.0, 0.0).astype(F32) for m in chunk_masks(2 * HG_CHUNK))
    mask_f, mask_b = chunk_masks(min(R, HGRN_SCORE_ROWS))

    @pl.when(i == 0)
    def _():
        st_ref[...] = jnp.zeros_like(st_ref)

    lanes = [slice(p * HEAD, (p + 1) * HEAD) for p in range(HP)]
    for d, (q_ref, v_ref, z_ref) in enumerate(((qf_ref, vf_ref, zf_ref), (qb_ref, vb_ref, zb_ref))):
        fwd = d == 0
        lb = _lower_bound(lbl_ref, d, layer)
        f = lb + (1.0 - lb) * jax.nn.sigmoid(z_ref[...].astype(F32))
        k = 1.0 - f
        qs = _silu(q_ref[...].astype(F32))
        cums = _hgrn_cumsum([jnp.log(f)[:, lanes[p]] for p in range(HP)], tri_f if fwd else tri_b)
        tile = i if fwd else NT - 1 - i
        rows = pl.ds(pl.multiple_of(tile * R, R), R)
        for p in range(HP):
            o, st = _hgrn_tile(qs[:, lanes[p]], k[:, lanes[p]], cums[p], v_ref[:, lanes[p]], st_ref[p, d],
                               mask_f if fwd else mask_b, fwd)
            (of_ref if fwd else ob_ref)[rows, lanes[p]] = o
            st_ref[p, d] = st

    @pl.when(i == NT - 1)
    def _():
        for t in range(NT):
            rows = slice(t * R, (t + 1) * R)
            for p in range(HP):
                o = of_ref[rows, lanes[p]] + ob_ref[rows, lanes[p]]
                gate = _silu(g_ref[rows, lanes[p]].astype(F32))
                o_ref[rows, lanes[p]] = (_rms(o) * gain_ref[0, :, lanes[p]] * gate).astype(BF16)


def _hgrn(cfg, proj, lb_logits, gain3, layer):
    offs, _ = cfg.offsets
    HP = HGRN_HEADS_PER_STEP
    W = HP * HEAD
    assert cfg.HH % HP == 0 and all(o % W == 0 for o in offs[:5])
    qc, fc, bc, ic, gc = (o // W for o in offs[:5])
    R = _pick(cfg.S, 512)
    NT = cfg.S // R
    nl = lb_logits.shape[1]
    fwd = lambda col: pl.BlockSpec((R, W), lambda b, h, i: (b * NT + i, col + h))
    bwd = lambda col: pl.BlockSpec((R, W), lambda b, h, i: (b * NT + NT - 1 - i, col + h))
    return pl.pallas_call(
        functools.partial(_hgrn_kernel, NT=NT, R=R, HP=HP, layer=layer),
        grid=(cfg.B, cfg.HH // HP, NT),
        in_specs=[pl.BlockSpec((2, nl, W), lambda b, h, i: (0, 0, h)),
                  pl.BlockSpec((1, 1, W), lambda b, h, i: (h, 0, 0)),
                  fwd(qc), fwd(ic), fwd(fc), bwd(qc), bwd(ic), bwd(bc),
                  pl.BlockSpec((cfg.S, W), lambda b, h, i: (b, gc + h))],
        out_specs=pl.BlockSpec((cfg.S, W), lambda b, h, i: (b, h)),
        out_shape=jax.ShapeDtypeStruct((cfg.T, cfg.HH * HEAD), BF16),
        scratch_shapes=[pltpu.VMEM((HP, 2, HEAD, HEAD), F32),
                        pltpu.VMEM((cfg.S, W), F32), pltpu.VMEM((cfg.S, W), F32)],
        compiler_params=_params("arbitrary", "arbitrary", "arbitrary"),
        name="hgrn",
    )(lb_logits, gain3, proj, proj, proj, proj, proj, proj, proj)


def _merge_kernel(oa_ref, ob_ref, *refs, nsub):
    ga_refs, gb_refs = refs[:nsub], refs[nsub:2 * nsub]
    wa_ref, wb_ref, o_ref, wabf_ref, wbbf_ref = refs[2 * nsub:]

    @pl.when(pl.program_id(1) == 0)
    def _():
        wabf_ref[...] = wa_ref[...].astype(BF16)
        wbbf_ref[...] = wb_ref[...].astype(BF16)

    ya = _dot(oa_ref[...], wabf_ref[...])
    yb = _dot(ob_ref[...], wbbf_ref[...])
    gw = ya.shape[1] // nsub
    for u in range(nsub):
        cols = slice(u * gw, (u + 1) * gw)
        o_ref[:, cols] = (jax.nn.sigmoid(ga_refs[u][...].astype(F32)) * ya[:, cols]
                          + jax.nn.sigmoid(gb_refs[u][...].astype(F32)) * yb[:, cols]).astype(BF16)


def _merge(cfg, oa, ob, proj, wa, wb):
    offs, _ = cfg.offsets
    gw = _pick(cfg.D, 512)
    assert offs[8] % gw == 0 and offs[9] % gw == 0
    nsub = 2 if cfg.D % (2 * gw) == 0 else 1
    tn = nsub * gw
    ga0, gb0 = offs[8] // gw, offs[9] // gw
    tm = _pick(cfg.T, 1024)
    ka, kb = oa.shape[1], ob.shape[1]
    gate = lambda c0, u: pl.BlockSpec((tm, gw), lambda j, i: (i, c0 + nsub * j + u))
    return pl.pallas_call(
        functools.partial(_merge_kernel, nsub=nsub),
        grid=(cfg.D // tn, cfg.T // tm),
        in_specs=[pl.BlockSpec((tm, ka), lambda j, i: (i, 0)),
                  pl.BlockSpec((tm, kb), lambda j, i: (i, 0)),
                  *[gate(ga0, u) for u in range(nsub)],
                  *[gate(gb0, u) for u in range(nsub)],
                  pl.BlockSpec((ka, tn), lambda j, i: (0, j)),
                  pl.BlockSpec((kb, tn), lambda j, i: (0, j))],
        out_specs=pl.BlockSpec((tm, tn), lambda j, i: (i, j)),
        out_shape=jax.ShapeDtypeStruct((cfg.T, cfg.D), BF16),
        scratch_shapes=[pltpu.VMEM((ka, tn), BF16), pltpu.VMEM((kb, tn), BF16)],
        compiler_params=_params("arbitrary", "arbitrary"),
        name="merge",
    )(oa, ob, *([proj] * (2 * nsub)), wa, wb)


def _cast_kernel(w_ref, o_ref):
    o_ref[...] = w_ref[...].astype(BF16)


def _to_bf16(w):
    k, n = w.shape
    tk = _pick(k, 512)
    return pl.pallas_call(
        _cast_kernel,
        grid=(k // tk,),
        in_specs=[pl.BlockSpec((tk, n), lambda i: (i, 0))],
        out_specs=pl.BlockSpec((tk, n), lambda i: (i, 0)),
        out_shape=jax.ShapeDtypeStruct((k, n), BF16),
        compiler_params=_params("arbitrary"),
        name="wcast",
    )(w)


def _outproj_kernel(a_ref, w_ref, x_ref, mod_ref, gpost_ref, gpre_ref, wr_ref,
                    x1_ref, h_ref, aff_ref, y_ref, *, E, SUB):
    i = pl.program_id(0)

    @pl.when(i == 0)
    def _():
        y_ref[...] = jnp.zeros_like(y_ref)

    def fused(cur, prev):
        y_ref[cur] = _dot(a_ref[...], w_ref[...])
        wr = wr_ref[...]
        w_hi = wr.astype(BF16)
        w_pair = jnp.concatenate([w_hi, (wr - w_hi.astype(F32)).astype(BF16)], axis=1)
        post_gain = mod_ref[0, 2:3, :] * gpost_ref[...]
        pre_gain = gpre_ref[...] * (1.0 + mod_ref[0, 4:5, :])
        for u in range(a_ref.shape[0] // SUB):
            rows = slice(u * SUB, (u + 1) * SUB)
            x1 = x_ref[rows, :] + _rms(y_ref[prev, rows, :]) * post_gain
            x1_ref[rows, :] = x1
            h = _rms(x1) * pre_gain + mod_ref[0, 3:4, :]
            h_bf = h.astype(BF16)
            h_ref[rows, :] = h_bf
            both = _dot(h_bf, w_pair)
            logits = both[:, :LANES] + both[:, LANES:]
            lane = lax.broadcasted_iota(jnp.int32, logits.shape, 1)
            logits = jnp.where(lane < E, logits, -jnp.inf)
            p = jnp.exp(logits - jnp.max(logits, axis=-1, keepdims=True))
            aff_ref[rows, :] = p / jnp.sum(p, axis=-1, keepdims=True)

    @pl.when(i % 2 == 0)
    def _():
        fused(0, 1)

    @pl.when(i % 2 == 1)
    def _():
        fused(1, 0)


def _outproj(cfg, merged, w_bf, x2, mod3, gpost, gpre, wr_pad):
    tm = _pick(cfg.S, 512)
    sub = _pick(tm, 256)
    per_b = cfg.S // tm
    n = cfg.T // tm
    D = cfg.D
    const = lambda i: (0, 0)
    cur = lambda i: (jnp.minimum(i, n - 1), 0)
    lag = lambda i: (jnp.maximum(i - 1, 0), 0)
    return pl.pallas_call(
        functools.partial(_outproj_kernel, E=cfg.E, SUB=sub),
        grid=(n + 1,),
        in_specs=[pl.BlockSpec((tm, D), cur),
                  pl.BlockSpec((D, D), const, pipeline_mode=pl.Buffered(1)),
                  pl.BlockSpec((tm, D), lag),
                  pl.BlockSpec((1, 6, D), lambda i: (jnp.maximum(i - 1, 0) // per_b, 0, 0)),
                  pl.BlockSpec((1, D), const),
                  pl.BlockSpec((1, D), const),
                  pl.BlockSpec((D, LANES), const)],
        out_specs=[pl.BlockSpec((tm, D), lag),
                   pl.BlockSpec((tm, D), lag),
                   pl.BlockSpec((tm, LANES), lag)],
        out_shape=[jax.ShapeDtypeStruct((cfg.T, D), F32),
                   jax.ShapeDtypeStruct((cfg.T, D), BF16),
                   jax.ShapeDtypeStruct((cfg.T, LANES), F32)],
        scratch_shapes=[pltpu.VMEM((2, tm, D), F32)],
        compiler_params=_params("arbitrary"),
        name="outproj",
    )(merged, w_bf, x2, mod3, gpost, gpre, wr_pad)


def _route_kernel(aff_ref, posm_ref, postm_ref, offs_ref, a_ref, *, B, E, S, C):
    nt = S // LANES
    BE = B * E
    for b in range(B):
        for t in range(nt):
            tok = slice(b * S + t * LANES, b * S + (t + 1) * LANES)
            a_ref[b * E:(b + 1) * E, t * LANES:(t + 1) * LANES] = aff_ref[tok, :].T[:E, :]

    cap = jnp.float32(C)

    def narrow(_, carry):
        lo, hi = carry
        width = hi - lo
        a = a_ref[...]
        new_lo, new_hi = lo, hi
        found = jnp.zeros(lo.shape, jnp.bool_)
        for j in range(SEARCH_PROBES, 0, -1):
            m = lo + (j / (SEARCH_PROBES + 1)) * width
            keep = jnp.sum(jnp.where(a >= m, 1.0, 0.0), axis=1, keepdims=True) >= cap
            new_lo = jnp.where(keep & ~found, m, new_lo)
            new_hi = jnp.where(keep | found, new_hi, m)
            found = found | keep
        return new_lo, new_hi

    lo, hi = lax.fori_loop(0, SEARCH_ROUNDS, narrow,
                           (jnp.zeros((BE, 1), F32), jnp.full((BE, 1), 2.0, F32)))
    need = cap - jnp.sum(jnp.where(a_ref[...] >= hi, 1.0, 0.0), axis=1, keepdims=True)

    ui = lax.broadcasted_iota(jnp.int32, (LANES, LANES), 0)
    uj = lax.broadcasted_iota(jnp.int32, (LANES, LANES), 1)
    upper = jnp.where(ui <= uj, 1.0, 0.0).astype(BF16)
    lane = lax.broadcasted_iota(jnp.int32, (BE, LANES), 1)
    run_tie = jnp.zeros((BE, 1), F32)
    run_sel = jnp.zeros((BE, 1), F32)
    offs = jnp.zeros((BE, LANES), F32)
    pad = jnp.full((LANES - E, LANES), -1.0, F32)
    for t in range(nt):
        cols = slice(t * LANES, (t + 1) * LANES)
        a = a_ref[:, cols]
        tie = (a >= lo) & (a < hi)
        tie_t = jnp.where(tie, 1.0, 0.0)
        tie_rank = _dot(tie_t.astype(BF16), upper) - tie_t + run_tie
        sel = (a >= hi) | (tie & (tie_rank < need))
        sel_t = jnp.where(sel, 1.0, 0.0)
        pos = _dot(sel_t.astype(BF16), upper) - sel_t + run_sel
        posm = jnp.where(sel, pos, -1.0)
        posm_ref[:, cols] = posm
        for b in range(B):
            tok = slice(b * S + t * LANES, b * S + (t + 1) * LANES)
            postm_ref[tok, :] = jnp.concatenate([posm[b * E:(b + 1) * E], pad], axis=0).T
        offs = jnp.where(lane == t, run_sel, offs)
        run_tie = run_tie + jnp.sum(tie_t, axis=1, keepdims=True)
        run_sel = run_sel + jnp.sum(sel_t, axis=1, keepdims=True)
    offs = jnp.where(lane == nt, run_sel, offs)
    offs_ref[...] = offs.astype(jnp.int32)


def _route(cfg, aff_tm):
    B, E, S = cfg.B, cfg.E, cfg.S
    assert S // LANES < LANES and E % 8 == 0
    whole = lambda i: (0, 0)
    return pl.pallas_call(
        functools.partial(_route_kernel, B=B, E=E, S=S, C=cfg.C),
        grid=(1,),
        in_specs=[pl.BlockSpec((cfg.T, LANES), whole)],
        out_specs=[pl.BlockSpec((B * E, S), whole),
                   pl.BlockSpec((cfg.T, LANES), whole),
                   pl.BlockSpec((B * E, LANES), whole)],
        out_shape=[jax.ShapeDtypeStruct((B * E, S), F32),
                   jax.ShapeDtypeStruct((cfg.T, LANES), F32),
                   jax.ShapeDtypeStruct((B * E, LANES), jnp.int32)],
        scratch_shapes=[pltpu.VMEM((B * E, S), F32)],
        compiler_params=_params("arbitrary"),
        name="route",
    )(aff_tm)


def _slot_window(offs_ref, row, token_block, C):
    base = row * LANES + token_block * (COMBINE_TOKENS // LANES)
    first, end = offs_ref[base], offs_ref[base + COMBINE_TOKENS // LANES]
    start = jnp.minimum((first // BF16_ROWS) * BF16_ROWS, C - COMBINE_WINDOW)
    return pl.multiple_of(start, BF16_ROWS), end


def _extra_windows(start, end):
    return (jnp.maximum(end - start - COMBINE_WINDOW, 0) + COMBINE_WINDOW - 1) // COMBINE_WINDOW


def _gather_kernel(offs_ref, posm_ref, h_ref, x_ref, *, E, G, C):
    b, grp, kb = pl.program_id(0), pl.program_id(1), pl.program_id(2)
    W, TB = COMBINE_WINDOW, COMBINE_TOKENS

    @pl.when(kb == 0)
    def _():
        x_ref[...] = jnp.zeros_like(x_ref)

    slot0 = lax.broadcasted_iota(jnp.int32, (W, TB), 0)
    starts, pieces = [], []
    for j in range(G):
        start, _ = _slot_window(offs_ref, b * E + grp * G + j, kb, C)
        starts.append(start)
        pieces.append(jnp.where(posm_ref[j] == (start + slot0).astype(F32), 1.0, 0.0).astype(BF16))
    rows = _dot(jnp.concatenate(pieces, axis=0), h_ref[...]).astype(BF16)
    for j in range(G):
        win = pl.ds(starts[j], W)
        x_ref[j, win, :] = x_ref[j, win, :] + rows[j * W:(j + 1) * W]

    def spill(j, carry):
        start, end = _slot_window(offs_ref, b * E + grp * G + j, kb, C)

        def extra(w, c2):
            cur = start + (w + 1) * W
            s2 = pl.multiple_of(jnp.minimum(cur, C - W), BF16_ROWS)
            slot = s2 + slot0
            onehot = jnp.where((posm_ref[j] == slot.astype(F32)) & (slot >= cur), 1.0, 0.0).astype(BF16)
            win = pl.ds(s2, W)
            x_ref[j, win, :] = x_ref[j, win, :] + _dot(onehot, h_ref[...]).astype(BF16)
            return c2

        return lax.fori_loop(0, _extra_windows(start, end), extra, carry)

    lax.fori_loop(0, G, spill, 0)


def _gather(cfg, offs_flat, posm3, h2):
    B, E, S, C, D = cfg.B, cfg.E, cfg.S, cfg.C, cfg.D
    G, TB = min(E, GATHER_EXPERTS), COMBINE_TOKENS
    assert E % G == 0 and S % TB == 0 and C % COMBINE_WINDOW == 0
    nk = S // TB
    grid_spec = pltpu.PrefetchScalarGridSpec(
        num_scalar_prefetch=1,
        grid=(B, E // G, nk),
        in_specs=[pl.BlockSpec((G, 1, TB), lambda b, g, k, offs: (b * (E // G) + g, 0, k)),
                  pl.BlockSpec((TB, D), lambda b, g, k, offs: (b * nk + k, 0))],
        out_specs=pl.BlockSpec((G, C, D), lambda b, g, k, offs: (g, b, 0)),
    )
    return pl.pallas_call(
        functools.partial(_gather_kernel, E=E, G=G, C=C),
        grid_spec=grid_spec,
        out_shape=jax.ShapeDtypeStruct((E, B * C, D), BF16),
        compiler_params=_params("arbitrary", "arbitrary", "arbitrary"),
        name="gather",
    )(offs_flat, posm3, h2)


def _ffn_kernel(x_ref, wg_ref, wu_ref, wd_ref, y_ref, acc_ref):
    f = pl.program_id(1)

    @pl.when(f == 0)
    def _():
        acc_ref[...] = jnp.zeros_like(acc_ref)

    x = x_ref[0]
    a = (_silu(_dot(x, wg_ref[0].astype(BF16))) * _dot(x, wu_ref[0].astype(BF16))).astype(BF16)
    acc_ref[...] += _dot(a, wd_ref[0].astype(BF16))

    @pl.when(f == pl.num_programs(1) - 1)
    def _():
        y_ref[0] = acc_ref[...].astype(BF16)


def _ffn(cfg, xe, wg, wu, wd):
    B, E, C, D, FF = cfg.B, cfg.E, cfg.C, cfg.D, cfg.FF
    tf = _pick(FF, 256)
    return pl.pallas_call(
        _ffn_kernel,
        grid=(E, FF // tf),
        in_specs=[pl.BlockSpec((1, B * C, D), lambda e, f: (e, 0, 0)),
                  pl.BlockSpec((1, D, tf), lambda e, f: (e, 0, f)),
                  pl.BlockSpec((1, D, tf), lambda e, f: (e, 0, f)),
                  pl.BlockSpec((1, tf, D), lambda e, f: (e, f, 0))],
        out_specs=pl.BlockSpec((1, B * C, D), lambda e, f: (e, 0, 0)),
        out_shape=jax.ShapeDtypeStruct((E, B * C, D), BF16),
        scratch_shapes=[pltpu.VMEM((B * C, D), F32)],
        compiler_params=_params("arbitrary", "arbitrary"),
        name="ffn",
    )(xe, wg, wu, wd)


def _combine_kernel(offs_ref, pos_ref, aff_ref, ppos_ref, paff_ref, y_hbm, x1_ref, mod_ref, g_ref, o_ref,
                    z_ref, acc_ref, zx_ref, sem, semx, *, E, C, NK):
    b, k = pl.program_id(0), pl.program_id(1)
    W, TB = COMBINE_WINDOW, COMBINE_TOKENS
    lane = lax.broadcasted_iota(jnp.int32, (TB, LANES), 1)

    def window_copy(e, start, half):
        return pltpu.make_async_copy(y_hbm.at[e, pl.ds(b * C + start, W), :],
                                     z_ref.at[half, pl.ds(e * W, W), :], sem.at[half])

    def windows(block, half, wait):
        starts = []
        for e in range(E):
            start, _ = _slot_window(offs_ref, b * E + e, block, C)
            cp = window_copy(e, start, half)
            cp.wait() if wait else cp.start()
            starts.append(start)
        return starts

    @pl.when(k == 0)
    def _():
        acc_ref[...] = jnp.zeros_like(acc_ref)
        windows(0, 0, wait=False)

    def spill(e, carry):
        start, end = _slot_window(offs_ref, b * E + e, k - 1, C)

        def extra(w, c2):
            mine = lane == e
            pos = jnp.sum(jnp.where(mine, ppos_ref[...], 0.0), axis=1, keepdims=True)
            aff = jnp.sum(jnp.where(mine, paff_ref[...], 0.0), axis=1, keepdims=True)
            cur = start + (w + 1) * W
            s2 = pl.multiple_of(jnp.minimum(cur, C - W), BF16_ROWS)
            slot = s2 + lax.broadcasted_iota(jnp.int32, (TB, W), 1)
            m = jnp.where((pos == slot.astype(F32)) & (slot >= cur), aff, 0.0).astype(BF16)
            cp = pltpu.make_async_copy(y_hbm.at[e, pl.ds(b * C + s2, W), :], zx_ref, semx)
            cp.start()
            cp.wait()
            acc_ref[(k + 1) % 2] += _dot(m, zx_ref[...])
            return c2

        return lax.fori_loop(0, _extra_windows(start, end), extra, carry)

    @pl.when(k > 0)
    def _():
        lax.fori_loop(0, E, spill, 0)

    def fused(cur, prev):
        windows(k + 1, prev, wait=False)
        starts = [_slot_window(offs_ref, b * E + e, k, C)[0] for e in range(E)]
        o_ref[...] = x1_ref[...] + mod_ref[0, 5:6, :] * (_rms(acc_ref[prev]) * g_ref[...])
        seg = lane // W
        off = lane - seg * W
        group = LANES // W
        pieces = []
        for p in range(E // group):
            slot = jnp.zeros((TB, LANES), jnp.int32)
            pos = jnp.full((TB, LANES), -1.0, F32)
            aff = jnp.zeros((TB, LANES), F32)
            for q in range(group):
                e = p * group + q
                mine = seg == q
                slot = jnp.where(mine, starts[e] + off, slot)
                pos = jnp.where(mine, pos_ref[:, e:e + 1], pos)
                aff = jnp.where(mine, aff_ref[:, e:e + 1], aff)
            pieces.append(jnp.where(pos == slot.astype(F32), aff, 0.0).astype(BF16))
        m = jnp.concatenate(pieces, axis=1)
        windows(k, cur, wait=True)
        acc_ref[cur] = _dot(m, z_ref[cur])

        @pl.when(k == NK)
        def _():
            windows(k + 1, prev, wait=True)

    @pl.when(k % 2 == 0)
    def _():
        fused(0, 1)

    @pl.when(k % 2 == 1)
    def _():
        fused(1, 0)


def _combine(cfg, offs_flat, pos_tm, aff_tm, y, x1, mod3, g):
    B, E, S, C, D = cfg.B, cfg.E, cfg.S, cfg.C, cfg.D
    W, TB = COMBINE_WINDOW, COMBINE_TOKENS
    assert LANES % W == 0 and E % (LANES // W) == 0 and C % W == 0 and W % BF16_ROWS == 0 and S % TB == 0
    nk = S // TB
    assert (nk + 3) * (TB // LANES) <= LANES
    cur = lambda b, k, offs: (b * nk + jnp.minimum(k, nk - 1), 0)
    lag = lambda b, k, offs: (b * nk + jnp.maximum(k - 1, 0), 0)
    grid_spec = pltpu.PrefetchScalarGridSpec(
        num_scalar_prefetch=1,
        grid=(B, nk + 1),
        in_specs=[pl.BlockSpec((TB, LANES), cur),
                  pl.BlockSpec((TB, LANES), cur),
                  pl.BlockSpec((TB, LANES), lag),
                  pl.BlockSpec((TB, LANES), lag),
                  pl.BlockSpec(memory_space=pl.ANY),
                  pl.BlockSpec((TB, D), lag),
                  pl.BlockSpec((1, 6, D), lambda b, k, offs: (b, 0, 0)),
                  pl.BlockSpec((1, D), lambda b, k, offs: (0, 0))],
        out_specs=pl.BlockSpec((TB, D), lag),
        scratch_shapes=[pltpu.VMEM((2, E * W, D), BF16), pltpu.VMEM((2, TB, D), F32),
                        pltpu.VMEM((W, D), BF16),
                        pltpu.SemaphoreType.DMA((2,)), pltpu.SemaphoreType.DMA(())],
    )
    return pl.pallas_call(
        functools.partial(_combine_kernel, E=E, C=C, NK=nk),
        grid_spec=grid_spec,
        out_shape=jax.ShapeDtypeStruct((cfg.T, D), F32),
        compiler_params=_params("arbitrary", "arbitrary"),
        name="combine",
    )(offs_flat, pos_tm, aff_tm, pos_tm, aff_tm, y, x1, mod3, g)


def _layer(cfg, layer, x2, c_pad, pos2, inv, sgn, w_ada, b_ada, g_pre_mix, g_post_mix, g_pre_ffn,
           g_post_ffn, w_in, hg_lb_logits, hg_out_norm, attn_sink, w_branch_a, w_branch_b, w_out,
           w_router, w_exp_gate, w_exp_up, w_exp_down):
    D, E = cfg.D, cfg.E
    row = lambda v: v.reshape(1, -1)
    mod = _ada(c_pad, w_ada[layer], row(b_ada[layer]))[:cfg.B]
    mod3 = mod.reshape(cfg.B, 6, D)
    h = _prenorm(cfg, x2, row(g_pre_mix[layer]), mod3)
    proj = _inproj(h, w_in[layer])
    q_rot, k_rot = _rope(cfg, proj, pos2, inv, sgn)
    att = _attn(cfg, q_rot, k_rot, proj, attn_sink[layer])
    gain3 = hg_out_norm[layer].reshape(cfg.HH // HGRN_HEADS_PER_STEP, 1, HGRN_HEADS_PER_STEP * HEAD)
    oa = _hgrn(cfg, proj, hg_lb_logits, gain3, layer)
    merged = _merge(cfg, oa, att, proj, w_branch_a[layer], w_branch_b[layer])
    wr_pad = jnp.pad(w_router[layer], ((0, 0), (0, LANES - E)))
    x1, h2, aff_tm = _outproj(cfg, merged, _to_bf16(w_out[layer]), x2, mod3, row(g_post_mix[layer]),
                              row(g_pre_ffn[layer]), wr_pad)
    posm, pos_tm, offs = _route(cfg, aff_tm)
    offs_flat = offs.reshape(-1)
    xe = _gather(cfg, offs_flat, posm.reshape(cfg.B * E, 1, cfg.S), h2)
    ye = _ffn(cfg, xe, w_exp_gate[layer], w_exp_up[layer], w_exp_down[layer])
    return _combine(cfg, offs_flat, pos_tm, aff_tm, ye, x1, mod3, row(g_post_ffn[layer]))


def _block(cfg, x, c, positions, w_ada, *rest):
    depth = w_ada.shape[0]
    x2 = x.reshape(cfg.T, cfg.D)
    c_pad = jnp.pad(c, ((0, 8 - cfg.B), (0, 0)))
    pos2 = positions.reshape(cfg.T, 1)
    half = HEAD // 2
    inv_half = ROPE_THETA ** (-jnp.arange(half, dtype=F32) / half)
    inv = jnp.concatenate([inv_half, inv_half]).reshape(1, HEAD)
    sgn = jnp.concatenate([-jnp.ones((half,), F32), jnp.ones((half,), F32)]).reshape(1, HEAD)
    for layer in range(depth):
        x2 = _layer(cfg, layer, x2, c_pad, pos2, inv, sgn, w_ada, *rest)
    return x2.reshape(cfg.B, cfg.S, cfg.D)


def kernel(x, c, positions, w_ada, b_ada, g_pre_mix, g_post_mix, g_pre_ffn, g_post_ffn, w_in,
           hg_lb_logits, hg_out_norm, attn_sink, w_branch_a, w_branch_b, w_out, w_router,
           w_exp_gate, w_exp_up, w_exp_down):
    B, S, D = x.shape
    cfg = Cfg(B=B, S=S, D=D,
              HH=hg_out_norm.shape[1], AH=attn_sink.shape[1],
              KVH=(w_in.shape[2] - 5 * hg_out_norm.shape[1] * HEAD - attn_sink.shape[1] * HEAD - 2 * D) // (2 * HEAD),
              E=w_router.shape[2], FF=w_exp_gate.shape[3])
    return _block(cfg, x, c, positions, w_ada, b_ada, g_pre_mix, g_post_mix, g_pre_ffn, g_post_ffn,
                  w_in, hg_lb_logits, hg_out_norm, attn_sink, w_branch_a, w_branch_b, w_out,
                  w_router, w_exp_gate, w_exp_up, w_exp_down)
```

```python
import functools
from typing import NamedTuple

import jax
import jax.numpy as jnp
from jax import lax
from jax.experimental import pallas as pl
from jax.experimental.pallas import tpu as pltpu

F32 = jnp.float32
BF16 = jnp.bfloat16

LANES = 128
HEAD = 128
HG_CHUNK = 64
WINDOW = 128
ROPE_THETA = 10000.0
NORM_EPS = 1e-6
CAPACITY_FACTOR = 2
VMEM_LIMIT_BYTES = 56 * 1024 * 1024
SEARCH_PROBES = 3
SEARCH_ROUNDS = 76
HGRN_HEADS_PER_STEP = 2
HGRN_SCORE_ROWS = 256
COMBINE_TOKENS = 256
COMBINE_WINDOW = 64
GATHER_EXPERTS = 8
BF16_ROWS = 16


class Cfg(NamedTuple):
    B: int
    S: int
    D: int
    HH: int
    AH: int
    KVH: int
    E: int
    FF: int

    @property
    def T(self):
        return self.B * self.S

    @property
    def C(self):
        return CAPACITY_FACTOR * self.S // self.E

    @property
    def offsets(self):
        hq = self.HH * HEAD
        widths = (hq, hq, hq, hq, hq, self.AH * HEAD, self.KVH * HEAD, self.KVH * HEAD, self.D, self.D)
        offs, o = [], 0
        for w in widths:
            offs.append(o)
            o += w
        return tuple(offs), o


def _params(*sem):
    return pltpu.CompilerParams(dimension_semantics=sem, vmem_limit_bytes=VMEM_LIMIT_BYTES)


def _pick(n, target):
    if n <= target:
        return n
    t = (target // LANES) * LANES
    while t > LANES and n % t:
        t -= LANES
    return t


def _dot(a, b):
    return jnp.dot(a, b, preferred_element_type=F32)


def _dot_nt(a, b):
    return lax.dot_general(a, b, (((1,), (1,)), ((), ())), preferred_element_type=F32)


def _dot_tn(a, b):
    return lax.dot_general(a, b, (((0,), (0,)), ((), ())), preferred_element_type=F32)


def _silu(x):
    return x * jax.nn.sigmoid(x)


def _rms(x):
    return x * lax.rsqrt(jnp.mean(x * x, axis=-1, keepdims=True) + NORM_EPS)


def _ada_kernel(c_ref, w_ref, b_ref, o_ref):
    a = _silu(c_ref[...]).astype(BF16)
    o_ref[...] = _dot(a, w_ref[...].astype(BF16)) + b_ref[...]


def _ada(c_pad, w, b):
    rows, d = c_pad.shape
    n = w.shape[1]
    tn = _pick(n, 1024)
    return pl.pallas_call(
        _ada_kernel,
        grid=(n // tn,),
        in_specs=[pl.BlockSpec((rows, d), lambda j: (0, 0)),
                  pl.BlockSpec((d, tn), lambda j: (0, j)),
                  pl.BlockSpec((1, tn), lambda j: (0, j))],
        out_specs=pl.BlockSpec((rows, tn), lambda j: (0, j)),
        out_shape=jax.ShapeDtypeStruct((rows, n), F32),
        compiler_params=_params("arbitrary"),
        name="ada",
    )(c_pad, w, b)


def _prenorm_kernel(x_ref, g_ref, mod_ref, o_ref):
    y = _rms(x_ref[...]) * g_ref[...]
    shift = mod_ref[0, 0:1, :]
    scale = mod_ref[0, 1:2, :]
    o_ref[...] = (y * (1.0 + scale) + shift).astype(BF16)


def _prenorm(cfg, x2, g, mod3):
    tm = _pick(cfg.S, 1024)
    per_b = cfg.S // tm
    return pl.pallas_call(
        _prenorm_kernel,
        grid=(cfg.T // tm,),
        in_specs=[pl.BlockSpec((tm, cfg.D), lambda i: (i, 0)),
                  pl.BlockSpec((1, cfg.D), lambda i: (0, 0)),
                  pl.BlockSpec((1, 6, cfg.D), lambda i: (i // per_b, 0, 0))],
        out_specs=pl.BlockSpec((tm, cfg.D), lambda i: (i, 0)),
        out_shape=jax.ShapeDtypeStruct((cfg.T, cfg.D), BF16),
        compiler_params=_params("arbitrary"),
        name="prenorm",
    )(x2, g, mod3)


def _inproj_kernel(a_ref, w_ref, o_ref, wbf_ref):
    @pl.when(pl.program_id(1) == 0)
    def _():
        wbf_ref[...] = w_ref[...].astype(BF16)

    o_ref[...] = _dot(a_ref[...], wbf_ref[...]).astype(o_ref.dtype)


def _inproj(a, w):
    m, k = a.shape
    n = w.shape[1]
    tn = _pick(n, 1792)
    tm = _pick(m, 1024)
    return pl.pallas_call(
        _inproj_kernel,
        grid=(n // tn, m // tm),
        in_specs=[pl.BlockSpec((tm, k), lambda j, i: (i, 0)),
                  pl.BlockSpec((k, tn), lambda j, i: (0, j))],
        out_specs=pl.BlockSpec((tm, tn), lambda j, i: (i, j)),
        out_shape=jax.ShapeDtypeStruct((m, n), BF16),
        scratch_shapes=[pltpu.VMEM((k, tn), BF16)],
        compiler_params=_params("arbitrary", "arbitrary"),
        name="inproj",
    )(a, w)


def _rope_kernel(q_ref, k_ref, pos_ref, inv_ref, sgn_ref, qo_ref, ko_ref, *, q_scale):
    ang = pos_ref[...].astype(F32) * inv_ref[...]
    cos = jnp.cos(ang)
    sin = jnp.sin(ang) * sgn_ref[...]

    def rotate(x_ref, o_ref, scale):
        for h in range(x_ref.shape[1] // HEAD):
            cols = slice(h * HEAD, (h + 1) * HEAD)
            x = x_ref[:, cols].astype(F32)
            r = x * cos + pltpu.roll(x, HEAD // 2, 1) * sin
            o_ref[:, cols] = (r if scale is None else r * scale).astype(BF16)

    rotate(q_ref, qo_ref, q_scale)
    rotate(k_ref, ko_ref, None)


def _rope(cfg, proj, pos2, inv, sgn):
    offs, _ = cfg.offsets
    wq, wk = cfg.AH * HEAD, cfg.KVH * HEAD
    assert offs[5] % wq == 0 and offs[6] % wk == 0
    tm = _pick(cfg.T, 512)
    return pl.pallas_call(
        functools.partial(_rope_kernel, q_scale=HEAD ** -0.5),
        grid=(cfg.T // tm,),
        in_specs=[pl.BlockSpec((tm, wq), lambda i: (i, offs[5] // wq)),
                  pl.BlockSpec((tm, wk), lambda i: (i, offs[6] // wk)),
                  pl.BlockSpec((tm, 1), lambda i: (i, 0)),
                  pl.BlockSpec((1, HEAD), lambda i: (0, 0)),
                  pl.BlockSpec((1, HEAD), lambda i: (0, 0))],
        out_specs=[pl.BlockSpec((tm, wq), lambda i: (i, 0)),
                   pl.BlockSpec((tm, wk), lambda i: (i, 0))],
        out_shape=[jax.ShapeDtypeStruct((cfg.T, wq), BF16),
                   jax.ShapeDtypeStruct((cfg.T, wk), BF16)],
        compiler_params=_params("arbitrary"),
        name="rope",
    )(proj, proj, pos2, inv, sgn)


def _attn_kernel(sink_ref, q_ref, kp_ref, kc_ref, kn_ref, vp_ref, vc_ref, vn_ref, o_ref, *, G, KVH, NB):
    n = pl.program_id(1)
    blk = WINDOW
    row = lax.broadcasted_iota(jnp.int32, (G * blk, 3 * blk), 0)
    qi = row - (row // blk) * blk
    kj = lax.broadcasted_iota(jnp.int32, (G * blk, 3 * blk), 1)
    valid = jnp.abs(kj - blk - qi) <= WINDOW
    valid = valid & ((kj >= blk) | (n > 0)) & ((kj < 2 * blk) | (n < NB - 1))
    head = lax.broadcasted_iota(jnp.int32, (G * blk, 1), 0) // blk
    for kvh in range(KVH):
        kv = slice(kvh * HEAD, (kvh + 1) * HEAD)
        k = jnp.concatenate([kp_ref[:, kv], kc_ref[:, kv], kn_ref[:, kv]], axis=0)
        v = jnp.concatenate([vp_ref[:, kv], vc_ref[:, kv], vn_ref[:, kv]], axis=0).astype(BF16)
        heads = [kvh * G + g for g in range(G)]
        q = jnp.concatenate([q_ref[:, h * HEAD:(h + 1) * HEAD] for h in heads], axis=0)
        sk = jnp.zeros((G * blk, 1), F32)
        for g, h in enumerate(heads):
            sk = jnp.where(head == g, sink_ref[h], sk)
        s = jnp.where(valid, _dot_nt(q, k), -jnp.inf)
        m = jnp.maximum(jnp.max(s, axis=-1, keepdims=True), sk)
        p = jnp.exp(s - m)
        denom = jnp.sum(p, axis=-1, keepdims=True) + jnp.exp(sk - m)
        o = (_dot(p.astype(BF16), v) / denom).astype(BF16)
        for g, h in enumerate(heads):
            o_ref[:, h * HEAD:(h + 1) * HEAD] = o[g * blk:(g + 1) * blk]


def _attn(cfg, q_rot, k_rot, proj, sink):
    offs, _ = cfg.offsets
    G = cfg.AH // cfg.KVH
    blk = WINDOW
    NB = cfg.S // blk
    wq, wkv = cfg.AH * HEAD, cfg.KVH * HEAD
    assert offs[7] % wkv == 0
    vcol = offs[7] // wkv
    prev = lambda n: jnp.maximum(n - 1, 0)
    nxt = lambda n: jnp.minimum(n + 1, NB - 1)
    return pl.pallas_call(
        functools.partial(_attn_kernel, G=G, KVH=cfg.KVH, NB=NB),
        grid=(cfg.B, NB),
        in_specs=[pl.BlockSpec(memory_space=pltpu.SMEM),
                  pl.BlockSpec((blk, wq), lambda b, n: (b * NB + n, 0)),
                  pl.BlockSpec((blk, wkv), lambda b, n: (b * NB + prev(n), 0)),
                  pl.BlockSpec((blk, wkv), lambda b, n: (b * NB + n, 0)),
                  pl.BlockSpec((blk, wkv), lambda b, n: (b * NB + nxt(n), 0)),
                  pl.BlockSpec((blk, wkv), lambda b, n: (b * NB + prev(n), vcol)),
                  pl.BlockSpec((blk, wkv), lambda b, n: (b * NB + n, vcol)),
                  pl.BlockSpec((blk, wkv), lambda b, n: (b * NB + nxt(n), vcol))],
        out_specs=pl.BlockSpec((blk, wq), lambda b, n: (b * NB + n, 0)),
        out_shape=jax.ShapeDtypeStruct((cfg.T, wq), BF16),
        compiler_params=_params("arbitrary", "arbitrary"),
        name="attn",
    )(sink, q_rot, k_rot, k_rot, k_rot, proj, proj, proj)


def _lower_bound(lbl_ref, d, layer):
    n = lbl_ref.shape[1]
    rows = [lbl_ref[d, j:j + 1, :] for j in range(n)]
    m = functools.reduce(jnp.maximum, rows)
    es = [jnp.exp(r - m) for r in rows]
    return sum(es[:layer + 1]) / sum(es)


def _hgrn_cumsum(logfs, tri2):
    P = 2 * HG_CHUNK
    npair = logfs[0].shape[0] // P
    cols = []
    for lf in logfs:
        hi = lf.astype(BF16)
        lo = (lf - hi.astype(F32)).astype(BF16)
        for p in range(npair):
            cols += [hi[p * P:(p + 1) * P], lo[p * P:(p + 1) * P]]
    both = _dot(tri2.astype(BF16), jnp.concatenate(cols, axis=1))
    outs = []
    for s in range(len(logfs)):
        parts = []
        for p in range(npair):
            c0 = (s * npair + p) * 2 * HEAD
            parts.append(both[:, c0:c0 + HEAD] + both[:, c0 + HEAD:c0 + 2 * HEAD])
        outs.append(jnp.concatenate(parts, axis=0))
    return outs


def _hgrn_tile(qs, k, b, v, st, mask, fwd):
    L = HG_CHUNK
    nc = qs.shape[0] // L
    ref_i, last_i = (L // 2, L - 1) if fwd else (L - 1 - L // 2, 0)

    def per_chunk(rows):
        return jnp.concatenate([jnp.broadcast_to(r, (L, HEAD)) for r in rows], axis=0)

    brefs = [b[c * L + ref_i:c * L + ref_i + 1, :] for c in range(nc)]
    blasts = [b[c * L + last_i:c * L + last_i + 1, :] for c in range(nc)]
    d = b - per_chunk(brefs)
    q_in = qs * jnp.exp(d)
    k_in = k * jnp.exp(-d)
    q_bf, k_bf, v_bf = q_in.astype(BF16), k_in.astype(BF16), v.astype(BF16)
    nb = mask.shape[0]
    o_intra = []
    for r0 in range(0, qs.shape[0], nb):
        r = slice(r0, r0 + nb)
        a = jnp.where(mask, _dot_nt(q_bf[r], k_bf[r]), 0.0)
        o_intra.append(_dot(a.astype(BF16), v_bf[r]))
    o_intra = jnp.concatenate(o_intra, axis=0)
    q_dec = (q_in * per_chunk([jnp.exp(r) for r in brefs])).astype(BF16)
    k_dec = k_in * per_chunk([jnp.exp(l - r) for l, r in zip(blasts, brefs)])

    zero = jnp.zeros((L, HEAD), F32)
    cs = []
    for p in range(nc // 2):
        r = slice(2 * p * L, 2 * (p + 1) * L)
        kd = k_dec[r]
        k_blk = jnp.concatenate([jnp.concatenate([kd[:L], zero], axis=1),
                                 jnp.concatenate([zero, kd[L:]], axis=1)], axis=0).astype(BF16)
        both = _dot_tn(v_bf[r], k_blk)
        cs += [both[:, :HEAD], both[:, HEAD:]]
    entering = [None] * nc
    for c in (range(nc) if fwd else reversed(range(nc))):
        entering[c] = st
        st = st * jnp.exp(blasts[c]) + cs[c]
    o_inter = []
    for p in range(nc // 2):
        r = slice(2 * p * L, 2 * (p + 1) * L)
        st2 = jnp.concatenate([entering[2 * p], entering[2 * p + 1]], axis=0).astype(BF16)
        both = _dot_nt(q_dec[r], st2)
        o_inter += [both[:L, :HEAD], both[L:, HEAD:]]
    return o_intra + jnp.concatenate(o_inter, axis=0), st


def _hgrn_kernel(lbl_ref, gain_ref, qf_ref, vf_ref, zf_ref, qb_ref, vb_ref, zb_ref, g_ref, o_ref,
                 st_ref, of_ref, ob_ref, *, NT, R, HP, layer):
    i = pl.program_id(2)

    def chunk_masks(n):
        li = lax.broadcasted_iota(jnp.int32, (n, n), 0)
        mi = lax.broadcasted_iota(jnp.int32, (n, n), 1)
        same = (li // HG_CHUNK) == (mi // HG_CHUNK)
        return same & (mi <= li), same & (mi >= li)

    tri_f, tri_b = (jnp.where(m, 1.0, 0.0).astype(F32) for m in chunk_masks(2 * HG_CHUNK))
    mask_f, mask_b = chunk_masks(min(R, HGRN_SCORE_ROWS))

    @pl.when(i == 0)
    def _():
        st_ref[...] = jnp.zeros_like(st_ref)

    lanes = [slice(p * HEAD, (p + 1) * HEAD) for p in range(HP)]
    for d, (q_ref, v_ref, z_ref) in enumerate(((qf_ref, vf_ref, zf_ref), (qb_ref, vb_ref, zb_ref))):
        fwd = d == 0
        lb = _lower_bound(lbl_ref, d, layer)
        f = lb + (1.0 - lb) * jax.nn.sigmoid(z_ref[...].astype(F32))
        k = 1.0 - f
        qs = _silu(q_ref[...].astype(F32))
        cums = _hgrn_cumsum([jnp.log(f)[:, lanes[p]] for p in range(HP)], tri_f if fwd else tri_b)
        tile = i if fwd else NT - 1 - i
        rows = pl.ds(pl.multiple_of(tile * R, R), R)
        for p in range(HP):
            o, st = _hgrn_tile(qs[:, lanes[p]], k[:, lanes[p]], cums[p], v_ref[:, lanes[p]], st_ref[p, d],
                               mask_f if fwd else mask_b, fwd)
            (of_ref if fwd else ob_ref)[rows, lanes[p]] = o
            st_ref[p, d] = st

    @pl.when(i == NT - 1)
    def _():
        for t in range(NT):
            rows = slice(t * R, (t + 1) * R)
            for p in range(HP):
                o = of_ref[rows, lanes[p]] + ob_ref[rows, lanes[p]]
                gate = _silu(g_ref[rows, lanes[p]].astype(F32))
                o_ref[rows, lanes[p]] = (_rms(o) * gain_ref[0, :, lanes[p]] * gate).astype(BF16)


def _hgrn(cfg, proj, lb_logits, gain3, layer):
    offs, _ = cfg.offsets
    HP = HGRN_HEADS_PER_STEP
    W = HP * HEAD
    assert cfg.HH % HP == 0 and all(o % W == 0 for o in offs[:5])
    qc, fc, bc, ic, gc = (o // W for o in offs[:5])
    R = _pick(cfg.S, 512)
    NT = cfg.S // R
    nl = lb_logits.shape[1]
    fwd = lambda col: pl.BlockSpec((R, W), lambda b, h, i: (b * NT + i, col + h))
    bwd = lambda col: pl.BlockSpec((R, W), lambda b, h, i: (b * NT + NT - 1 - i, col + h))
    return pl.pallas_call(
        functools.partial(_hgrn_kernel, NT=NT, R=R, HP=HP, layer=layer),
        grid=(cfg.B, cfg.HH // HP, NT),
        in_specs=[pl.BlockSpec((2, nl, W), lambda b, h, i: (0, 0, h)),
                  pl.BlockSpec((1, 1, W), lambda b, h, i: (h, 0, 0)),
                  fwd(qc), fwd(ic), fwd(fc), bwd(qc), bwd(ic), bwd(bc),
                  pl.BlockSpec((cfg.S, W), lambda b, h, i: (b, gc + h))],
        out_specs=pl.BlockSpec((cfg.S, W), lambda b, h, i: (b, h)),
        out_shape=jax.ShapeDtypeStruct((cfg.T, cfg.HH * HEAD), BF16),
        scratch_shapes=[pltpu.VMEM((HP, 2, HEAD, HEAD), F32),
                        pltpu.VMEM((cfg.S, W), F32), pltpu.VMEM((cfg.S, W), F32)],
        compiler_params=_params("arbitrary", "arbitrary", "arbitrary"),
        name="hgrn",
    )(lb_logits, gain3, proj, proj, proj, proj, proj, proj, proj)


def _merge_kernel(oa_ref, ob_ref, *refs, nsub):
    ga_refs, gb_refs = refs[:nsub], refs[nsub:2 * nsub]
    wa_ref, wb_ref, o_ref, wabf_ref, wbbf_ref = refs[2 * nsub:]

    @pl.when(pl.program_id(1) == 0)
    def _():
        wabf_ref[...] = wa_ref[...].astype(BF16)
        wbbf_ref[...] = wb_ref[...].astype(BF16)

    ya = _dot(oa_ref[...], wabf_ref[...])
    yb = _dot(ob_ref[...], wbbf_ref[...])
    gw = ya.shape[1] // nsub
    for u in range(nsub):
        cols = slice(u * gw, (u + 1) * gw)
        o_ref[:, cols] = (jax.nn.sigmoid(ga_refs[u][...].astype(F32)) * ya[:, cols]
                          + jax.nn.sigmoid(gb_refs[u][...].astype(F32)) * yb[:, cols]).astype(BF16)


def _merge(cfg, oa, ob, proj, wa, wb):
    offs, _ = cfg.offsets
    gw = _pick(cfg.D, 512)
    assert offs[8] % gw == 0 and offs[9] % gw == 0
    nsub = 2 if cfg.D % (2 * gw) == 0 else 1
    tn = nsub * gw
    ga0, gb0 = offs[8] // gw, offs[9] // gw
    tm = _pick(cfg.T, 1024)
    ka, kb = oa.shape[1], ob.shape[1]
    gate = lambda c0, u: pl.BlockSpec((tm, gw), lambda j, i: (i, c0 + nsub * j + u))
    return pl.pallas_call(
        functools.partial(_merge_kernel, nsub=nsub),
        grid=(cfg.D // tn, cfg.T // tm),
        in_specs=[pl.BlockSpec((tm, ka), lambda j, i: (i, 0)),
                  pl.BlockSpec((tm, kb), lambda j, i: (i, 0)),
                  *[gate(ga0, u) for u in range(nsub)],
                  *[gate(gb0, u) for u in range(nsub)],
                  pl.BlockSpec((ka, tn), lambda j, i: (0, j)),
                  pl.BlockSpec((kb, tn), lambda j, i: (0, j))],
        out_specs=pl.BlockSpec((tm, tn), lambda j, i: (i, j)),
        out_shape=jax.ShapeDtypeStruct((cfg.T, cfg.D), BF16),
        scratch_shapes=[pltpu.VMEM((ka, tn), BF16), pltpu.VMEM((kb, tn), BF16)],
        compiler_params=_params("arbitrary", "arbitrary"),
        name="merge",
    )(oa, ob, *([proj] * (2 * nsub)), wa, wb)


def _cast_kernel(w_ref, o_ref):
    o_ref[...] = w_ref[...].astype(BF16)


def _to_bf16(w):
    k, n = w.shape
    tk = _pick(k, 512)
    return pl.pallas_call(
        _cast_kernel,
        grid=(k // tk,),
        in_specs=[pl.BlockSpec((tk, n), lambda i: (i, 0))],
        out_specs=pl.BlockSpec((tk, n), lambda i: (i, 0)),
        out_shape=jax.ShapeDtypeStruct((k, n), BF16),
        compiler_params=_params("arbitrary"),
        name="wcast",
    )(w)


def _outproj_kernel(a_ref, w_ref, x_ref, mod_ref, gpost_ref, gpre_ref, wr_ref,
                    x1_ref, h_ref, aff_ref, y_ref, *, E, SUB):
    i = pl.program_id(0)

    @pl.when(i == 0)
    def _():
        y_ref[...] = jnp.zeros_like(y_ref)

    def fused(cur, prev):
        y_ref[cur] = _dot(a_ref[...], w_ref[...])
        wr = wr_ref[...]
        w_hi = wr.astype(BF16)
        w_pair = jnp.concatenate([w_hi, (wr - w_hi.astype(F32)).astype(BF16)], axis=1)
        post_gain = mod_ref[0, 2:3, :] * gpost_ref[...]
        pre_gain = gpre_ref[...] * (1.0 + mod_ref[0, 4:5, :])
        for u in range(a_ref.shape[0] // SUB):
            rows = slice(u * SUB, (u + 1) * SUB)
            x1 = x_ref[rows, :] + _rms(y_ref[prev, rows, :]) * post_gain
            x1_ref[rows, :] = x1
            h = _rms(x1) * pre_gain + mod_ref[0, 3:4, :]
            h_bf = h.astype(BF16)
            h_ref[rows, :] = h_bf
            both = _dot(h_bf, w_pair)
            logits = both[:, :LANES] + both[:, LANES:]
            lane = lax.broadcasted_iota(jnp.int32, logits.shape, 1)
            logits = jnp.where(lane < E, logits, -jnp.inf)
            p = jnp.exp(logits - jnp.max(logits, axis=-1, keepdims=True))
            aff_ref[rows, :] = p / jnp.sum(p, axis=-1, keepdims=True)

    @pl.when(i % 2 == 0)
    def _():
        fused(0, 1)

    @pl.when(i % 2 == 1)
    def _():
        fused(1, 0)


def _outproj(cfg, merged, w_bf, x2, mod3, gpost, gpre, wr_pad):
    tm = _pick(cfg.S, 512)
    sub = _pick(tm, 256)
    per_b = cfg.S // tm
    n = cfg.T // tm
    D = cfg.D
    const = lambda i: (0, 0)
    cur = lambda i: (jnp.minimum(i, n - 1), 0)
    lag = lambda i: (jnp.maximum(i - 1, 0), 0)
    return pl.pallas_call(
        functools.partial(_outproj_kernel, E=cfg.E, SUB=sub),
        grid=(n + 1,),
        in_specs=[pl.BlockSpec((tm, D), cur),
                  pl.BlockSpec((D, D), const, pipeline_mode=pl.Buffered(1)),
                  pl.BlockSpec((tm, D), lag),
                  pl.BlockSpec((1, 6, D), lambda i: (jnp.maximum(i - 1, 0) // per_b, 0, 0)),
                  pl.BlockSpec((1, D), const),
                  pl.BlockSpec((1, D), const),
                  pl.BlockSpec((D, LANES), const)],
        out_specs=[pl.BlockSpec((tm, D), lag),
                   pl.BlockSpec((tm, D), lag),
                   pl.BlockSpec((tm, LANES), lag)],
        out_shape=[jax.ShapeDtypeStruct((cfg.T, D), F32),
                   jax.ShapeDtypeStruct((cfg.T, D), BF16),
                   jax.ShapeDtypeStruct((cfg.T, LANES), F32)],
        scratch_shapes=[pltpu.VMEM((2, tm, D), F32)],
        compiler_params=_params("arbitrary"),
        name="outproj",
    )(merged, w_bf, x2, mod3, gpost, gpre, wr_pad)


def _route_kernel(aff_ref, posm_ref, postm_ref, offs_ref, a_ref, *, B, E, S, C):
    nt = S // LANES
    BE = B * E
    for b in range(B):
        for t in range(nt):
            tok = slice(b * S + t * LANES, b * S + (t + 1) * LANES)
            a_ref[b * E:(b + 1) * E, t * LANES:(t + 1) * LANES] = aff_ref[tok, :].T[:E, :]

    cap = jnp.float32(C)

    def narrow(_, carry):
        lo, hi = carry
        width = hi - lo
        a = a_ref[...]
        new_lo, new_hi = lo, hi
        found = jnp.zeros(lo.shape, jnp.bool_)
        for j in range(SEARCH_PROBES, 0, -1):
            m = lo + (j / (SEARCH_PROBES + 1)) * width
            keep = jnp.sum(jnp.where(a >= m, 1.0, 0.0), axis=1, keepdims=True) >= cap
            new_lo = jnp.where(keep & ~found, m, new_lo)
            new_hi = jnp.where(keep | found, new_hi, m)
            found = found | keep
        return new_lo, new_hi

    lo, hi = lax.fori_loop(0, SEARCH_ROUNDS, narrow,
                           (jnp.zeros((BE, 1), F32), jnp.full((BE, 1), 2.0, F32)))
    need = cap - jnp.sum(jnp.where(a_ref[...] >= hi, 1.0, 0.0), axis=1, keepdims=True)

    ui = lax.broadcasted_iota(jnp.int32, (LANES, LANES), 0)
    uj = lax.broadcasted_iota(jnp.int32, (LANES, LANES), 1)
    upper = jnp.where(ui <= uj, 1.0, 0.0).astype(BF16)
    lane = lax.broadcasted_iota(jnp.int32, (BE, LANES), 1)
    run_tie = jnp.zeros((BE, 1), F32)
    run_sel = jnp.zeros((BE, 1), F32)
    offs = jnp.zeros((BE, LANES), F32)
    pad = jnp.full((LANES - E, LANES), -1.0, F32)
    for t in range(nt):
        cols = slice(t * LANES, (t + 1) * LANES)
        a = a_ref[:, cols]
        tie = (a >= lo) & (a < hi)
        tie_t = jnp.where(tie, 1.0, 0.0)
        tie_rank = _dot(tie_t.astype(BF16), upper) - tie_t + run_tie
        sel = (a >= hi) | (tie & (tie_rank < need))
        sel_t = jnp.where(sel, 1.0, 0.0)
        pos = _dot(sel_t.astype(BF16), upper) - sel_t + run_sel
        posm = jnp.where(sel, pos, -1.0)
        posm_ref[:, cols] = posm
        for b in range(B):
            tok = slice(b * S + t * LANES, b * S + (t + 1) * LANES)
            postm_ref[tok, :] = jnp.concatenate([posm[b * E:(b + 1) * E], pad], axis=0).T
        offs = jnp.where(lane == t, run_sel, offs)
        run_tie = run_tie + jnp.sum(tie_t, axis=1, keepdims=True)
        run_sel = run_sel + jnp.sum(sel_t, axis=1, keepdims=True)
    offs = jnp.where(lane == nt, run_sel, offs)
    offs_ref[...] = offs.astype(jnp.int32)


def _route(cfg, aff_tm):
    B, E, S = cfg.B, cfg.E, cfg.S
    assert S // LANES < LANES and E % 8 == 0
    whole = lambda i: (0, 0)
    return pl.pallas_call(
        functools.partial(_route_kernel, B=B, E=E, S=S, C=cfg.C),
        grid=(1,),
        in_specs=[pl.BlockSpec((cfg.T, LANES), whole)],
        out_specs=[pl.BlockSpec((B * E, S), whole),
                   pl.BlockSpec((cfg.T, LANES), whole),
                   pl.BlockSpec((B * E, LANES), whole)],
        out_shape=[jax.ShapeDtypeStruct((B * E, S), F32),
                   jax.ShapeDtypeStruct((cfg.T, LANES), F32),
                   jax.ShapeDtypeStruct((B * E, LANES), jnp.int32)],
        scratch_shapes=[pltpu.VMEM((B * E, S), F32)],
        compiler_params=_params("arbitrary"),
        name="route",
    )(aff_tm)


def _slot_window(offs_ref, row, token_block, C):
    base = row * LANES + token_block * (COMBINE_TOKENS // LANES)
    first, end = offs_ref[base], offs_ref[base + COMBINE_TOKENS // LANES]
    start = jnp.minimum((first // BF16_ROWS) * BF16_ROWS, C - COMBINE_WINDOW)
    return pl.multiple_of(start, BF16_ROWS), end


def _extra_windows(start, end):
    return (jnp.maximum(end - start - COMBINE_WINDOW, 0) + COMBINE_WINDOW - 1) // COMBINE_WINDOW


def _gather_kernel(offs_ref, posm_ref, h_ref, x_ref, *, E, G, C):
    b, grp, kb = pl.program_id(0), pl.program_id(1), pl.program_id(2)
    W, TB = COMBINE_WINDOW, COMBINE_TOKENS

    @pl.when(kb == 0)
    def _():
        x_ref[...] = jnp.zeros_like(x_ref)

    slot0 = lax.broadcasted_iota(jnp.int32, (W, TB), 0)
    starts, pieces = [], []
    for j in range(G):
        start, _ = _slot_window(offs_ref, b * E + grp * G + j, kb, C)
        starts.append(start)
        pieces.append(jnp.where(posm_ref[j] == (start + slot0).astype(F32), 1.0, 0.0).astype(BF16))
    rows = _dot(jnp.concatenate(pieces, axis=0), h_ref[...]).astype(BF16)
    for j in range(G):
        win = pl.ds(starts[j], W)
        x_ref[j, win, :] = x_ref[j, win, :] + rows[j * W:(j + 1) * W]

    def spill(j, carry):
        start, end = _slot_window(offs_ref, b * E + grp * G + j, kb, C)

        def extra(w, c2):
            cur = start + (w + 1) * W
            s2 = pl.multiple_of(jnp.minimum(cur, C - W), BF16_ROWS)
            slot = s2 + slot0
            onehot = jnp.where((posm_ref[j] == slot.astype(F32)) & (slot >= cur), 1.0, 0.0).astype(BF16)
            win = pl.ds(s2, W)
            x_ref[j, win, :] = x_ref[j, win, :] + _dot(onehot, h_ref[...]).astype(BF16)
            return c2

        return lax.fori_loop(0, _extra_windows(start, end), extra, carry)

    lax.fori_loop(0, G, spill, 0)


def _gather(cfg, offs_flat, posm3, h2):
    B, E, S, C, D = cfg.B, cfg.E, cfg.S, cfg.C, cfg.D
    G, TB = min(E, GATHER_EXPERTS), COMBINE_TOKENS
    assert E % G == 0 and S % TB == 0 and C % COMBINE_WINDOW == 0
    nk = S // TB
    grid_spec = pltpu.PrefetchScalarGridSpec(
        num_scalar_prefetch=1,
        grid=(B, E // G, nk),
        in_specs=[pl.BlockSpec((G, 1, TB), lambda b, g, k, offs: (b * (E // G) + g, 0, k)),
                  pl.BlockSpec((TB, D), lambda b, g, k, offs: (b * nk + k, 0))],
        out_specs=pl.BlockSpec((G, C, D), lambda b, g, k, offs: (g, b, 0)),
    )
    return pl.pallas_call(
        functools.partial(_gather_kernel, E=E, G=G, C=C),
        grid_spec=grid_spec,
        out_shape=jax.ShapeDtypeStruct((E, B * C, D), BF16),
        compiler_params=_params("arbitrary", "arbitrary", "arbitrary"),
        name="gather",
    )(offs_flat, posm3, h2)


def _ffn_kernel(x_ref, wg_ref, wu_ref, wd_ref, y_ref, acc_ref):
    f = pl.program_id(1)

    @pl.when(f == 0)
    def _():
        acc_ref[...] = jnp.zeros_like(acc_ref)

    x = x_ref[0]
    a = (_silu(_dot(x, wg_ref[0].astype(BF16))) * _dot(x, wu_ref[0].astype(BF16))).astype(BF16)
    acc_ref[...] += _dot(a, wd_ref[0].astype(BF16))

    @pl.when(f == pl.num_programs(1) - 1)
    def _():
        y_ref[0] = acc_ref[...].astype(BF16)


def _ffn(cfg, xe, wg, wu, wd):
    B, E, C, D, FF = cfg.B, cfg.E, cfg.C, cfg.D, cfg.FF
    tf = _pick(FF, 256)
    return pl.pallas_call(
        _ffn_kernel,
        grid=(E, FF // tf),
        in_specs=[pl.BlockSpec((1, B * C, D), lambda e, f: (e, 0, 0)),
                  pl.BlockSpec((1, D, tf), lambda e, f: (e, 0, f)),
                  pl.BlockSpec((1, D, tf), lambda e, f: (e, 0, f)),
                  pl.BlockSpec((1, tf, D), lambda e, f: (e, f, 0))],
        out_specs=pl.BlockSpec((1, B * C, D), lambda e, f: (e, 0, 0)),
        out_shape=jax.ShapeDtypeStruct((E, B * C, D), BF16),
        scratch_shapes=[pltpu.VMEM((B * C, D), F32)],
        compiler_params=_params("arbitrary", "arbitrary"),
        name="ffn",
    )(xe, wg, wu, wd)


def _combine_kernel(offs_ref, pos_ref, aff_ref, ppos_ref, paff_ref, y_hbm, x1_ref, mod_ref, g_ref, o_ref,
                    z_ref, acc_ref, zx_ref, sem, semx, *, E, C, NK):
    b, k = pl.program_id(0), pl.program_id(1)
    W, TB = COMBINE_WINDOW, COMBINE_TOKENS
    lane = lax.broadcasted_iota(jnp.int32, (TB, LANES), 1)

    def window_copy(e, start, half):
        return pltpu.make_async_copy(y_hbm.at[e, pl.ds(b * C + start, W), :],
                                     z_ref.at[half, pl.ds(e * W, W), :], sem.at[half])

    def windows(block, half, wait):
        starts = []
        for e in range(E):
            start, _ = _slot_window(offs_ref, b * E + e, block, C)
            cp = window_copy(e, start, half)
            cp.wait() if wait else cp.start()
            starts.append(start)
        return starts

    @pl.when(k == 0)
    def _():
        acc_ref[...] = jnp.zeros_like(acc_ref)
        windows(0, 0, wait=False)

    def spill(e, carry):
        start, end = _slot_window(offs_ref, b * E + e, k - 1, C)

        def extra(w, c2):
            mine = lane == e
            pos = jnp.sum(jnp.where(mine, ppos_ref[...], 0.0), axis=1, keepdims=True)
            aff = jnp.sum(jnp.where(mine, paff_ref[...], 0.0), axis=1, keepdims=True)
            cur = start + (w + 1) * W
            s2 = pl.multiple_of(jnp.minimum(cur, C - W), BF16_ROWS)
            slot = s2 + lax.broadcasted_iota(jnp.int32, (TB, W), 1)
            m = jnp.where((pos == slot.astype(F32)) & (slot >= cur), aff, 0.0).astype(BF16)
            cp = pltpu.make_async_copy(y_hbm.at[e, pl.ds(b * C + s2, W), :], zx_ref, semx)
            cp.start()
            cp.wait()
            acc_ref[(k + 1) % 2] += _dot(m, zx_ref[...])
            return c2

        return lax.fori_loop(0, _extra_windows(start, end), extra, carry)

    @pl.when(k > 0)
    def _():
        lax.fori_loop(0, E, spill, 0)

    def fused(cur, prev):
        starts = [_slot_window(offs_ref, b * E + e, k, C)[0] for e in range(E)]
        o_ref[...] = x1_ref[...] + mod_ref[0, 5:6, :] * (_rms(acc_ref[prev]) * g_ref[...])
        seg = lane // W
        off = lane - seg * W
        group = LANES // W
        pieces = []
        for p in range(E // group):
            slot = jnp.zeros((TB, LANES), jnp.int32)
            pos = jnp.full((TB, LANES), -1.0, F32)
            aff = jnp.zeros((TB, LANES), F32)
            for q in range(group):
                e = p * group + q
                mine = seg == q
                slot = jnp.where(mine, starts[e] + off, slot)
                pos = jnp.where(mine, pos_ref[:, e:e + 1], pos)
                aff = jnp.where(mine, aff_ref[:, e:e + 1], aff)
            pieces.append(jnp.where(pos == slot.astype(F32), aff, 0.0).astype(BF16))
        m = jnp.concatenate(pieces, axis=1)
        windows(k + 1, prev, wait=False)
        windows(k, cur, wait=True)
        acc_ref[cur] = _dot(m, z_ref[cur])

        @pl.when(k == NK)
        def _():
            windows(k + 1, prev, wait=True)

    @pl.when(k % 2 == 0)
    def _():
        fused(0, 1)

    @pl.when(k % 2 == 1)
    def _():
        fused(1, 0)


def _combine(cfg, offs_flat, pos_tm, aff_tm, y, x1, mod3, g):
    B, E, S, C, D = cfg.B, cfg.E, cfg.S, cfg.C, cfg.D
    W, TB = COMBINE_WINDOW, COMBINE_TOKENS
    assert LANES % W == 0 and E % (LANES // W) == 0 and C % W == 0 and W % BF16_ROWS == 0 and S % TB == 0
    nk = S // TB
    assert (nk + 3) * (TB // LANES) <= LANES
    cur = lambda b, k, offs: (b * nk + jnp.minimum(k, nk - 1), 0)
    lag = lambda b, k, offs: (b * nk + jnp.maximum(k - 1, 0), 0)
    grid_spec = pltpu.PrefetchScalarGridSpec(
        num_scalar_prefetch=1,
        grid=(B, nk + 1),
        in_specs=[pl.BlockSpec((TB, LANES), cur),
                  pl.BlockSpec((TB, LANES), cur),
                  pl.BlockSpec((TB, LANES), lag),
                  pl.BlockSpec((TB, LANES), lag),
                  pl.BlockSpec(memory_space=pl.ANY),
                  pl.BlockSpec((TB, D), lag),
                  pl.BlockSpec((1, 6, D), lambda b, k, offs: (b, 0, 0)),
                  pl.BlockSpec((1, D), lambda b, k, offs: (0, 0))],
        out_specs=pl.BlockSpec((TB, D), lag),
        scratch_shapes=[pltpu.VMEM((2, E * W, D), BF16), pltpu.VMEM((2, TB, D), F32),
                        pltpu.VMEM((W, D), BF16),
                        pltpu.SemaphoreType.DMA((2,)), pltpu.SemaphoreType.DMA(())],
    )
    return pl.pallas_call(
        functools.partial(_combine_kernel, E=E, C=C, NK=nk),
        grid_spec=grid_spec,
        out_shape=jax.ShapeDtypeStruct((cfg.T, D), F32),
        compiler_params=_params("arbitrary", "arbitrary"),
        name="combine",
    )(offs_flat, pos_tm, aff_tm, pos_tm, aff_tm, y, x1, mod3, g)


def _layer(cfg, layer, x2, c_pad, pos2, inv, sgn, w_ada, b_ada, g_pre_mix, g_post_mix, g_pre_ffn,
           g_post_ffn, w_in, hg_lb_logits, hg_out_norm, attn_sink, w_branch_a, w_branch_b, w_out,
           w_router, w_exp_gate, w_exp_up, w_exp_down):
    D, E = cfg.D, cfg.E
    row = lambda v: v.reshape(1, -1)
    mod = _ada(c_pad, w_ada[layer], row(b_ada[layer]))[:cfg.B]
    mod3 = mod.reshape(cfg.B, 6, D)
    h = _prenorm(cfg, x2, row(g_pre_mix[layer]), mod3)
    proj = _inproj(h, w_in[layer])
    q_rot, k_rot = _rope(cfg, proj, pos2, inv, sgn)
    att = _attn(cfg, q_rot, k_rot, proj, attn_sink[layer])
    gain3 = hg_out_norm[layer].reshape(cfg.HH // HGRN_HEADS_PER_STEP, 1, HGRN_HEADS_PER_STEP * HEAD)
    oa = _hgrn(cfg, proj, hg_lb_logits, gain3, layer)
    merged = _merge(cfg, oa, att, proj, w_branch_a[layer], w_branch_b[layer])
    wr_pad = jnp.pad(w_router[layer], ((0, 0), (0, LANES - E)))
    x1, h2, aff_tm = _outproj(cfg, merged, _to_bf16(w_out[layer]), x2, mod3, row(g_post_mix[layer]),
                              row(g_pre_ffn[layer]), wr_pad)
    posm, pos_tm, offs = _route(cfg, aff_tm)
    offs_flat = offs.reshape(-1)
    xe = _gather(cfg, offs_flat, posm.reshape(cfg.B * E, 1, cfg.S), h2)
    ye = _ffn(cfg, xe, w_exp_gate[layer], w_exp_up[layer], w_exp_down[layer])
    return _combine(cfg, offs_flat, pos_tm, aff_tm, ye, x1, mod3, row(g_post_ffn[layer]))


def _block(cfg, x, c, positions, w_ada, *rest):
    depth = w_ada.shape[0]
    x2 = x.reshape(cfg.T, cfg.D)
    c_pad = jnp.pad(c, ((0, 8 - cfg.B), (0, 0)))
    pos2 = positions.reshape(cfg.T, 1)
    half = HEAD // 2
    inv_half = ROPE_THETA ** (-jnp.arange(half, dtype=F32) / half)
    inv = jnp.concatenate([inv_half, inv_half]).reshape(1, HEAD)
    sgn = jnp.concatenate([-jnp.ones((half,), F32), jnp.ones((half,), F32)]).reshape(1, HEAD)
    for layer in range(depth):
        x2 = _layer(cfg, layer, x2, c_pad, pos2, inv, sgn, w_ada, *rest)
    return x2.reshape(cfg.B, cfg.S, cfg.D)


def kernel(x, c, positions, w_ada, b_ada, g_pre_mix, g_post_mix, g_pre_ffn, g_post_ffn, w_in,
           hg_lb_logits, hg_out_norm, attn_sink, w_branch_a, w_branch_b, w_out, w_router,
           w_exp_gate, w_exp_up, w_exp_down):
    B, S, D = x.shape
    cfg = Cfg(B=B, S=S, D=D,
              HH=hg_out_norm.shape[1], AH=attn_sink.shape[1],
              KVH=(w_in.shape[2] - 5 * hg_out_norm.shape[1] * HEAD - attn_sink.shape[1] * HEAD - 2 * D) // (2 * HEAD),
              E=w_router.shape[2], FF=w_exp_gate.shape[3])
    return _block(cfg, x, c, positions, w_ada, b_ada, g_pre_mix, g_post_mix, g_pre_ffn, g_post_ffn,
                  w_in, hg_lb_logits, hg_out_norm, attn_sink, w_branch_a, w_branch_b, w_out,
                  w_router, w_exp_gate, w_exp_up, w_exp_down)
```

```python
import functools
from typing import NamedTuple

import jax
import jax.numpy as jnp
from jax import lax
from jax.experimental import pallas as pl
from jax.experimental.pallas import tpu as pltpu

F32 = jnp.float32
BF16 = jnp.bfloat16

LANES = 128
HEAD = 128
HG_CHUNK = 64
WINDOW = 128
ROPE_THETA = 10000.0
NORM_EPS = 1e-6
CAPACITY_FACTOR = 2
VMEM_LIMIT_BYTES = 56 * 1024 * 1024
SEARCH_PROBES = 3
SEARCH_ROUNDS = 76
HGRN_HEADS_PER_STEP = 4
HGRN_SCORE_ROWS = 256
COMBINE_TOKENS = 256
COMBINE_WINDOW = 64
GATHER_EXPERTS = 8
BF16_ROWS = 16


class Cfg(NamedTuple):
    B: int
    S: int
    D: int
    HH: int
    AH: int
    KVH: int
    E: int
    FF: int

    @property
    def T(self):
        return self.B * self.S

    @property
    def C(self):
        return CAPACITY_FACTOR * self.S // self.E

    @property
    def offsets(self):
        hq = self.HH * HEAD
        widths = (hq, hq, hq, hq, hq, self.AH * HEAD, self.KVH * HEAD, self.KVH * HEAD, self.D, self.D)
        offs, o = [], 0
        for w in widths:
            offs.append(o)
            o += w
        return tuple(offs), o


def _params(*sem):
    return pltpu.CompilerParams(dimension_semantics=sem, vmem_limit_bytes=VMEM_LIMIT_BYTES)


def _pick(n, target):
    if n <= target:
        return n
    t = (target // LANES) * LANES
    while t > LANES and n % t:
        t -= LANES
    return t


def _dot(a, b):
    return jnp.dot(a, b, preferred_element_type=F32)


def _dot_nt(a, b):
    return lax.dot_general(a, b, (((1,), (1,)), ((), ())), preferred_element_type=F32)


def _dot_tn(a, b):
    return lax.dot_general(a, b, (((0,), (0,)), ((), ())), preferred_element_type=F32)


def _silu(x):
    return x * jax.nn.sigmoid(x)


def _rms(x):
    return x * lax.rsqrt(jnp.mean(x * x, axis=-1, keepdims=True) + NORM_EPS)


def _ada_kernel(c_ref, w_ref, b_ref, o_ref):
    a = _silu(c_ref[...]).astype(BF16)
    o_ref[...] = _dot(a, w_ref[...].astype(BF16)) + b_ref[...]


def _ada(c_pad, w, b):
    rows, d = c_pad.shape
    n = w.shape[1]
    tn = _pick(n, 1024)
    return pl.pallas_call(
        _ada_kernel,
        grid=(n // tn,),
        in_specs=[pl.BlockSpec((rows, d), lambda j: (0, 0)),
                  pl.BlockSpec((d, tn), lambda j: (0, j)),
                  pl.BlockSpec((1, tn), lambda j: (0, j))],
        out_specs=pl.BlockSpec((rows, tn), lambda j: (0, j)),
        out_shape=jax.ShapeDtypeStruct((rows, n), F32),
        compiler_params=_params("arbitrary"),
        name="ada",
    )(c_pad, w, b)


def _prenorm_kernel(x_ref, g_ref, mod_ref, o_ref):
    y = _rms(x_ref[...]) * g_ref[...]
    shift = mod_ref[0, 0:1, :]
    scale = mod_ref[0, 1:2, :]
    o_ref[...] = (y * (1.0 + scale) + shift).astype(BF16)


def _prenorm(cfg, x2, g, mod3):
    tm = _pick(cfg.S, 1024)
    per_b = cfg.S // tm
    return pl.pallas_call(
        _prenorm_kernel,
        grid=(cfg.T // tm,),
        in_specs=[pl.BlockSpec((tm, cfg.D), lambda i: (i, 0)),
                  pl.BlockSpec((1, cfg.D), lambda i: (0, 0)),
                  pl.BlockSpec((1, 6, cfg.D), lambda i: (i // per_b, 0, 0))],
        out_specs=pl.BlockSpec((tm, cfg.D), lambda i: (i, 0)),
        out_shape=jax.ShapeDtypeStruct((cfg.T, cfg.D), BF16),
        compiler_params=_params("arbitrary"),
        name="prenorm",
    )(x2, g, mod3)


def _inproj_kernel(a_ref, w_ref, o_ref, wbf_ref):
    @pl.when(pl.program_id(1) == 0)
    def _():
        wbf_ref[...] = w_ref[...].astype(BF16)

    o_ref[...] = _dot(a_ref[...], wbf_ref[...]).astype(o_ref.dtype)


def _inproj(a, w):
    m, k = a.shape
    n = w.shape[1]
    tn = _pick(n, 1792)
    tm = _pick(m, 1024)
    return pl.pallas_call(
        _inproj_kernel,
        grid=(n // tn, m // tm),
        in_specs=[pl.BlockSpec((tm, k), lambda j, i: (i, 0)),
                  pl.BlockSpec((k, tn), lambda j, i: (0, j))],
        out_specs=pl.BlockSpec((tm, tn), lambda j, i: (i, j)),
        out_shape=jax.ShapeDtypeStruct((m, n), BF16),
        scratch_shapes=[pltpu.VMEM((k, tn), BF16)],
        compiler_params=_params("arbitrary", "arbitrary"),
        name="inproj",
    )(a, w)


def _rope_kernel(q_ref, k_ref, pos_ref, inv_ref, sgn_ref, qo_ref, ko_ref, *, q_scale):
    ang = pos_ref[...].astype(F32) * inv_ref[...]
    cos = jnp.cos(ang)
    sin = jnp.sin(ang) * sgn_ref[...]

    def rotate(x_ref, o_ref, scale):
        for h in range(x_ref.shape[1] // HEAD):
            cols = slice(h * HEAD, (h + 1) * HEAD)
            x = x_ref[:, cols].astype(F32)
            r = x * cos + pltpu.roll(x, HEAD // 2, 1) * sin
            o_ref[:, cols] = (r if scale is None else r * scale).astype(BF16)

    rotate(q_ref, qo_ref, q_scale)
    rotate(k_ref, ko_ref, None)


def _rope(cfg, proj, pos2, inv, sgn):
    offs, _ = cfg.offsets
    wq, wk = cfg.AH * HEAD, cfg.KVH * HEAD
    assert offs[5] % wq == 0 and offs[6] % wk == 0
    tm = _pick(cfg.T, 512)
    return pl.pallas_call(
        functools.partial(_rope_kernel, q_scale=HEAD ** -0.5),
        grid=(cfg.T // tm,),
        in_specs=[pl.BlockSpec((tm, wq), lambda i: (i, offs[5] // wq)),
                  pl.BlockSpec((tm, wk), lambda i: (i, offs[6] // wk)),
                  pl.BlockSpec((tm, 1), lambda i: (i, 0)),
                  pl.BlockSpec((1, HEAD), lambda i: (0, 0)),
                  pl.BlockSpec((1, HEAD), lambda i: (0, 0))],
        out_specs=[pl.BlockSpec((tm, wq), lambda i: (i, 0)),
                   pl.BlockSpec((tm, wk), lambda i: (i, 0))],
        out_shape=[jax.ShapeDtypeStruct((cfg.T, wq), BF16),
                   jax.ShapeDtypeStruct((cfg.T, wk), BF16)],
        compiler_params=_params("arbitrary"),
        name="rope",
    )(proj, proj, pos2, inv, sgn)


def _attn_kernel(sink_ref, q_ref, kp_ref, kc_ref, kn_ref, vp_ref, vc_ref, vn_ref, o_ref, *, G, KVH, NB):
    n = pl.program_id(1)
    blk = WINDOW
    row = lax.broadcasted_iota(jnp.int32, (G * blk, 3 * blk), 0)
    qi = row - (row // blk) * blk
    kj = lax.broadcasted_iota(jnp.int32, (G * blk, 3 * blk), 1)
    valid = jnp.abs(kj - blk - qi) <= WINDOW
    valid = valid & ((kj >= blk) | (n > 0)) & ((kj < 2 * blk) | (n < NB - 1))
    head = lax.broadcasted_iota(jnp.int32, (G * blk, 1), 0) // blk
    for kvh in range(KVH):
        kv = slice(kvh * HEAD, (kvh + 1) * HEAD)
        k = jnp.concatenate([kp_ref[:, kv], kc_ref[:, kv], kn_ref[:, kv]], axis=0)
        v = jnp.concatenate([vp_ref[:, kv], vc_ref[:, kv], vn_ref[:, kv]], axis=0).astype(BF16)
        heads = [kvh * G + g for g in range(G)]
        q = jnp.concatenate([q_ref[:, h * HEAD:(h + 1) * HEAD] for h in heads], axis=0)
        sk = jnp.zeros((G * blk, 1), F32)
        for g, h in enumerate(heads):
            sk = jnp.where(head == g, sink_ref[h], sk)
        s = jnp.where(valid, _dot_nt(q, k), -jnp.inf)
        m = jnp.maximum(jnp.max(s, axis=-1, keepdims=True), sk)
        p = jnp.exp(s - m)
        denom = jnp.sum(p, axis=-1, keepdims=True) + jnp.exp(sk - m)
        o = (_dot(p.astype(BF16), v) / denom).astype(BF16)
        for g, h in enumerate(heads):
            o_ref[:, h * HEAD:(h + 1) * HEAD] = o[g * blk:(g + 1) * blk]


def _attn(cfg, q_rot, k_rot, proj, sink):
    offs, _ = cfg.offsets
    G = cfg.AH // cfg.KVH
    blk = WINDOW
    NB = cfg.S // blk
    wq, wkv = cfg.AH * HEAD, cfg.KVH * HEAD
    assert offs[7] % wkv == 0
    vcol = offs[7] // wkv
    prev = lambda n: jnp.maximum(n - 1, 0)
    nxt = lambda n: jnp.minimum(n + 1, NB - 1)
    return pl.pallas_call(
        functools.partial(_attn_kernel, G=G, KVH=cfg.KVH, NB=NB),
        grid=(cfg.B, NB),
        in_specs=[pl.BlockSpec(memory_space=pltpu.SMEM),
                  pl.BlockSpec((blk, wq), lambda b, n: (b * NB + n, 0)),
                  pl.BlockSpec((blk, wkv), lambda b, n: (b * NB + prev(n), 0)),
                  pl.BlockSpec((blk, wkv), lambda b, n: (b * NB + n, 0)),
                  pl.BlockSpec((blk, wkv), lambda b, n: (b * NB + nxt(n), 0)),
                  pl.BlockSpec((blk, wkv), lambda b, n: (b * NB + prev(n), vcol)),
                  pl.BlockSpec((blk, wkv), lambda b, n: (b * NB + n, vcol)),
                  pl.BlockSpec((blk, wkv), lambda b, n: (b * NB + nxt(n), vcol))],
        out_specs=pl.BlockSpec((blk, wq), lambda b, n: (b * NB + n, 0)),
        out_shape=jax.ShapeDtypeStruct((cfg.T, wq), BF16),
        compiler_params=_params("arbitrary", "arbitrary"),
        name="attn",
    )(sink, q_rot, k_rot, k_rot, k_rot, proj, proj, proj)


def _lower_bound(lbl_ref, d, layer):
    n = lbl_ref.shape[1]
    rows = [lbl_ref[d, j:j + 1, :] for j in range(n)]
    m = functools.reduce(jnp.maximum, rows)
    es = [jnp.exp(r - m) for r in rows]
    return sum(es[:layer + 1]) / sum(es)


def _hgrn_cumsum(logfs, tri2):
    P = 2 * HG_CHUNK
    npair = logfs[0].shape[0] // P
    cols = []
    for lf in logfs:
        hi = lf.astype(BF16)
        lo = (lf - hi.astype(F32)).astype(BF16)
        for p in range(npair):
            cols += [hi[p * P:(p + 1) * P], lo[p * P:(p + 1) * P]]
    both = _dot(tri2.astype(BF16), jnp.concatenate(cols, axis=1))
    outs = []
    for s in range(len(logfs)):
        parts = []
        for p in range(npair):
            c0 = (s * npair + p) * 2 * HEAD
            parts.append(both[:, c0:c0 + HEAD] + both[:, c0 + HEAD:c0 + 2 * HEAD])
        outs.append(jnp.concatenate(parts, axis=0))
    return outs


def _hgrn_tile(qs, k, b, v, st, mask, fwd):
    L = HG_CHUNK
    nc = qs.shape[0] // L
    ref_i, last_i = (L // 2, L - 1) if fwd else (L - 1 - L // 2, 0)

    def per_chunk(rows):
        return jnp.concatenate([jnp.broadcast_to(r, (L, HEAD)) for r in rows], axis=0)

    brefs = [b[c * L + ref_i:c * L + ref_i + 1, :] for c in range(nc)]
    blasts = [b[c * L + last_i:c * L + last_i + 1, :] for c in range(nc)]
    d = b - per_chunk(brefs)
    q_in = qs * jnp.exp(d)
    k_in = k * jnp.exp(-d)
    q_bf, k_bf, v_bf = q_in.astype(BF16), k_in.astype(BF16), v.astype(BF16)
    nb = mask.shape[0]
    o_intra = []
    for r0 in range(0, qs.shape[0], nb):
        r = slice(r0, r0 + nb)
        a = jnp.where(mask, _dot_nt(q_bf[r], k_bf[r]), 0.0)
        o_intra.append(_dot(a.astype(BF16), v_bf[r]))
    o_intra = jnp.concatenate(o_intra, axis=0)
    q_dec = (q_in * per_chunk([jnp.exp(r) for r in brefs])).astype(BF16)
    k_dec = k_in * per_chunk([jnp.exp(l - r) for l, r in zip(blasts, brefs)])

    zero = jnp.zeros((L, HEAD), F32)
    cs = []
    for p in range(nc // 2):
        r = slice(2 * p * L, 2 * (p + 1) * L)
        kd = k_dec[r]
        k_blk = jnp.concatenate([jnp.concatenate([kd[:L], zero], axis=1),
                                 jnp.concatenate([zero, kd[L:]], axis=1)], axis=0).astype(BF16)
        both = _dot_tn(v_bf[r], k_blk)
        cs += [both[:, :HEAD], both[:, HEAD:]]
    entering = [None] * nc
    for c in (range(nc) if fwd else reversed(range(nc))):
        entering[c] = st
        st = st * jnp.exp(blasts[c]) + cs[c]
    o_inter = []
    for p in range(nc // 2):
        r = slice(2 * p * L, 2 * (p + 1) * L)
        st2 = jnp.concatenate([entering[2 * p], entering[2 * p + 1]], axis=0).astype(BF16)
        both = _dot_nt(q_dec[r], st2)
        o_inter += [both[:L, :HEAD], both[L:, HEAD:]]
    return o_intra + jnp.concatenate(o_inter, axis=0), st


def _hgrn_kernel(lbl_ref, gain_ref, qf_ref, vf_ref, zf_ref, qb_ref, vb_ref, zb_ref, g_ref, o_ref,
                 st_ref, of_ref, ob_ref, *, NT, R, HP, layer):
    i = pl.program_id(2)

    def chunk_masks(n):
        li = lax.broadcasted_iota(jnp.int32, (n, n), 0)
        mi = lax.broadcasted_iota(jnp.int32, (n, n), 1)
        same = (li // HG_CHUNK) == (mi // HG_CHUNK)
        return same & (mi <= li), same & (mi >= li)

    tri_f, tri_b = (jnp.where(m, 1.0, 0.0).astype(F32) for m in chunk_masks(2 * HG_CHUNK))
    mask_f, mask_b = chunk_masks(min(R, HGRN_SCORE_ROWS))

    @pl.when(i == 0)
    def _():
        st_ref[...] = jnp.zeros_like(st_ref)

    lanes = [slice(p * HEAD, (p + 1) * HEAD) for p in range(HP)]
    for d, (q_ref, v_ref, z_ref) in enumerate(((qf_ref, vf_ref, zf_ref), (qb_ref, vb_ref, zb_ref))):
        fwd = d == 0
        lb = _lower_bound(lbl_ref, d, layer)
        f = lb + (1.0 - lb) * jax.nn.sigmoid(z_ref[...].astype(F32))
        k = 1.0 - f
        qs = _silu(q_ref[...].astype(F32))
        cums = _hgrn_cumsum([jnp.log(f)[:, lanes[p]] for p in range(HP)], tri_f if fwd else tri_b)
        tile = i if fwd else NT - 1 - i
        rows = pl.ds(pl.multiple_of(tile * R, R), R)
        for p in range(HP):
            o, st = _hgrn_tile(qs[:, lanes[p]], k[:, lanes[p]], cums[p], v_ref[:, lanes[p]], st_ref[p, d],
                               mask_f if fwd else mask_b, fwd)
            (of_ref if fwd else ob_ref)[rows, lanes[p]] = o
            st_ref[p, d] = st

    @pl.when(i == NT - 1)
    def _():
        for t in range(NT):
            rows = slice(t * R, (t + 1) * R)
            for p in range(HP):
                o = of_ref[rows, lanes[p]] + ob_ref[rows, lanes[p]]
                gate = _silu(g_ref[rows, lanes[p]].astype(F32))
                o_ref[rows, lanes[p]] = (_rms(o) * gain_ref[0, :, lanes[p]] * gate).astype(BF16)


def _hgrn(cfg, proj, lb_logits, gain3, layer):
    offs, _ = cfg.offsets
    HP = HGRN_HEADS_PER_STEP
    W = HP * HEAD
    assert cfg.HH % HP == 0 and all(o % W == 0 for o in offs[:5])
    qc, fc, bc, ic, gc = (o // W for o in offs[:5])
    R = _pick(cfg.S, 512)
    NT = cfg.S // R
    nl = lb_logits.shape[1]
    fwd = lambda col: pl.BlockSpec((R, W), lambda b, h, i: (b * NT + i, col + h))
    bwd = lambda col: pl.BlockSpec((R, W), lambda b, h, i: (b * NT + NT - 1 - i, col + h))
    return pl.pallas_call(
        functools.partial(_hgrn_kernel, NT=NT, R=R, HP=HP, layer=layer),
        grid=(cfg.B, cfg.HH // HP, NT),
        in_specs=[pl.BlockSpec((2, nl, W), lambda b, h, i: (0, 0, h)),
                  pl.BlockSpec((1, 1, W), lambda b, h, i: (h, 0, 0)),
                  fwd(qc), fwd(ic), fwd(fc), bwd(qc), bwd(ic), bwd(bc),
                  pl.BlockSpec((cfg.S, W), lambda b, h, i: (b, gc + h))],
        out_specs=pl.BlockSpec((cfg.S, W), lambda b, h, i: (b, h)),
        out_shape=jax.ShapeDtypeStruct((cfg.T, cfg.HH * HEAD), BF16),
        scratch_shapes=[pltpu.VMEM((HP, 2, HEAD, HEAD), F32),
                        pltpu.VMEM((cfg.S, W), F32), pltpu.VMEM((cfg.S, W), F32)],
        compiler_params=_params("arbitrary", "arbitrary", "arbitrary"),
        name="hgrn",
    )(lb_logits, gain3, proj, proj, proj, proj, proj, proj, proj)


def _merge_kernel(oa_ref, ob_ref, *refs, nsub):
    ga_refs, gb_refs = refs[:nsub], refs[nsub:2 * nsub]
    wa_ref, wb_ref, o_ref, wabf_ref, wbbf_ref = refs[2 * nsub:]

    @pl.when(pl.program_id(1) == 0)
    def _():
        wabf_ref[...] = wa_ref[...].astype(BF16)
        wbbf_ref[...] = wb_ref[...].astype(BF16)

    ya = _dot(oa_ref[...], wabf_ref[...])
    yb = _dot(ob_ref[...], wbbf_ref[...])
    gw = ya.shape[1] // nsub
    for u in range(nsub):
        cols = slice(u * gw, (u + 1) * gw)
        o_ref[:, cols] = (jax.nn.sigmoid(ga_refs[u][...].astype(F32)) * ya[:, cols]
                          + jax.nn.sigmoid(gb_refs[u][...].astype(F32)) * yb[:, cols]).astype(BF16)


def _merge(cfg, oa, ob, proj, wa, wb):
    offs, _ = cfg.offsets
    gw = _pick(cfg.D, 512)
    assert offs[8] % gw == 0 and offs[9] % gw == 0
    nsub = 2 if cfg.D % (2 * gw) == 0 else 1
    tn = nsub * gw
    ga0, gb0 = offs[8] // gw, offs[9] // gw
    tm = _pick(cfg.T, 1024)
    ka, kb = oa.shape[1], ob.shape[1]
    gate = lambda c0, u: pl.BlockSpec((tm, gw), lambda j, i: (i, c0 + nsub * j + u))
    return pl.pallas_call(
        functools.partial(_merge_kernel, nsub=nsub),
        grid=(cfg.D // tn, cfg.T // tm),
        in_specs=[pl.BlockSpec((tm, ka), lambda j, i: (i, 0)),
                  pl.BlockSpec((tm, kb), lambda j, i: (i, 0)),
                  *[gate(ga0, u) for u in range(nsub)],
                  *[gate(gb0, u) for u in range(nsub)],
                  pl.BlockSpec((ka, tn), lambda j, i: (0, j)),
                  pl.BlockSpec((kb, tn), lambda j, i: (0, j))],
        out_specs=pl.BlockSpec((tm, tn), lambda j, i: (i, j)),
        out_shape=jax.ShapeDtypeStruct((cfg.T, cfg.D), BF16),
        scratch_shapes=[pltpu.VMEM((ka, tn), BF16), pltpu.VMEM((kb, tn), BF16)],
        compiler_params=_params("arbitrary", "arbitrary"),
        name="merge",
    )(oa, ob, *([proj] * (2 * nsub)), wa, wb)


def _cast_kernel(w_ref, o_ref):
    o_ref[...] = w_ref[...].astype(BF16)


def _to_bf16(w):
    k, n = w.shape
    tk = _pick(k, 512)
    return pl.pallas_call(
        _cast_kernel,
        grid=(k // tk,),
        in_specs=[pl.BlockSpec((tk, n), lambda i: (i, 0))],
        out_specs=pl.BlockSpec((tk, n), lambda i: (i, 0)),
        out_shape=jax.ShapeDtypeStruct((k, n), BF16),
        compiler_params=_params("arbitrary"),
        name="wcast",
    )(w)


def _outproj_kernel(a_ref, w_ref, x_ref, mod_ref, gpost_ref, gpre_ref, wr_ref,
                    x1_ref, h_ref, aff_ref, y_ref, *, E, SUB):
    i = pl.program_id(0)

    @pl.when(i == 0)
    def _():
        y_ref[...] = jnp.zeros_like(y_ref)

    def fused(cur, prev):
        y_ref[cur] = _dot(a_ref[...], w_ref[...])
        wr = wr_ref[...]
        w_hi = wr.astype(BF16)
        w_pair = jnp.concatenate([w_hi, (wr - w_hi.astype(F32)).astype(BF16)], axis=1)
        post_gain = mod_ref[0, 2:3, :] * gpost_ref[...]
        pre_gain = gpre_ref[...] * (1.0 + mod_ref[0, 4:5, :])
        for u in range(a_ref.shape[0] // SUB):
            rows = slice(u * SUB, (u + 1) * SUB)
            x1 = x_ref[rows, :] + _rms(y_ref[prev, rows, :]) * post_gain
            x1_ref[rows, :] = x1
            h = _rms(x1) * pre_gain + mod_ref[0, 3:4, :]
            h_bf = h.astype(BF16)
            h_ref[rows, :] = h_bf
            both = _dot(h_bf, w_pair)
            logits = both[:, :LANES] + both[:, LANES:]
            lane = lax.broadcasted_iota(jnp.int32, logits.shape, 1)
            logits = jnp.where(lane < E, logits, -jnp.inf)
            p = jnp.exp(logits - jnp.max(logits, axis=-1, keepdims=True))
            aff_ref[rows, :] = p / jnp.sum(p, axis=-1, keepdims=True)

    @pl.when(i % 2 == 0)
    def _():
        fused(0, 1)

    @pl.when(i % 2 == 1)
    def _():
        fused(1, 0)


def _outproj(cfg, merged, w_bf, x2, mod3, gpost, gpre, wr_pad):
    tm = _pick(cfg.S, 512)
    sub = _pick(tm, 256)
    per_b = cfg.S // tm
    n = cfg.T // tm
    D = cfg.D
    const = lambda i: (0, 0)
    cur = lambda i: (jnp.minimum(i, n - 1), 0)
    lag = lambda i: (jnp.maximum(i - 1, 0), 0)
    return pl.pallas_call(
        functools.partial(_outproj_kernel, E=cfg.E, SUB=sub),
        grid=(n + 1,),
        in_specs=[pl.BlockSpec((tm, D), cur),
                  pl.BlockSpec((D, D), const, pipeline_mode=pl.Buffered(1)),
                  pl.BlockSpec((tm, D), lag),
                  pl.BlockSpec((1, 6, D), lambda i: (jnp.maximum(i - 1, 0) // per_b, 0, 0)),
                  pl.BlockSpec((1, D), const),
                  pl.BlockSpec((1, D), const),
                  pl.BlockSpec((D, LANES), const)],
        out_specs=[pl.BlockSpec((tm, D), lag),
                   pl.BlockSpec((tm, D), lag),
                   pl.BlockSpec((tm, LANES), lag)],
        out_shape=[jax.ShapeDtypeStruct((cfg.T, D), F32),
                   jax.ShapeDtypeStruct((cfg.T, D), BF16),
                   jax.ShapeDtypeStruct((cfg.T, LANES), F32)],
        scratch_shapes=[pltpu.VMEM((2, tm, D), F32)],
        compiler_params=_params("arbitrary"),
        name="outproj",
    )(merged, w_bf, x2, mod3, gpost, gpre, wr_pad)


def _route_kernel(aff_ref, posm_ref, postm_ref, offs_ref, a_ref, *, B, E, S, C):
    nt = S // LANES
    BE = B * E
    for b in range(B):
        for t in range(nt):
            tok = slice(b * S + t * LANES, b * S + (t + 1) * LANES)
            a_ref[b * E:(b + 1) * E, t * LANES:(t + 1) * LANES] = aff_ref[tok, :].T[:E, :]

    cap = jnp.float32(C)

    def narrow(_, carry):
        lo, hi = carry
        width = hi - lo
        a = a_ref[...]
        new_lo, new_hi = lo, hi
        found = jnp.zeros(lo.shape, jnp.bool_)
        for j in range(SEARCH_PROBES, 0, -1):
            m = lo + (j / (SEARCH_PROBES + 1)) * width
            keep = jnp.sum(jnp.where(a >= m, 1.0, 0.0), axis=1, keepdims=True) >= cap
            new_lo = jnp.where(keep & ~found, m, new_lo)
            new_hi = jnp.where(keep | found, new_hi, m)
            found = found | keep
        return new_lo, new_hi

    lo, hi = lax.fori_loop(0, SEARCH_ROUNDS, narrow,
                           (jnp.zeros((BE, 1), F32), jnp.full((BE, 1), 2.0, F32)))
    need = cap - jnp.sum(jnp.where(a_ref[...] >= hi, 1.0, 0.0), axis=1, keepdims=True)

    ui = lax.broadcasted_iota(jnp.int32, (LANES, LANES), 0)
    uj = lax.broadcasted_iota(jnp.int32, (LANES, LANES), 1)
    upper = jnp.where(ui <= uj, 1.0, 0.0).astype(BF16)
    lane = lax.broadcasted_iota(jnp.int32, (BE, LANES), 1)
    run_tie = jnp.zeros((BE, 1), F32)
    run_sel = jnp.zeros((BE, 1), F32)
    offs = jnp.zeros((BE, LANES), F32)
    pad = jnp.full((LANES - E, LANES), -1.0, F32)
    for t in range(nt):
        cols = slice(t * LANES, (t + 1) * LANES)
        a = a_ref[:, cols]
        tie = (a >= lo) & (a < hi)
        tie_t = jnp.where(tie, 1.0, 0.0)
        tie_rank = _dot(tie_t.astype(BF16), upper) - tie_t + run_tie
        sel = (a >= hi) | (tie & (tie_rank < need))
        sel_t = jnp.where(sel, 1.0, 0.0)
        pos = _dot(sel_t.astype(BF16), upper) - sel_t + run_sel
        posm = jnp.where(sel, pos, -1.0)
        posm_ref[:, cols] = posm
        for b in range(B):
            tok = slice(b * S + t * LANES, b * S + (t + 1) * LANES)
            postm_ref[tok, :] = jnp.concatenate([posm[b * E:(b + 1) * E], pad], axis=0).T
        offs = jnp.where(lane == t, run_sel, offs)
        run_tie = run_tie + jnp.sum(tie_t, axis=1, keepdims=True)
        run_sel = run_sel + jnp.sum(sel_t, axis=1, keepdims=True)
    offs = jnp.where(lane == nt, run_sel, offs)
    offs_ref[...] = offs.astype(jnp.int32)


def _route(cfg, aff_tm):
    B, E, S = cfg.B, cfg.E, cfg.S
    assert S // LANES < LANES and E % 8 == 0
    whole = lambda i: (0, 0)
    return pl.pallas_call(
        functools.partial(_route_kernel, B=B, E=E, S=S, C=cfg.C),
        grid=(1,),
        in_specs=[pl.BlockSpec((cfg.T, LANES), whole)],
        out_specs=[pl.BlockSpec((B * E, S), whole),
                   pl.BlockSpec((cfg.T, LANES), whole),
                   pl.BlockSpec((B * E, LANES), whole)],
        out_shape=[jax.ShapeDtypeStruct((B * E, S), F32),
                   jax.ShapeDtypeStruct((cfg.T, LANES), F32),
                   jax.ShapeDtypeStruct((B * E, LANES), jnp.int32)],
        scratch_shapes=[pltpu.VMEM((B * E, S), F32)],
        compiler_params=_params("arbitrary"),
        name="route",
    )(aff_tm)


def _slot_window(offs_ref, row, token_block, C):
    base = row * LANES + token_block * (COMBINE_TOKENS // LANES)
    first, end = offs_ref[base], offs_ref[base + COMBINE_TOKENS // LANES]
    start = jnp.minimum((first // BF16_ROWS) * BF16_ROWS, C - COMBINE_WINDOW)
    return pl.multiple_of(start, BF16_ROWS), end


def _extra_windows(start, end):
    return (jnp.maximum(end - start - COMBINE_WINDOW, 0) + COMBINE_WINDOW - 1) // COMBINE_WINDOW


def _gather_kernel(offs_ref, posm_ref, h_ref, x_ref, *, E, G, C):
    b, grp, kb = pl.program_id(0), pl.program_id(1), pl.program_id(2)
    W, TB = COMBINE_WINDOW, COMBINE_TOKENS

    @pl.when(kb == 0)
    def _():
        x_ref[...] = jnp.zeros_like(x_ref)

    slot0 = lax.broadcasted_iota(jnp.int32, (W, TB), 0)
    starts, pieces = [], []
    for j in range(G):
        start, _ = _slot_window(offs_ref, b * E + grp * G + j, kb, C)
        starts.append(start)
        pieces.append(jnp.where(posm_ref[j] == (start + slot0).astype(F32), 1.0, 0.0).astype(BF16))
    rows = _dot(jnp.concatenate(pieces, axis=0), h_ref[...]).astype(BF16)
    for j in range(G):
        win = pl.ds(starts[j], W)
        x_ref[j, win, :] = x_ref[j, win, :] + rows[j * W:(j + 1) * W]

    def spill(j, carry):
        start, end = _slot_window(offs_ref, b * E + grp * G + j, kb, C)

        def extra(w, c2):
            cur = start + (w + 1) * W
            s2 = pl.multiple_of(jnp.minimum(cur, C - W), BF16_ROWS)
            slot = s2 + slot0
            onehot = jnp.where((posm_ref[j] == slot.astype(F32)) & (slot >= cur), 1.0, 0.0).astype(BF16)
            win = pl.ds(s2, W)
            x_ref[j, win, :] = x_ref[j, win, :] + _dot(onehot, h_ref[...]).astype(BF16)
            return c2

        return lax.fori_loop(0, _extra_windows(start, end), extra, carry)

    lax.fori_loop(0, G, spill, 0)


def _gather(cfg, offs_flat, posm3, h2):
    B, E, S, C, D = cfg.B, cfg.E, cfg.S, cfg.C, cfg.D
    G, TB = min(E, GATHER_EXPERTS), COMBINE_TOKENS
    assert E % G == 0 and S % TB == 0 and C % COMBINE_WINDOW == 0
    nk = S // TB
    grid_spec = pltpu.PrefetchScalarGridSpec(
        num_scalar_prefetch=1,
        grid=(B, E // G, nk),
        in_specs=[pl.BlockSpec((G, 1, TB), lambda b, g, k, offs: (b * (E // G) + g, 0, k)),
                  pl.BlockSpec((TB, D), lambda b, g, k, offs: (b * nk + k, 0))],
        out_specs=pl.BlockSpec((G, C, D), lambda b, g, k, offs: (g, b, 0)),
    )
    return pl.pallas_call(
        functools.partial(_gather_kernel, E=E, G=G, C=C),
        grid_spec=grid_spec,
        out_shape=jax.ShapeDtypeStruct((E, B * C, D), BF16),
        compiler_params=_params("arbitrary", "arbitrary", "arbitrary"),
        name="gather",
    )(offs_flat, posm3, h2)


def _ffn_kernel(x_ref, wg_ref, wu_ref, wd_ref, y_ref, acc_ref):
    f = pl.program_id(1)

    @pl.when(f == 0)
    def _():
        acc_ref[...] = jnp.zeros_like(acc_ref)

    x = x_ref[0]
    a = (_silu(_dot(x, wg_ref[0].astype(BF16))) * _dot(x, wu_ref[0].astype(BF16))).astype(BF16)
    acc_ref[...] += _dot(a, wd_ref[0].astype(BF16))

    @pl.when(f == pl.num_programs(1) - 1)
    def _():
        y_ref[0] = acc_ref[...].astype(BF16)


def _ffn(cfg, xe, wg, wu, wd):
    B, E, C, D, FF = cfg.B, cfg.E, cfg.C, cfg.D, cfg.FF
    tf = _pick(FF, 256)
    return pl.pallas_call(
        _ffn_kernel,
        grid=(E, FF // tf),
        in_specs=[pl.BlockSpec((1, B * C, D), lambda e, f: (e, 0, 0)),
                  pl.BlockSpec((1, D, tf), lambda e, f: (e, 0, f)),
                  pl.BlockSpec((1, D, tf), lambda e, f: (e, 0, f)),
                  pl.BlockSpec((1, tf, D), lambda e, f: (e, f, 0))],
        out_specs=pl.BlockSpec((1, B * C, D), lambda e, f: (e, 0, 0)),
        out_shape=jax.ShapeDtypeStruct((E, B * C, D), BF16),
        scratch_shapes=[pltpu.VMEM((B * C, D), F32)],
        compiler_params=_params("arbitrary", "arbitrary"),
        name="ffn",
    )(xe, wg, wu, wd)


def _combine_kernel(offs_ref, pos_ref, aff_ref, ppos_ref, paff_ref, y_hbm, x1_ref, mod_ref, g_ref, o_ref,
                    z_ref, acc_ref, zx_ref, sem, semx, *, E, C, NK):
    b, k = pl.program_id(0), pl.program_id(1)
    W, TB = COMBINE_WINDOW, COMBINE_TOKENS
    lane = lax.broadcasted_iota(jnp.int32, (TB, LANES), 1)

    def window_copy(e, start, half):
        return pltpu.make_async_copy(y_hbm.at[e, pl.ds(b * C + start, W), :],
                                     z_ref.at[half, pl.ds(e * W, W), :], sem.at[half])

    def windows(block, half, wait):
        starts = []
        for e in range(E):
            start, _ = _slot_window(offs_ref, b * E + e, block, C)
            cp = window_copy(e, start, half)
            cp.wait() if wait else cp.start()
            starts.append(start)
        return starts

    @pl.when(k == 0)
    def _():
        acc_ref[...] = jnp.zeros_like(acc_ref)
        windows(0, 0, wait=False)

    def spill(e, carry):
        start, end = _slot_window(offs_ref, b * E + e, k - 1, C)

        def extra(w, c2):
            mine = lane == e
            pos = jnp.sum(jnp.where(mine, ppos_ref[...], 0.0), axis=1, keepdims=True)
            aff = jnp.sum(jnp.where(mine, paff_ref[...], 0.0), axis=1, keepdims=True)
            cur = start + (w + 1) * W
            s2 = pl.multiple_of(jnp.minimum(cur, C - W), BF16_ROWS)
            slot = s2 + lax.broadcasted_iota(jnp.int32, (TB, W), 1)
            m = jnp.where((pos == slot.astype(F32)) & (slot >= cur), aff, 0.0).astype(BF16)
            cp = pltpu.make_async_copy(y_hbm.at[e, pl.ds(b * C + s2, W), :], zx_ref, semx)
            cp.start()
            cp.wait()
            acc_ref[(k + 1) % 2] += _dot(m, zx_ref[...])
            return c2

        return lax.fori_loop(0, _extra_windows(start, end), extra, carry)

    @pl.when(k > 0)
    def _():
        lax.fori_loop(0, E, spill, 0)

    def fused(cur, prev):
        starts = [_slot_window(offs_ref, b * E + e, k, C)[0] for e in range(E)]
        o_ref[...] = x1_ref[...] + mod_ref[0, 5:6, :] * (_rms(acc_ref[prev]) * g_ref[...])
        seg = lane // W
        off = lane - seg * W
        group = LANES // W
        pieces = []
        for p in range(E // group):
            slot = jnp.zeros((TB, LANES), jnp.int32)
            pos = jnp.full((TB, LANES), -1.0, F32)
            aff = jnp.zeros((TB, LANES), F32)
            for q in range(group):
                e = p * group + q
                mine = seg == q
                slot = jnp.where(mine, starts[e] + off, slot)
                pos = jnp.where(mine, pos_ref[:, e:e + 1], pos)
                aff = jnp.where(mine, aff_ref[:, e:e + 1], aff)
            pieces.append(jnp.where(pos == slot.astype(F32), aff, 0.0).astype(BF16))
        m = jnp.concatenate(pieces, axis=1)
        windows(k + 1, prev, wait=False)
        windows(k, cur, wait=True)
        acc_ref[cur] = _dot(m, z_ref[cur])

        @pl.when(k == NK)
        def _():
            windows(k + 1, prev, wait=True)

    @pl.when(k % 2 == 0)
    def _():
        fused(0, 1)

    @pl.when(k % 2 == 1)
    def _():
        fused(1, 0)


def _combine(cfg, offs_flat, pos_tm, aff_tm, y, x1, mod3, g):
    B, E, S, C, D = cfg.B, cfg.E, cfg.S, cfg.C, cfg.D
    W, TB = COMBINE_WINDOW, COMBINE_TOKENS
    assert LANES % W == 0 and E % (LANES // W) == 0 and C % W == 0 and W % BF16_ROWS == 0 and S % TB == 0
    nk = S // TB
    assert (nk + 3) * (TB // LANES) <= LANES
    cur = lambda b, k, offs: (b * nk + jnp.minimum(k, nk - 1), 0)
    lag = lambda b, k, offs: (b * nk + jnp.maximum(k - 1, 0), 0)
    grid_spec = pltpu.PrefetchScalarGridSpec(
        num_scalar_prefetch=1,
        grid=(B, nk + 1),
        in_specs=[pl.BlockSpec((TB, LANES), cur),
                  pl.BlockSpec((TB, LANES), cur),
                  pl.BlockSpec((TB, LANES), lag),
                  pl.BlockSpec((TB, LANES), lag),
                  pl.BlockSpec(memory_space=pl.ANY),
                  pl.BlockSpec((TB, D), lag),
                  pl.BlockSpec((1, 6, D), lambda b, k, offs: (b, 0, 0)),
                  pl.BlockSpec((1, D), lambda b, k, offs: (0, 0))],
        out_specs=pl.BlockSpec((TB, D), lag),
        scratch_shapes=[pltpu.VMEM((2, E * W, D), BF16), pltpu.VMEM((2, TB, D), F32),
                        pltpu.VMEM((W, D), BF16),
                        pltpu.SemaphoreType.DMA((2,)), pltpu.SemaphoreType.DMA(())],
    )
    return pl.pallas_call(
        functools.partial(_combine_kernel, E=E, C=C, NK=nk),
        grid_spec=grid_spec,
        out_shape=jax.ShapeDtypeStruct((cfg.T, D), F32),
        compiler_params=_params("arbitrary", "arbitrary"),
        name="combine",
    )(offs_flat, pos_tm, aff_tm, pos_tm, aff_tm, y, x1, mod3, g)


def _layer(cfg, layer, x2, c_pad, pos2, inv, sgn, w_ada, b_ada, g_pre_mix, g_post_mix, g_pre_ffn,
           g_post_ffn, w_in, hg_lb_logits, hg_out_norm, attn_sink, w_branch_a, w_branch_b, w_out,
           w_router, w_exp_gate, w_exp_up, w_exp_down):
    D, E = cfg.D, cfg.E
    row = lambda v: v.reshape(1, -1)
    mod = _ada(c_pad, w_ada[layer], row(b_ada[layer]))[:cfg.B]
    mod3 = mod.reshape(cfg.B, 6, D)
    h = _prenorm(cfg, x2, row(g_pre_mix[layer]), mod3)
    proj = _inproj(h, w_in[layer])
    q_rot, k_rot = _rope(cfg, proj, pos2, inv, sgn)
    att = _attn(cfg, q_rot, k_rot, proj, attn_sink[layer])
    gain3 = hg_out_norm[layer].reshape(cfg.HH // HGRN_HEADS_PER_STEP, 1, HGRN_HEADS_PER_STEP * HEAD)
    oa = _hgrn(cfg, proj, hg_lb_logits, gain3, layer)
    merged = _merge(cfg, oa, att, proj, w_branch_a[layer], w_branch_b[layer])
    wr_pad = jnp.pad(w_router[layer], ((0, 0), (0, LANES - E)))
    x1, h2, aff_tm = _outproj(cfg, merged, _to_bf16(w_out[layer]), x2, mod3, row(g_post_mix[layer]),
                              row(g_pre_ffn[layer]), wr_pad)
    posm, pos_tm, offs = _route(cfg, aff_tm)
    offs_flat = offs.reshape(-1)
    xe = _gather(cfg, offs_flat, posm.reshape(cfg.B * E, 1, cfg.S), h2)
    ye = _ffn(cfg, xe, w_exp_gate[layer], w_exp_up[layer], w_exp_down[layer])
    return _combine(cfg, offs_flat, pos_tm, aff_tm, ye, x1, mod3, row(g_post_ffn[layer]))


def _block(cfg, x, c, positions, w_ada, *rest):
    depth = w_ada.shape[0]
    x2 = x.reshape(cfg.T, cfg.D)
    c_pad = jnp.pad(c, ((0, 8 - cfg.B), (0, 0)))
    pos2 = positions.reshape(cfg.T, 1)
    half = HEAD // 2
    inv_half = ROPE_THETA ** (-jnp.arange(half, dtype=F32) / half)
    inv = jnp.concatenate([inv_half, inv_half]).reshape(1, HEAD)
    sgn = jnp.concatenate([-jnp.ones((half,), F32), jnp.ones((half,), F32)]).reshape(1, HEAD)
    for layer in range(depth):
        x2 = _layer(cfg, layer, x2, c_pad, pos2, inv, sgn, w_ada, *rest)
    return x2.reshape(cfg.B, cfg.S, cfg.D)


def kernel(x, c, positions, w_ada, b_ada, g_pre_mix, g_post_mix, g_pre_ffn, g_post_ffn, w_in,
           hg_lb_logits, hg_out_norm, attn_sink, w_branch_a, w_branch_b, w_out, w_router,
           w_exp_gate, w_exp_up, w_exp_down):
    B, S, D = x.shape
    cfg = Cfg(B=B, S=S, D=D,
              HH=hg_out_norm.shape[1], AH=attn_sink.shape[1],
              KVH=(w_in.shape[2] - 5 * hg_out_norm.shape[1] * HEAD - attn_sink.shape[1] * HEAD - 2 * D) // (2 * HEAD),
              E=w_router.shape[2], FF=w_exp_gate.shape[3])
    return _block(cfg, x, c, positions, w_ada, b_ada, g_pre_mix, g_post_mix, g_pre_ffn, g_post_ffn,
                  w_in, hg_lb_logits, hg_out_norm, attn_sink, w_branch_a, w_branch_b, w_out,
                  w_router, w_exp_gate, w_exp_up, w_exp_down)
```

```python
import functools
from typing import NamedTuple

import jax
import jax.numpy as jnp
from jax import lax
from jax.experimental import pallas as pl
from jax.experimental.pallas import tpu as pltpu

F32 = jnp.float32
BF16 = jnp.bfloat16

LANES = 128
HEAD = 128
HG_CHUNK = 64
WINDOW = 128
ROPE_THETA = 10000.0
LOG2E = 1.4426950408889634
NORM_EPS = 1e-6
CAPACITY_FACTOR = 2
VMEM_LIMIT_BYTES = 56 * 1024 * 1024
SEARCH_PROBES = 3
SEARCH_ROUNDS = 76
HGRN_HEADS_PER_STEP = 4
HGRN_SCORE_ROWS = 256
COMBINE_TOKENS = 256
COMBINE_WINDOW = 64
GATHER_EXPERTS = 8
BF16_ROWS = 16


class Cfg(NamedTuple):
    B: int
    S: int
    D: int
    HH: int
    AH: int
    KVH: int
    E: int
    FF: int

    @property
    def T(self):
        return self.B * self.S

    @property
    def C(self):
        return CAPACITY_FACTOR * self.S // self.E

    @property
    def offsets(self):
        hq = self.HH * HEAD
        widths = (hq, hq, hq, hq, hq, self.AH * HEAD, self.KVH * HEAD, self.KVH * HEAD, self.D, self.D)
        offs, o = [], 0
        for w in widths:
            offs.append(o)
            o += w
        return tuple(offs), o


def _params(*sem):
    return pltpu.CompilerParams(dimension_semantics=sem, vmem_limit_bytes=VMEM_LIMIT_BYTES)


def _pick(n, target):
    if n <= target:
        return n
    t = (target // LANES) * LANES
    while t > LANES and n % t:
        t -= LANES
    return t


def _dot(a, b):
    return jnp.dot(a, b, preferred_element_type=F32)


def _dot_nt(a, b):
    return lax.dot_general(a, b, (((1,), (1,)), ((), ())), preferred_element_type=F32)


def _dot_tn(a, b):
    return lax.dot_general(a, b, (((0,), (0,)), ((), ())), preferred_element_type=F32)


def _silu(x):
    return x * jax.nn.sigmoid(x)


def _rms(x):
    return x * lax.rsqrt(jnp.mean(x * x, axis=-1, keepdims=True) + NORM_EPS)


def _ada_kernel(c_ref, w_ref, b_ref, o_ref):
    a = _silu(c_ref[...]).astype(BF16)
    o_ref[...] = _dot(a, w_ref[...].astype(BF16)) + b_ref[...]


def _ada(c_pad, w, b):
    rows, d = c_pad.shape
    n = w.shape[1]
    tn = _pick(n, 1024)
    return pl.pallas_call(
        _ada_kernel,
        grid=(n // tn,),
        in_specs=[pl.BlockSpec((rows, d), lambda j: (0, 0)),
                  pl.BlockSpec((d, tn), lambda j: (0, j)),
                  pl.BlockSpec((1, tn), lambda j: (0, j))],
        out_specs=pl.BlockSpec((rows, tn), lambda j: (0, j)),
        out_shape=jax.ShapeDtypeStruct((rows, n), F32),
        compiler_params=_params("arbitrary"),
        name="ada",
    )(c_pad, w, b)


def _prenorm_kernel(x_ref, g_ref, mod_ref, o_ref):
    y = _rms(x_ref[...]) * g_ref[...]
    shift = mod_ref[0, 0:1, :]
    scale = mod_ref[0, 1:2, :]
    o_ref[...] = (y * (1.0 + scale) + shift).astype(BF16)


def _prenorm(cfg, x2, g, mod3):
    tm = _pick(cfg.S, 1024)
    per_b = cfg.S // tm
    return pl.pallas_call(
        _prenorm_kernel,
        grid=(cfg.T // tm,),
        in_specs=[pl.BlockSpec((tm, cfg.D), lambda i: (i, 0)),
                  pl.BlockSpec((1, cfg.D), lambda i: (0, 0)),
                  pl.BlockSpec((1, 6, cfg.D), lambda i: (i // per_b, 0, 0))],
        out_specs=pl.BlockSpec((tm, cfg.D), lambda i: (i, 0)),
        out_shape=jax.ShapeDtypeStruct((cfg.T, cfg.D), BF16),
        compiler_params=_params("arbitrary"),
        name="prenorm",
    )(x2, g, mod3)


def _inproj_kernel(a_ref, w_ref, o_ref, wbf_ref):
    @pl.when(pl.program_id(1) == 0)
    def _():
        wbf_ref[...] = w_ref[...].astype(BF16)

    o_ref[...] = _dot(a_ref[...], wbf_ref[...]).astype(o_ref.dtype)


def _inproj(a, w):
    m, k = a.shape
    n = w.shape[1]
    tn = _pick(n, 1792)
    tm = _pick(m, 1024)
    return pl.pallas_call(
        _inproj_kernel,
        grid=(n // tn, m // tm),
        in_specs=[pl.BlockSpec((tm, k), lambda j, i: (i, 0)),
                  pl.BlockSpec((k, tn), lambda j, i: (0, j))],
        out_specs=pl.BlockSpec((tm, tn), lambda j, i: (i, j)),
        out_shape=jax.ShapeDtypeStruct((m, n), BF16),
        scratch_shapes=[pltpu.VMEM((k, tn), BF16)],
        compiler_params=_params("arbitrary", "arbitrary"),
        name="inproj",
    )(a, w)


def _rope_kernel(q_ref, k_ref, pos_ref, inv_ref, sgn_ref, qo_ref, ko_ref, *, q_scale):
    ang = pos_ref[...].astype(F32) * inv_ref[...]
    cos = jnp.cos(ang)
    sin = jnp.sin(ang) * sgn_ref[...]

    def rotate(x_ref, o_ref, scale):
        for h in range(x_ref.shape[1] // HEAD):
            cols = slice(h * HEAD, (h + 1) * HEAD)
            x = x_ref[:, cols].astype(F32)
            r = x * cos + pltpu.roll(x, HEAD // 2, 1) * sin
            o_ref[:, cols] = (r if scale is None else r * scale).astype(BF16)

    rotate(q_ref, qo_ref, q_scale)
    rotate(k_ref, ko_ref, None)


def _rope(cfg, proj, pos2, inv, sgn):
    offs, _ = cfg.offsets
    wq, wk = cfg.AH * HEAD, cfg.KVH * HEAD
    assert offs[5] % wq == 0 and offs[6] % wk == 0
    tm = _pick(cfg.T, 512)
    return pl.pallas_call(
        functools.partial(_rope_kernel, q_scale=HEAD ** -0.5 * LOG2E),
        grid=(cfg.T // tm,),
        in_specs=[pl.BlockSpec((tm, wq), lambda i: (i, offs[5] // wq)),
                  pl.BlockSpec((tm, wk), lambda i: (i, offs[6] // wk)),
                  pl.BlockSpec((tm, 1), lambda i: (i, 0)),
                  pl.BlockSpec((1, HEAD), lambda i: (0, 0)),
                  pl.BlockSpec((1, HEAD), lambda i: (0, 0))],
        out_specs=[pl.BlockSpec((tm, wq), lambda i: (i, 0)),
                   pl.BlockSpec((tm, wk), lambda i: (i, 0))],
        out_shape=[jax.ShapeDtypeStruct((cfg.T, wq), BF16),
                   jax.ShapeDtypeStruct((cfg.T, wk), BF16)],
        compiler_params=_params("arbitrary"),
        name="rope",
    )(proj, proj, pos2, inv, sgn)


def _attn_kernel(sink_ref, q_ref, kp_ref, kc_ref, kn_ref, vp_ref, vc_ref, vn_ref, o_ref, *, G, KVH, NB):
    n = pl.program_id(1)
    blk = WINDOW
    row = lax.broadcasted_iota(jnp.int32, (G * blk, 3 * blk), 0)
    qi = row - (row // blk) * blk
    kj = lax.broadcasted_iota(jnp.int32, (G * blk, 3 * blk), 1)
    valid = jnp.abs(kj - blk - qi) <= WINDOW
    valid = valid & ((kj >= blk) | (n > 0)) & ((kj < 2 * blk) | (n < NB - 1))
    head = lax.broadcasted_iota(jnp.int32, (G * blk, 1), 0) // blk
    for kvh in range(KVH):
        kv = slice(kvh * HEAD, (kvh + 1) * HEAD)
        k = jnp.concatenate([kp_ref[:, kv], kc_ref[:, kv], kn_ref[:, kv]], axis=0)
        v = jnp.concatenate([vp_ref[:, kv], vc_ref[:, kv], vn_ref[:, kv]], axis=0).astype(BF16)
        heads = [kvh * G + g for g in range(G)]
        q = jnp.concatenate([q_ref[:, h * HEAD:(h + 1) * HEAD] for h in heads], axis=0)
        sk = jnp.zeros((G * blk, 1), F32)
        for g, h in enumerate(heads):
            sk = jnp.where(head == g, sink_ref[h] * LOG2E, sk)
        s = jnp.where(valid, _dot_nt(q, k), -jnp.inf)
        m = jnp.maximum(jnp.max(s, axis=-1, keepdims=True), sk)
        p = jnp.exp2(s - m)
        denom = jnp.sum(p, axis=-1, keepdims=True) + jnp.exp2(sk - m)
        o = (_dot(p.astype(BF16), v) / denom).astype(BF16)
        for g, h in enumerate(heads):
            o_ref[:, h * HEAD:(h + 1) * HEAD] = o[g * blk:(g + 1) * blk]


def _attn(cfg, q_rot, k_rot, proj, sink):
    offs, _ = cfg.offsets
    G = cfg.AH // cfg.KVH
    blk = WINDOW
    NB = cfg.S // blk
    wq, wkv = cfg.AH * HEAD, cfg.KVH * HEAD
    assert offs[7] % wkv == 0
    vcol = offs[7] // wkv
    prev = lambda n: jnp.maximum(n - 1, 0)
    nxt = lambda n: jnp.minimum(n + 1, NB - 1)
    return pl.pallas_call(
        functools.partial(_attn_kernel, G=G, KVH=cfg.KVH, NB=NB),
        grid=(cfg.B, NB),
        in_specs=[pl.BlockSpec(memory_space=pltpu.SMEM),
                  pl.BlockSpec((blk, wq), lambda b, n: (b * NB + n, 0)),
                  pl.BlockSpec((blk, wkv), lambda b, n: (b * NB + prev(n), 0)),
                  pl.BlockSpec((blk, wkv), lambda b, n: (b * NB + n, 0)),
                  pl.BlockSpec((blk, wkv), lambda b, n: (b * NB + nxt(n), 0)),
                  pl.BlockSpec((blk, wkv), lambda b, n: (b * NB + prev(n), vcol)),
                  pl.BlockSpec((blk, wkv), lambda b, n: (b * NB + n, vcol)),
                  pl.BlockSpec((blk, wkv), lambda b, n: (b * NB + nxt(n), vcol))],
        out_specs=pl.BlockSpec((blk, wq), lambda b, n: (b * NB + n, 0)),
        out_shape=jax.ShapeDtypeStruct((cfg.T, wq), BF16),
        compiler_params=_params("arbitrary", "arbitrary"),
        name="attn",
    )(sink, q_rot, k_rot, k_rot, k_rot, proj, proj, proj)


def _lower_bound(lbl_ref, d, layer):
    n = lbl_ref.shape[1]
    rows = [lbl_ref[d, j:j + 1, :] for j in range(n)]
    m = functools.reduce(jnp.maximum, rows)
    es = [jnp.exp(r - m) for r in rows]
    return sum(es[:layer + 1]) / sum(es)


def _hgrn_cumsum(logfs, tri2):
    P = 2 * HG_CHUNK
    npair = logfs[0].shape[0] // P
    cols = []
    for lf in logfs:
        hi = lf.astype(BF16)
        lo = (lf - hi.astype(F32)).astype(BF16)
        for p in range(npair):
            cols += [hi[p * P:(p + 1) * P], lo[p * P:(p + 1) * P]]
    both = _dot(tri2.astype(BF16), jnp.concatenate(cols, axis=1))
    outs = []
    for s in range(len(logfs)):
        parts = []
        for p in range(npair):
            c0 = (s * npair + p) * 2 * HEAD
            parts.append(both[:, c0:c0 + HEAD] + both[:, c0 + HEAD:c0 + 2 * HEAD])
        outs.append(jnp.concatenate(parts, axis=0))
    return outs


def _hgrn_tile(qs, k, b, v, st, mask, fwd):
    L = HG_CHUNK
    nc = qs.shape[0] // L
    ref_i, last_i = (L // 2, L - 1) if fwd else (L - 1 - L // 2, 0)

    def per_chunk(rows):
        return jnp.concatenate([jnp.broadcast_to(r, (L, HEAD)) for r in rows], axis=0)

    brefs = [b[c * L + ref_i:c * L + ref_i + 1, :] for c in range(nc)]
    blasts = [b[c * L + last_i:c * L + last_i + 1, :] for c in range(nc)]
    d = b - per_chunk(brefs)
    q_in = qs * jnp.exp(d)
    k_in = k * jnp.exp(-d)
    q_bf, k_bf, v_bf = q_in.astype(BF16), k_in.astype(BF16), v.astype(BF16)
    nb = mask.shape[0]
    o_intra = []
    for r0 in range(0, qs.shape[0], nb):
        r = slice(r0, r0 + nb)
        a = jnp.where(mask, _dot_nt(q_bf[r], k_bf[r]), 0.0)
        o_intra.append(_dot(a.astype(BF16), v_bf[r]))
    o_intra = jnp.concatenate(o_intra, axis=0)
    q_dec = (q_in * per_chunk([jnp.exp(r) for r in brefs])).astype(BF16)
    k_dec = k_in * per_chunk([jnp.exp(l - r) for l, r in zip(blasts, brefs)])

    zero = jnp.zeros((L, HEAD), F32)
    cs = []
    for p in range(nc // 2):
        r = slice(2 * p * L, 2 * (p + 1) * L)
        kd = k_dec[r]
        k_blk = jnp.concatenate([jnp.concatenate([kd[:L], zero], axis=1),
                                 jnp.concatenate([zero, kd[L:]], axis=1)], axis=0).astype(BF16)
        both = _dot_tn(v_bf[r], k_blk)
        cs += [both[:, :HEAD], both[:, HEAD:]]
    entering = [None] * nc
    for c in (range(nc) if fwd else reversed(range(nc))):
        entering[c] = st
        st = st * jnp.exp(blasts[c]) + cs[c]
    o_inter = []
    for p in range(nc // 2):
        r = slice(2 * p * L, 2 * (p + 1) * L)
        st2 = jnp.concatenate([entering[2 * p], entering[2 * p + 1]], axis=0).astype(BF16)
        both = _dot_nt(q_dec[r], st2)
        o_inter += [both[:L, :HEAD], both[L:, HEAD:]]
    return o_intra + jnp.concatenate(o_inter, axis=0), st


def _hgrn_kernel(lbl_ref, gain_ref, qf_ref, vf_ref, zf_ref, qb_ref, vb_ref, zb_ref, g_ref, o_ref,
                 st_ref, of_ref, ob_ref, *, NT, R, HP, layer):
    i = pl.program_id(2)

    def chunk_masks(n):
        li = lax.broadcasted_iota(jnp.int32, (n, n), 0)
        mi = lax.broadcasted_iota(jnp.int32, (n, n), 1)
        same = (li // HG_CHUNK) == (mi // HG_CHUNK)
        return same & (mi <= li), same & (mi >= li)

    tri_f, tri_b = (jnp.where(m, 1.0, 0.0).astype(F32) for m in chunk_masks(2 * HG_CHUNK))
    mask_f, mask_b = chunk_masks(min(R, HGRN_SCORE_ROWS))

    @pl.when(i == 0)
    def _():
        st_ref[...] = jnp.zeros_like(st_ref)

    lanes = [slice(p * HEAD, (p + 1) * HEAD) for p in range(HP)]
    for d, (q_ref, v_ref, z_ref) in enumerate(((qf_ref, vf_ref, zf_ref), (qb_ref, vb_ref, zb_ref))):
        fwd = d == 0
        lb = _lower_bound(lbl_ref, d, layer)
        f = lb + (1.0 - lb) * jax.nn.sigmoid(z_ref[...].astype(F32))
        k = 1.0 - f
        qs = _silu(q_ref[...].astype(F32))
        cums = _hgrn_cumsum([jnp.log(f)[:, lanes[p]] for p in range(HP)], tri_f if fwd else tri_b)
        tile = i if fwd else NT - 1 - i
        rows = pl.ds(pl.multiple_of(tile * R, R), R)
        for p in range(HP):
            o, st = _hgrn_tile(qs[:, lanes[p]], k[:, lanes[p]], cums[p], v_ref[:, lanes[p]], st_ref[p, d],
                               mask_f if fwd else mask_b, fwd)
            (of_ref if fwd else ob_ref)[rows, lanes[p]] = o
            st_ref[p, d] = st

    @pl.when(i == NT - 1)
    def _():
        for t in range(NT):
            rows = slice(t * R, (t + 1) * R)
            for p in range(HP):
                o = of_ref[rows, lanes[p]] + ob_ref[rows, lanes[p]]
                gate = _silu(g_ref[rows, lanes[p]].astype(F32))
                o_ref[rows, lanes[p]] = (_rms(o) * gain_ref[0, :, lanes[p]] * gate).astype(BF16)


def _hgrn(cfg, proj, lb_logits, gain3, layer):
    offs, _ = cfg.offsets
    HP = HGRN_HEADS_PER_STEP
    W = HP * HEAD
    assert cfg.HH % HP == 0 and all(o % W == 0 for o in offs[:5])
    qc, fc, bc, ic, gc = (o // W for o in offs[:5])
    R = _pick(cfg.S, 512)
    NT = cfg.S // R
    nl = lb_logits.shape[1]
    fwd = lambda col: pl.BlockSpec((R, W), lambda b, h, i: (b * NT + i, col + h))
    bwd = lambda col: pl.BlockSpec((R, W), lambda b, h, i: (b * NT + NT - 1 - i, col + h))
    return pl.pallas_call(
        functools.partial(_hgrn_kernel, NT=NT, R=R, HP=HP, layer=layer),
        grid=(cfg.B, cfg.HH // HP, NT),
        in_specs=[pl.BlockSpec((2, nl, W), lambda b, h, i: (0, 0, h)),
                  pl.BlockSpec((1, 1, W), lambda b, h, i: (h, 0, 0)),
                  fwd(qc), fwd(ic), fwd(fc), bwd(qc), bwd(ic), bwd(bc),
                  pl.BlockSpec((cfg.S, W), lambda b, h, i: (b, gc + h))],
        out_specs=pl.BlockSpec((cfg.S, W), lambda b, h, i: (b, h)),
        out_shape=jax.ShapeDtypeStruct((cfg.T, cfg.HH * HEAD), BF16),
        scratch_shapes=[pltpu.VMEM((HP, 2, HEAD, HEAD), F32),
                        pltpu.VMEM((cfg.S, W), F32), pltpu.VMEM((cfg.S, W), F32)],
        compiler_params=_params("arbitrary", "arbitrary", "arbitrary"),
        name="hgrn",
    )(lb_logits, gain3, proj, proj, proj, proj, proj, proj, proj)


def _merge_kernel(oa_ref, ob_ref, *refs, nsub):
    ga_refs, gb_refs = refs[:nsub], refs[nsub:2 * nsub]
    wa_ref, wb_ref, o_ref, wabf_ref, wbbf_ref = refs[2 * nsub:]

    @pl.when(pl.program_id(1) == 0)
    def _():
        wabf_ref[...] = wa_ref[...].astype(BF16)
        wbbf_ref[...] = wb_ref[...].astype(BF16)

    ya = _dot(oa_ref[...], wabf_ref[...])
    yb = _dot(ob_ref[...], wbbf_ref[...])
    gw = ya.shape[1] // nsub
    for u in range(nsub):
        cols = slice(u * gw, (u + 1) * gw)
        o_ref[:, cols] = (jax.nn.sigmoid(ga_refs[u][...].astype(F32)) * ya[:, cols]
                          + jax.nn.sigmoid(gb_refs[u][...].astype(F32)) * yb[:, cols]).astype(BF16)


def _merge(cfg, oa, ob, proj, wa, wb):
    offs, _ = cfg.offsets
    gw = _pick(cfg.D, 512)
    assert offs[8] % gw == 0 and offs[9] % gw == 0
    nsub = 2 if cfg.D % (2 * gw) == 0 else 1
    tn = nsub * gw
    ga0, gb0 = offs[8] // gw, offs[9] // gw
    tm = _pick(cfg.T, 1024)
    ka, kb = oa.shape[1], ob.shape[1]
    gate = lambda c0, u: pl.BlockSpec((tm, gw), lambda j, i: (i, c0 + nsub * j + u))
    return pl.pallas_call(
        functools.partial(_merge_kernel, nsub=nsub),
        grid=(cfg.D // tn, cfg.T // tm),
        in_specs=[pl.BlockSpec((tm, ka), lambda j, i: (i, 0)),
                  pl.BlockSpec((tm, kb), lambda j, i: (i, 0)),
                  *[gate(ga0, u) for u in range(nsub)],
                  *[gate(gb0, u) for u in range(nsub)],
                  pl.BlockSpec((ka, tn), lambda j, i: (0, j)),
                  pl.BlockSpec((kb, tn), lambda j, i: (0, j))],
        out_specs=pl.BlockSpec((tm, tn), lambda j, i: (i, j)),
        out_shape=jax.ShapeDtypeStruct((cfg.T, cfg.D), BF16),
        scratch_shapes=[pltpu.VMEM((ka, tn), BF16), pltpu.VMEM((kb, tn), BF16)],
        compiler_params=_params("arbitrary", "arbitrary"),
        name="merge",
    )(oa, ob, *([proj] * (2 * nsub)), wa, wb)


def _cast_kernel(w_ref, o_ref):
    o_ref[...] = w_ref[...].astype(BF16)


def _to_bf16(w):
    k, n = w.shape
    tk = _pick(k, 512)
    return pl.pallas_call(
        _cast_kernel,
        grid=(k // tk,),
        in_specs=[pl.BlockSpec((tk, n), lambda i: (i, 0))],
        out_specs=pl.BlockSpec((tk, n), lambda i: (i, 0)),
        out_shape=jax.ShapeDtypeStruct((k, n), BF16),
        compiler_params=_params("arbitrary"),
        name="wcast",
    )(w)


def _outproj_kernel(a_ref, w_ref, x_ref, mod_ref, gpost_ref, gpre_ref, wr_ref,
                    x1_ref, h_ref, aff_ref, y_ref, *, E, SUB):
    i = pl.program_id(0)

    @pl.when(i == 0)
    def _():
        y_ref[...] = jnp.zeros_like(y_ref)

    def fused(cur, prev):
        y_ref[cur] = _dot(a_ref[...], w_ref[...])
        wr = wr_ref[...]
        w_hi = wr.astype(BF16)
        w_pair = jnp.concatenate([w_hi, (wr - w_hi.astype(F32)).astype(BF16)], axis=1)
        post_gain = mod_ref[0, 2:3, :] * gpost_ref[...]
        pre_gain = gpre_ref[...] * (1.0 + mod_ref[0, 4:5, :])
        for u in range(a_ref.shape[0] // SUB):
            rows = slice(u * SUB, (u + 1) * SUB)
            x1 = x_ref[rows, :] + _rms(y_ref[prev, rows, :]) * post_gain
            x1_ref[rows, :] = x1
            h = _rms(x1) * pre_gain + mod_ref[0, 3:4, :]
            h_bf = h.astype(BF16)
            h_ref[rows, :] = h_bf
            both = _dot(h_bf, w_pair)
            logits = both[:, :LANES] + both[:, LANES:]
            lane = lax.broadcasted_iota(jnp.int32, logits.shape, 1)
            logits = jnp.where(lane < E, logits, -jnp.inf)
            p = jnp.exp(logits - jnp.max(logits, axis=-1, keepdims=True))
            aff_ref[rows, :] = p / jnp.sum(p, axis=-1, keepdims=True)

    @pl.when(i % 2 == 0)
    def _():
        fused(0, 1)

    @pl.when(i % 2 == 1)
    def _():
        fused(1, 0)


def _outproj(cfg, merged, w_bf, x2, mod3, gpost, gpre, wr_pad):
    tm = _pick(cfg.S, 512)
    sub = _pick(tm, 256)
    per_b = cfg.S // tm
    n = cfg.T // tm
    D = cfg.D
    const = lambda i: (0, 0)
    cur = lambda i: (jnp.minimum(i, n - 1), 0)
    lag = lambda i: (jnp.maximum(i - 1, 0), 0)
    return pl.pallas_call(
        functools.partial(_outproj_kernel, E=cfg.E, SUB=sub),
        grid=(n + 1,),
        in_specs=[pl.BlockSpec((tm, D), cur),
                  pl.BlockSpec((D, D), const, pipeline_mode=pl.Buffered(1)),
                  pl.BlockSpec((tm, D), lag),
                  pl.BlockSpec((1, 6, D), lambda i: (jnp.maximum(i - 1, 0) // per_b, 0, 0)),
                  pl.BlockSpec((1, D), const),
                  pl.BlockSpec((1, D), const),
                  pl.BlockSpec((D, LANES), const)],
        out_specs=[pl.BlockSpec((tm, D), lag),
                   pl.BlockSpec((tm, D), lag),
                   pl.BlockSpec((tm, LANES), lag)],
        out_shape=[jax.ShapeDtypeStruct((cfg.T, D), F32),
                   jax.ShapeDtypeStruct((cfg.T, D), BF16),
                   jax.ShapeDtypeStruct((cfg.T, LANES), F32)],
        scratch_shapes=[pltpu.VMEM((2, tm, D), F32)],
        compiler_params=_params("arbitrary"),
        name="outproj",
    )(merged, w_bf, x2, mod3, gpost, gpre, wr_pad)


def _route_kernel(aff_ref, posm_ref, postm_ref, offs_ref, a_ref, *, B, E, S, C):
    nt = S // LANES
    BE = B * E
    for b in range(B):
        for t in range(nt):
            tok = slice(b * S + t * LANES, b * S + (t + 1) * LANES)
            a_ref[b * E:(b + 1) * E, t * LANES:(t + 1) * LANES] = aff_ref[tok, :].T[:E, :]

    cap = jnp.float32(C)

    def narrow(_, carry):
        lo, hi = carry
        width = hi - lo
        a = a_ref[...]
        new_lo, new_hi = lo, hi
        found = jnp.zeros(lo.shape, jnp.bool_)
        for j in range(SEARCH_PROBES, 0, -1):
            m = lo + (j / (SEARCH_PROBES + 1)) * width
            keep = jnp.sum(jnp.where(a >= m, 1.0, 0.0), axis=1, keepdims=True) >= cap
            new_lo = jnp.where(keep & ~found, m, new_lo)
            new_hi = jnp.where(keep | found, new_hi, m)
            found = found | keep
        return new_lo, new_hi

    lo, hi = lax.fori_loop(0, SEARCH_ROUNDS, narrow,
                           (jnp.zeros((BE, 1), F32), jnp.full((BE, 1), 2.0, F32)))
    need = cap - jnp.sum(jnp.where(a_ref[...] >= hi, 1.0, 0.0), axis=1, keepdims=True)

    ui = lax.broadcasted_iota(jnp.int32, (LANES, LANES), 0)
    uj = lax.broadcasted_iota(jnp.int32, (LANES, LANES), 1)
    upper = jnp.where(ui <= uj, 1.0, 0.0).astype(BF16)
    lane = lax.broadcasted_iota(jnp.int32, (BE, LANES), 1)
    run_tie = jnp.zeros((BE, 1), F32)
    run_sel = jnp.zeros((BE, 1), F32)
    offs = jnp.zeros((BE, LANES), F32)
    pad = jnp.full((LANES - E, LANES), -1.0, F32)
    for t in range(nt):
        cols = slice(t * LANES, (t + 1) * LANES)
        a = a_ref[:, cols]
        tie = (a >= lo) & (a < hi)
        tie_t = jnp.where(tie, 1.0, 0.0)
        tie_rank = _dot(tie_t.astype(BF16), upper) - tie_t + run_tie
        sel = (a >= hi) | (tie & (tie_rank < need))
        sel_t = jnp.where(sel, 1.0, 0.0)
        pos = _dot(sel_t.astype(BF16), upper) - sel_t + run_sel
        posm = jnp.where(sel, pos, -1.0)
        posm_ref[:, cols] = posm
        for b in range(B):
            tok = slice(b * S + t * LANES, b * S + (t + 1) * LANES)
            postm_ref[tok, :] = jnp.concatenate([posm[b * E:(b + 1) * E], pad], axis=0).T
        offs = jnp.where(lane == t, run_sel, offs)
        run_tie = run_tie + jnp.sum(tie_t, axis=1, keepdims=True)
        run_sel = run_sel + jnp.sum(sel_t, axis=1, keepdims=True)
    offs = jnp.where(lane == nt, run_sel, offs)
    offs_ref[...] = offs.astype(jnp.int32)


def _route(cfg, aff_tm):
    B, E, S = cfg.B, cfg.E, cfg.S
    assert S // LANES < LANES and E % 8 == 0
    whole = lambda i: (0, 0)
    return pl.pallas_call(
        functools.partial(_route_kernel, B=B, E=E, S=S, C=cfg.C),
        grid=(1,),
        in_specs=[pl.BlockSpec((cfg.T, LANES), whole)],
        out_specs=[pl.BlockSpec((B * E, S), whole),
                   pl.BlockSpec((cfg.T, LANES), whole),
                   pl.BlockSpec((B * E, LANES), whole)],
        out_shape=[jax.ShapeDtypeStruct((B * E, S), F32),
                   jax.ShapeDtypeStruct((cfg.T, LANES), F32),
                   jax.ShapeDtypeStruct((B * E, LANES), jnp.int32)],
        scratch_shapes=[pltpu.VMEM((B * E, S), F32)],
        compiler_params=_params("arbitrary"),
        name="route",
    )(aff_tm)


def _slot_window(offs_ref, row, token_block, C):
    base = row * LANES + token_block * (COMBINE_TOKENS // LANES)
    first, end = offs_ref[base], offs_ref[base + COMBINE_TOKENS // LANES]
    start = jnp.minimum((first // BF16_ROWS) * BF16_ROWS, C - COMBINE_WINDOW)
    return pl.multiple_of(start, BF16_ROWS), end


def _extra_windows(start, end):
    return (jnp.maximum(end - start - COMBINE_WINDOW, 0) + COMBINE_WINDOW - 1) // COMBINE_WINDOW


def _gather_kernel(offs_ref, posm_ref, h_ref, x_ref, *, E, G, C):
    b, grp, kb = pl.program_id(0), pl.program_id(1), pl.program_id(2)
    W, TB = COMBINE_WINDOW, COMBINE_TOKENS

    @pl.when(kb == 0)
    def _():
        x_ref[...] = jnp.zeros_like(x_ref)

    slot0 = lax.broadcasted_iota(jnp.int32, (W, TB), 0)
    starts, pieces = [], []
    for j in range(G):
        start, _ = _slot_window(offs_ref, b * E + grp * G + j, kb, C)
        starts.append(start)
        pieces.append(jnp.where(posm_ref[j] == (start + slot0).astype(F32), 1.0, 0.0).astype(BF16))
    rows = _dot(jnp.concatenate(pieces, axis=0), h_ref[...]).astype(BF16)
    for j in range(G):
        win = pl.ds(starts[j], W)
        x_ref[j, win, :] = x_ref[j, win, :] + rows[j * W:(j + 1) * W]

    def spill(j, carry):
        start, end = _slot_window(offs_ref, b * E + grp * G + j, kb, C)

        def extra(w, c2):
            cur = start + (w + 1) * W
            s2 = pl.multiple_of(jnp.minimum(cur, C - W), BF16_ROWS)
            slot = s2 + slot0
            onehot = jnp.where((posm_ref[j] == slot.astype(F32)) & (slot >= cur), 1.0, 0.0).astype(BF16)
            win = pl.ds(s2, W)
            x_ref[j, win, :] = x_ref[j, win, :] + _dot(onehot, h_ref[...]).astype(BF16)
            return c2

        return lax.fori_loop(0, _extra_windows(start, end), extra, carry)

    lax.fori_loop(0, G, spill, 0)


def _gather(cfg, offs_flat, posm3, h2):
    B, E, S, C, D = cfg.B, cfg.E, cfg.S, cfg.C, cfg.D
    G, TB = min(E, GATHER_EXPERTS), COMBINE_TOKENS
    assert E % G == 0 and S % TB == 0 and C % COMBINE_WINDOW == 0
    nk = S // TB
    grid_spec = pltpu.PrefetchScalarGridSpec(
        num_scalar_prefetch=1,
        grid=(B, E // G, nk),
        in_specs=[pl.BlockSpec((G, 1, TB), lambda b, g, k, offs: (b * (E // G) + g, 0, k)),
                  pl.BlockSpec((TB, D), lambda b, g, k, offs: (b * nk + k, 0))],
        out_specs=pl.BlockSpec((G, C, D), lambda b, g, k, offs: (g, b, 0)),
    )
    return pl.pallas_call(
        functools.partial(_gather_kernel, E=E, G=G, C=C),
        grid_spec=grid_spec,
        out_shape=jax.ShapeDtypeStruct((E, B * C, D), BF16),
        compiler_params=_params("arbitrary", "arbitrary", "arbitrary"),
        name="gather",
    )(offs_flat, posm3, h2)


def _ffn_kernel(x_ref, wg_ref, wu_ref, wd_ref, y_ref, acc_ref):
    f = pl.program_id(1)

    @pl.when(f == 0)
    def _():
        acc_ref[...] = jnp.zeros_like(acc_ref)

    x = x_ref[0]
    a = (_silu(_dot(x, wg_ref[0].astype(BF16))) * _dot(x, wu_ref[0].astype(BF16))).astype(BF16)
    acc_ref[...] += _dot(a, wd_ref[0].astype(BF16))

    @pl.when(f == pl.num_programs(1) - 1)
    def _():
        y_ref[0] = acc_ref[...].astype(BF16)


def _ffn(cfg, xe, wg, wu, wd):
    B, E, C, D, FF = cfg.B, cfg.E, cfg.C, cfg.D, cfg.FF
    tf = _pick(FF, 256)
    return pl.pallas_call(
        _ffn_kernel,
        grid=(E, FF // tf),
        in_specs=[pl.BlockSpec((1, B * C, D), lambda e, f: (e, 0, 0)),
                  pl.BlockSpec((1, D, tf), lambda e, f: (e, 0, f)),
                  pl.BlockSpec((1, D, tf), lambda e, f: (e, 0, f)),
                  pl.BlockSpec((1, tf, D), lambda e, f: (e, f, 0))],
        out_specs=pl.BlockSpec((1, B * C, D), lambda e, f: (e, 0, 0)),
        out_shape=jax.ShapeDtypeStruct((E, B * C, D), BF16),
        scratch_shapes=[pltpu.VMEM((B * C, D), F32)],
        compiler_params=_params("arbitrary", "arbitrary"),
        name="ffn",
    )(xe, wg, wu, wd)


def _combine_kernel(offs_ref, pos_ref, aff_ref, ppos_ref, paff_ref, y_hbm, x1_ref, mod_ref, g_ref, o_ref,
                    z_ref, acc_ref, zx_ref, sem, semx, *, E, C, NK):
    b, k = pl.program_id(0), pl.program_id(1)
    W, TB = COMBINE_WINDOW, COMBINE_TOKENS
    lane = lax.broadcasted_iota(jnp.int32, (TB, LANES), 1)

    def window_copy(e, start, half):
        return pltpu.make_async_copy(y_hbm.at[e, pl.ds(b * C + start, W), :],
                                     z_ref.at[half, pl.ds(e * W, W), :], sem.at[half])

    def windows(block, half, wait):
        starts = []
        for e in range(E):
            start, _ = _slot_window(offs_ref, b * E + e, block, C)
            cp = window_copy(e, start, half)
            cp.wait() if wait else cp.start()
            starts.append(start)
        return starts

    @pl.when(k == 0)
    def _():
        acc_ref[...] = jnp.zeros_like(acc_ref)
        windows(0, 0, wait=False)

    def spill(e, carry):
        start, end = _slot_window(offs_ref, b * E + e, k - 1, C)

        def extra(w, c2):
            mine = lane == e
            pos = jnp.sum(jnp.where(mine, ppos_ref[...], 0.0), axis=1, keepdims=True)
            aff = jnp.sum(jnp.where(mine, paff_ref[...], 0.0), axis=1, keepdims=True)
            cur = start + (w + 1) * W
            s2 = pl.multiple_of(jnp.minimum(cur, C - W), BF16_ROWS)
            slot = s2 + lax.broadcasted_iota(jnp.int32, (TB, W), 1)
            m = jnp.where((pos == slot.astype(F32)) & (slot >= cur), aff, 0.0).astype(BF16)
            cp = pltpu.make_async_copy(y_hbm.at[e, pl.ds(b * C + s2, W), :], zx_ref, semx)
            cp.start()
            cp.wait()
            acc_ref[(k + 1) % 2] += _dot(m, zx_ref[...])
            return c2

        return lax.fori_loop(0, _extra_windows(start, end), extra, carry)

    @pl.when(k > 0)
    def _():
        lax.fori_loop(0, E, spill, 0)

    def fused(cur, prev):
        starts = [_slot_window(offs_ref, b * E + e, k, C)[0] for e in range(E)]
        o_ref[...] = x1_ref[...] + mod_ref[0, 5:6, :] * (_rms(acc_ref[prev]) * g_ref[...])
        seg = lane // W
        off = lane - seg * W
        group = LANES // W
        pieces = []
        for p in range(E // group):
            slot = jnp.zeros((TB, LANES), jnp.int32)
            pos = jnp.full((TB, LANES), -1.0, F32)
            aff = jnp.zeros((TB, LANES), F32)
            for q in range(group):
                e = p * group + q
                mine = seg == q
                slot = jnp.where(mine, starts[e] + off, slot)
                pos = jnp.where(mine, pos_ref[:, e:e + 1], pos)
                aff = jnp.where(mine, aff_ref[:, e:e + 1], aff)
            pieces.append(jnp.where(pos == slot.astype(F32), aff, 0.0).astype(BF16))
        m = jnp.concatenate(pieces, axis=1)
        windows(k + 1, prev, wait=False)
        windows(k, cur, wait=True)
        acc_ref[cur] = _dot(m, z_ref[cur])

        @pl.when(k == NK)
        def _():
            windows(k + 1, prev, wait=True)

    @pl.when(k % 2 == 0)
    def _():
        fused(0, 1)

    @pl.when(k % 2 == 1)
    def _():
        fused(1, 0)


def _combine(cfg, offs_flat, pos_tm, aff_tm, y, x1, mod3, g):
    B, E, S, C, D = cfg.B, cfg.E, cfg.S, cfg.C, cfg.D
    W, TB = COMBINE_WINDOW, COMBINE_TOKENS
    assert LANES % W == 0 and E % (LANES // W) == 0 and C % W == 0 and W % BF16_ROWS == 0 and S % TB == 0
    nk = S // TB
    assert (nk + 3) * (TB // LANES) <= LANES
    cur = lambda b, k, offs: (b * nk + jnp.minimum(k, nk - 1), 0)
    lag = lambda b, k, offs: (b * nk + jnp.maximum(k - 1, 0), 0)
    grid_spec = pltpu.PrefetchScalarGridSpec(
        num_scalar_prefetch=1,
        grid=(B, nk + 1),
        in_specs=[pl.BlockSpec((TB, LANES), cur),
                  pl.BlockSpec((TB, LANES), cur),
                  pl.BlockSpec((TB, LANES), lag),
                  pl.BlockSpec((TB, LANES), lag),
                  pl.BlockSpec(memory_space=pl.ANY),
                  pl.BlockSpec((TB, D), lag),
                  pl.BlockSpec((1, 6, D), lambda b, k, offs: (b, 0, 0)),
                  pl.BlockSpec((1, D), lambda b, k, offs: (0, 0))],
        out_specs=pl.BlockSpec((TB, D), lag),
        scratch_shapes=[pltpu.VMEM((2, E * W, D), BF16), pltpu.VMEM((2, TB, D), F32),
                        pltpu.VMEM((W, D), BF16),
                        pltpu.SemaphoreType.DMA((2,)), pltpu.SemaphoreType.DMA(())],
    )
    return pl.pallas_call(
        functools.partial(_combine_kernel, E=E, C=C, NK=nk),
        grid_spec=grid_spec,
        out_shape=jax.ShapeDtypeStruct((cfg.T, D), F32),
        compiler_params=_params("arbitrary", "arbitrary"),
        name="combine",
    )(offs_flat, pos_tm, aff_tm, pos_tm, aff_tm, y, x1, mod3, g)


def _layer(cfg, layer, x2, c_pad, pos2, inv, sgn, w_ada, b_ada, g_pre_mix, g_post_mix, g_pre_ffn,
           g_post_ffn, w_in, hg_lb_logits, hg_out_norm, attn_sink, w_branch_a, w_branch_b, w_out,
           w_router, w_exp_gate, w_exp_up, w_exp_down):
    D, E = cfg.D, cfg.E
    row = lambda v: v.reshape(1, -1)
    mod = _ada(c_pad, w_ada[layer], row(b_ada[layer]))[:cfg.B]
    mod3 = mod.reshape(cfg.B, 6, D)
    h = _prenorm(cfg, x2, row(g_pre_mix[layer]), mod3)
    proj = _inproj(h, w_in[layer])
    q_rot, k_rot = _rope(cfg, proj, pos2, inv, sgn)
    att = _attn(cfg, q_rot, k_rot, proj, attn_sink[layer])
    gain3 = hg_out_norm[layer].reshape(cfg.HH // HGRN_HEADS_PER_STEP, 1, HGRN_HEADS_PER_STEP * HEAD)
    oa = _hgrn(cfg, proj, hg_lb_logits, gain3, layer)
    merged = _merge(cfg, oa, att, proj, w_branch_a[layer], w_branch_b[layer])
    wr_pad = jnp.pad(w_router[layer], ((0, 0), (0, LANES - E)))
    x1, h2, aff_tm = _outproj(cfg, merged, _to_bf16(w_out[layer]), x2, mod3, row(g_post_mix[layer]),
                              row(g_pre_ffn[layer]), wr_pad)
    posm, pos_tm, offs = _route(cfg, aff_tm)
    offs_flat = offs.reshape(-1)
    xe = _gather(cfg, offs_flat, posm.reshape(cfg.B * E, 1, cfg.S), h2)
    ye = _ffn(cfg, xe, w_exp_gate[layer], w_exp_up[layer], w_exp_down[layer])
    return _combine(cfg, offs_flat, pos_tm, aff_tm, ye, x1, mod3, row(g_post_ffn[layer]))


def _block(cfg, x, c, positions, w_ada, *rest):
    depth = w_ada.shape[0]
    x2 = x.reshape(cfg.T, cfg.D)
    c_pad = jnp.pad(c, ((0, 8 - cfg.B), (0, 0)))
    pos2 = positions.reshape(cfg.T, 1)
    half = HEAD // 2
    inv_half = ROPE_THETA ** (-jnp.arange(half, dtype=F32) / half)
    inv = jnp.concatenate([inv_half, inv_half]).reshape(1, HEAD)
    sgn = jnp.concatenate([-jnp.ones((half,), F32), jnp.ones((half,), F32)]).reshape(1, HEAD)
    for layer in range(depth):
        x2 = _layer(cfg, layer, x2, c_pad, pos2, inv, sgn, w_ada, *rest)
    return x2.reshape(cfg.B, cfg.S, cfg.D)


def kernel(x, c, positions, w_ada, b_ada, g_pre_mix, g_post_mix, g_pre_ffn, g_post_ffn, w_in,
           hg_lb_logits, hg_out_norm, attn_sink, w_branch_a, w_branch_b, w_out, w_router,
           w_exp_gate, w_exp_up, w_exp_down):
    B, S, D = x.shape
    cfg = Cfg(B=B, S=S, D=D,
              HH=hg_out_norm.shape[1], AH=attn_sink.shape[1],
              KVH=(w_in.shape[2] - 5 * hg_out_norm.shape[1] * HEAD - attn_sink.shape[1] * HEAD - 2 * D) // (2 * HEAD),
              E=w_router.shape[2], FF=w_exp_gate.shape[3])
    return _block(cfg, x, c, positions, w_ada, b_ada, g_pre_mix, g_post_mix, g_pre_ffn, g_post_ffn,
                  w_in, hg_lb_logits, hg_out_norm, attn_sink, w_branch_a, w_branch_b, w_out,
                  w_router, w_exp_gate, w_exp_up, w_exp_down)
```

```python
import functools
from typing import NamedTuple

import jax
import jax.numpy as jnp
from jax import lax
from jax.experimental import pallas as pl
from jax.experimental.pallas import tpu as pltpu

F32 = jnp.float32
BF16 = jnp.bfloat16

LANES = 128
HEAD = 128
HG_CHUNK = 64
WINDOW = 128
ROPE_THETA = 10000.0
LOG2E = 1.4426950408889634
NORM_EPS = 1e-6
CAPACITY_FACTOR = 2
VMEM_LIMIT_BYTES = 56 * 1024 * 1024
SEARCH_PROBES = 3
SEARCH_ROUNDS = 76
HGRN_HEADS_PER_STEP = 4
HGRN_SCORE_ROWS = 256
COMBINE_TOKENS = 256
COMBINE_WINDOW = 64
GATHER_EXPERTS = 8
BF16_ROWS = 16


class Cfg(NamedTuple):
    B: int
    S: int
    D: int
    HH: int
    AH: int
    KVH: int
    E: int
    FF: int

    @property
    def T(self):
        return self.B * self.S

    @property
    def C(self):
        return CAPACITY_FACTOR * self.S // self.E

    @property
    def offsets(self):
        hq = self.HH * HEAD
        widths = (hq, hq, hq, hq, hq, self.AH * HEAD, self.KVH * HEAD, self.KVH * HEAD, self.D, self.D)
        offs, o = [], 0
        for w in widths:
            offs.append(o)
            o += w
        return tuple(offs), o


def _params(*sem):
    return pltpu.CompilerParams(dimension_semantics=sem, vmem_limit_bytes=VMEM_LIMIT_BYTES)


def _pick(n, target):
    if n <= target:
        return n
    t = (target // LANES) * LANES
    while t > LANES and n % t:
        t -= LANES
    return t


def _dot(a, b):
    return jnp.dot(a, b, preferred_element_type=F32)


def _dot_nt(a, b):
    return lax.dot_general(a, b, (((1,), (1,)), ((), ())), preferred_element_type=F32)


def _dot_tn(a, b):
    return lax.dot_general(a, b, (((0,), (0,)), ((), ())), preferred_element_type=F32)


def _silu(x):
    return x * jax.nn.sigmoid(x)


def _rms(x):
    return x * lax.rsqrt(jnp.mean(x * x, axis=-1, keepdims=True) + NORM_EPS)


def _ada_kernel(c_ref, w_ref, b_ref, o_ref):
    a = _silu(c_ref[...]).astype(BF16)
    o_ref[...] = _dot(a, w_ref[...].astype(BF16)) + b_ref[...]


def _ada(c_pad, w, b):
    rows, d = c_pad.shape
    n = w.shape[1]
    tn = _pick(n, 1024)
    return pl.pallas_call(
        _ada_kernel,
        grid=(n // tn,),
        in_specs=[pl.BlockSpec((rows, d), lambda j: (0, 0)),
                  pl.BlockSpec((d, tn), lambda j: (0, j)),
                  pl.BlockSpec((1, tn), lambda j: (0, j))],
        out_specs=pl.BlockSpec((rows, tn), lambda j: (0, j)),
        out_shape=jax.ShapeDtypeStruct((rows, n), F32),
        compiler_params=_params("arbitrary"),
        name="ada",
    )(c_pad, w, b)


def _prenorm_kernel(x_ref, g_ref, mod_ref, o_ref):
    y = _rms(x_ref[...]) * g_ref[...]
    shift = mod_ref[0, 0:1, :]
    scale = mod_ref[0, 1:2, :]
    o_ref[...] = (y * (1.0 + scale) + shift).astype(BF16)


def _prenorm(cfg, x2, g, mod3):
    tm = _pick(cfg.S, 1024)
    per_b = cfg.S // tm
    return pl.pallas_call(
        _prenorm_kernel,
        grid=(cfg.T // tm,),
        in_specs=[pl.BlockSpec((tm, cfg.D), lambda i: (i, 0)),
                  pl.BlockSpec((1, cfg.D), lambda i: (0, 0)),
                  pl.BlockSpec((1, 6, cfg.D), lambda i: (i // per_b, 0, 0))],
        out_specs=pl.BlockSpec((tm, cfg.D), lambda i: (i, 0)),
        out_shape=jax.ShapeDtypeStruct((cfg.T, cfg.D), BF16),
        compiler_params=_params("arbitrary"),
        name="prenorm",
    )(x2, g, mod3)


def _inproj_kernel(a_ref, w_ref, o_ref, wbf_ref):
    @pl.when(pl.program_id(1) == 0)
    def _():
        wbf_ref[...] = w_ref[...].astype(BF16)

    o_ref[...] = _dot(a_ref[...], wbf_ref[...]).astype(o_ref.dtype)


def _inproj(a, w):
    m, k = a.shape
    n = w.shape[1]
    tn = _pick(n, 1792)
    tm = _pick(m, 1024)
    return pl.pallas_call(
        _inproj_kernel,
        grid=(n // tn, m // tm),
        in_specs=[pl.BlockSpec((tm, k), lambda j, i: (i, 0)),
                  pl.BlockSpec((k, tn), lambda j, i: (0, j))],
        out_specs=pl.BlockSpec((tm, tn), lambda j, i: (i, j)),
        out_shape=jax.ShapeDtypeStruct((m, n), BF16),
        scratch_shapes=[pltpu.VMEM((k, tn), BF16)],
        compiler_params=_params("arbitrary", "arbitrary"),
        name="inproj",
    )(a, w)


def _rope_kernel(q_ref, k_ref, pos_ref, inv_ref, sgn_ref, qo_ref, ko_ref, *, q_scale):
    ang = pos_ref[...].astype(F32) * inv_ref[...]
    cos = jnp.cos(ang)
    sin = jnp.sin(ang) * sgn_ref[...]

    def rotate(x_ref, o_ref, scale):
        for h in range(x_ref.shape[1] // HEAD):
            cols = slice(h * HEAD, (h + 1) * HEAD)
            x = x_ref[:, cols].astype(F32)
            r = x * cos + pltpu.roll(x, HEAD // 2, 1) * sin
            o_ref[:, cols] = (r if scale is None else r * scale).astype(BF16)

    rotate(q_ref, qo_ref, q_scale)
    rotate(k_ref, ko_ref, None)


def _rope(cfg, proj, pos2, inv, sgn):
    offs, _ = cfg.offsets
    wq, wk = cfg.AH * HEAD, cfg.KVH * HEAD
    assert offs[5] % wq == 0 and offs[6] % wk == 0
    tm = _pick(cfg.T, 512)
    return pl.pallas_call(
        functools.partial(_rope_kernel, q_scale=HEAD ** -0.5 * LOG2E),
        grid=(cfg.T // tm,),
        in_specs=[pl.BlockSpec((tm, wq), lambda i: (i, offs[5] // wq)),
                  pl.BlockSpec((tm, wk), lambda i: (i, offs[6] // wk)),
                  pl.BlockSpec((tm, 1), lambda i: (i, 0)),
                  pl.BlockSpec((1, HEAD), lambda i: (0, 0)),
                  pl.BlockSpec((1, HEAD), lambda i: (0, 0))],
        out_specs=[pl.BlockSpec((tm, wq), lambda i: (i, 0)),
                   pl.BlockSpec((tm, wk), lambda i: (i, 0))],
        out_shape=[jax.ShapeDtypeStruct((cfg.T, wq), BF16),
                   jax.ShapeDtypeStruct((cfg.T, wk), BF16)],
        compiler_params=_params("arbitrary"),
        name="rope",
    )(proj, proj, pos2, inv, sgn)


def _attn_kernel(sink_ref, q_ref, kp_ref, kc_ref, kn_ref, vp_ref, vc_ref, vn_ref, o_ref, *, G, KVH, NB):
    n = pl.program_id(1)
    blk = WINDOW
    row = lax.broadcasted_iota(jnp.int32, (G * blk, 3 * blk), 0)
    qi = row - (row // blk) * blk
    kj = lax.broadcasted_iota(jnp.int32, (G * blk, 3 * blk), 1)
    valid = jnp.abs(kj - blk - qi) <= WINDOW
    valid = valid & ((kj >= blk) | (n > 0)) & ((kj < 2 * blk) | (n < NB - 1))
    head = lax.broadcasted_iota(jnp.int32, (G * blk, 1), 0) // blk
    for kvh in range(KVH):
        kv = slice(kvh * HEAD, (kvh + 1) * HEAD)
        k = jnp.concatenate([kp_ref[:, kv], kc_ref[:, kv], kn_ref[:, kv]], axis=0)
        v = jnp.concatenate([vp_ref[:, kv], vc_ref[:, kv], vn_ref[:, kv]], axis=0).astype(BF16)
        heads = [kvh * G + g for g in range(G)]
        q = jnp.concatenate([q_ref[:, h * HEAD:(h + 1) * HEAD] for h in heads], axis=0)
        sk = jnp.zeros((G * blk, 1), F32)
        for g, h in enumerate(heads):
            sk = jnp.where(head == g, sink_ref[h] * LOG2E, sk)
        s = jnp.where(valid, _dot_nt(q, k), -jnp.inf)
        m = jnp.maximum(jnp.max(s, axis=-1, keepdims=True), sk)
        p = jnp.exp2(s - m)
        denom = jnp.sum(p, axis=-1, keepdims=True) + jnp.exp2(sk - m)
        o = (_dot(p.astype(BF16), v) / denom).astype(BF16)
        for g, h in enumerate(heads):
            o_ref[:, h * HEAD:(h + 1) * HEAD] = o[g * blk:(g + 1) * blk]


def _attn(cfg, q_rot, k_rot, proj, sink):
    offs, _ = cfg.offsets
    G = cfg.AH // cfg.KVH
    blk = WINDOW
    NB = cfg.S // blk
    wq, wkv = cfg.AH * HEAD, cfg.KVH * HEAD
    assert offs[7] % wkv == 0
    vcol = offs[7] // wkv
    prev = lambda n: jnp.maximum(n - 1, 0)
    nxt = lambda n: jnp.minimum(n + 1, NB - 1)
    return pl.pallas_call(
        functools.partial(_attn_kernel, G=G, KVH=cfg.KVH, NB=NB),
        grid=(cfg.B, NB),
        in_specs=[pl.BlockSpec(memory_space=pltpu.SMEM),
                  pl.BlockSpec((blk, wq), lambda b, n: (b * NB + n, 0)),
                  pl.BlockSpec((blk, wkv), lambda b, n: (b * NB + prev(n), 0)),
                  pl.BlockSpec((blk, wkv), lambda b, n: (b * NB + n, 0)),
                  pl.BlockSpec((blk, wkv), lambda b, n: (b * NB + nxt(n), 0)),
                  pl.BlockSpec((blk, wkv), lambda b, n: (b * NB + prev(n), vcol)),
                  pl.BlockSpec((blk, wkv), lambda b, n: (b * NB + n, vcol)),
                  pl.BlockSpec((blk, wkv), lambda b, n: (b * NB + nxt(n), vcol))],
        out_specs=pl.BlockSpec((blk, wq), lambda b, n: (b * NB + n, 0)),
        out_shape=jax.ShapeDtypeStruct((cfg.T, wq), BF16),
        compiler_params=_params("arbitrary", "arbitrary"),
        name="attn",
    )(sink, q_rot, k_rot, k_rot, k_rot, proj, proj, proj)


def _lower_bound(lbl_ref, d, layer):
    n = lbl_ref.shape[1]
    rows = [lbl_ref[d, j:j + 1, :] for j in range(n)]
    m = functools.reduce(jnp.maximum, rows)
    es = [jnp.exp(r - m) for r in rows]
    return sum(es[:layer + 1]) / sum(es)


def _hgrn_cumsum(logfs, tri2):
    P = 2 * HG_CHUNK
    npair = logfs[0].shape[0] // P
    cols = []
    for lf in logfs:
        hi = lf.astype(BF16)
        lo = (lf - hi.astype(F32)).astype(BF16)
        for p in range(npair):
            cols += [hi[p * P:(p + 1) * P], lo[p * P:(p + 1) * P]]
    both = _dot(tri2.astype(BF16), jnp.concatenate(cols, axis=1))
    outs = []
    for s in range(len(logfs)):
        parts = []
        for p in range(npair):
            c0 = (s * npair + p) * 2 * HEAD
            parts.append(both[:, c0:c0 + HEAD] + both[:, c0 + HEAD:c0 + 2 * HEAD])
        outs.append(jnp.concatenate(parts, axis=0))
    return outs


def _hgrn_tile(qs, k, b, v, st, mask, fwd):
    L = HG_CHUNK
    nc = qs.shape[0] // L
    ref_i, last_i = (L // 2, L - 1) if fwd else (L - 1 - L // 2, 0)

    def per_chunk(rows):
        return jnp.concatenate([jnp.broadcast_to(r, (L, HEAD)) for r in rows], axis=0)

    brefs = [b[c * L + ref_i:c * L + ref_i + 1, :] for c in range(nc)]
    blasts = [b[c * L + last_i:c * L + last_i + 1, :] for c in range(nc)]
    d = b - per_chunk(brefs)
    q_in = qs * jnp.exp(d)
    k_in = k * jnp.exp(-d)
    q_bf, k_bf, v_bf = q_in.astype(BF16), k_in.astype(BF16), v.astype(BF16)
    nb = mask.shape[0]
    o_intra = []
    for r0 in range(0, qs.shape[0], nb):
        r = slice(r0, r0 + nb)
        a = jnp.where(mask, _dot_nt(q_bf[r], k_bf[r]), 0.0)
        o_intra.append(_dot(a.astype(BF16), v_bf[r]))
    o_intra = jnp.concatenate(o_intra, axis=0)
    q_dec = (q_in * per_chunk([jnp.exp(r) for r in brefs])).astype(BF16)
    k_dec = k_in * per_chunk([jnp.exp(l - r) for l, r in zip(blasts, brefs)])

    zero = jnp.zeros((L, HEAD), F32)
    cs = []
    for p in range(nc // 2):
        r = slice(2 * p * L, 2 * (p + 1) * L)
        kd = k_dec[r]
        k_blk = jnp.concatenate([jnp.concatenate([kd[:L], zero], axis=1),
                                 jnp.concatenate([zero, kd[L:]], axis=1)], axis=0).astype(BF16)
        both = _dot_tn(v_bf[r], k_blk)
        cs += [both[:, :HEAD], both[:, HEAD:]]
    entering = [None] * nc
    for c in (range(nc) if fwd else reversed(range(nc))):
        entering[c] = st
        st = st * jnp.exp(blasts[c]) + cs[c]
    o_inter = []
    for p in range(nc // 2):
        r = slice(2 * p * L, 2 * (p + 1) * L)
        st2 = jnp.concatenate([entering[2 * p], entering[2 * p + 1]], axis=0).astype(BF16)
        both = _dot_nt(q_dec[r], st2)
        o_inter += [both[:L, :HEAD], both[L:, HEAD:]]
    return o_intra + jnp.concatenate(o_inter, axis=0), st


def _hgrn_kernel(lbl_ref, gain_ref, qf_ref, vf_ref, zf_ref, qb_ref, vb_ref, zb_ref, g_ref, o_ref,
                 st_ref, of_ref, ob_ref, *, NT, R, HP, layer):
    i = pl.program_id(2)

    def chunk_masks(n):
        li = lax.broadcasted_iota(jnp.int32, (n, n), 0)
        mi = lax.broadcasted_iota(jnp.int32, (n, n), 1)
        same = (li // HG_CHUNK) == (mi // HG_CHUNK)
        return same & (mi <= li), same & (mi >= li)

    tri_f, tri_b = (jnp.where(m, 1.0, 0.0).astype(F32) for m in chunk_masks(2 * HG_CHUNK))
    mask_f, mask_b = chunk_masks(min(R, HGRN_SCORE_ROWS))

    @pl.when(i == 0)
    def _():
        st_ref[...] = jnp.zeros_like(st_ref)

    lanes = [slice(p * HEAD, (p + 1) * HEAD) for p in range(HP)]
    for d, (q_ref, v_ref, z_ref) in enumerate(((qf_ref, vf_ref, zf_ref), (qb_ref, vb_ref, zb_ref))):
        fwd = d == 0
        lb = _lower_bound(lbl_ref, d, layer)
        f = lb + (1.0 - lb) * jax.nn.sigmoid(z_ref[...].astype(F32))
        k = 1.0 - f
        qs = _silu(q_ref[...].astype(F32))
        cums = _hgrn_cumsum([jnp.log(f)[:, lanes[p]] for p in range(HP)], tri_f if fwd else tri_b)
        tile = i if fwd else NT - 1 - i
        rows = pl.ds(pl.multiple_of(tile * R, R), R)
        for p in range(HP):
            o, st = _hgrn_tile(qs[:, lanes[p]], k[:, lanes[p]], cums[p], v_ref[:, lanes[p]], st_ref[p, d],
                               mask_f if fwd else mask_b, fwd)
            (of_ref if fwd else ob_ref)[rows, lanes[p]] = o
            st_ref[p, d] = st

    @pl.when(i == NT - 1)
    def _():
        for t in range(NT):
            rows = slice(t * R, (t + 1) * R)
            for p in range(HP):
                o = of_ref[rows, lanes[p]] + ob_ref[rows, lanes[p]]
                gate = _silu(g_ref[rows, lanes[p]].astype(F32))
                o_ref[rows, lanes[p]] = (_rms(o) * gain_ref[0, :, lanes[p]] * gate).astype(BF16)


def _hgrn(cfg, proj, lb_logits, gain3, layer):
    offs, _ = cfg.offsets
    HP = HGRN_HEADS_PER_STEP
    W = HP * HEAD
    assert cfg.HH % HP == 0 and all(o % W == 0 for o in offs[:5])
    qc, fc, bc, ic, gc = (o // W for o in offs[:5])
    R = _pick(cfg.S, 512)
    NT = cfg.S // R
    nl = lb_logits.shape[1]
    fwd = lambda col: pl.BlockSpec((R, W), lambda b, h, i: (b * NT + i, col + h))
    bwd = lambda col: pl.BlockSpec((R, W), lambda b, h, i: (b * NT + NT - 1 - i, col + h))
    return pl.pallas_call(
        functools.partial(_hgrn_kernel, NT=NT, R=R, HP=HP, layer=layer),
        grid=(cfg.B, cfg.HH // HP, NT),
        in_specs=[pl.BlockSpec((2, nl, W), lambda b, h, i: (0, 0, h)),
                  pl.BlockSpec((1, 1, W), lambda b, h, i: (h, 0, 0)),
                  fwd(qc), fwd(ic), fwd(fc), bwd(qc), bwd(ic), bwd(bc),
                  pl.BlockSpec((cfg.S, W), lambda b, h, i: (b, gc + h))],
        out_specs=pl.BlockSpec((cfg.S, W), lambda b, h, i: (b, h)),
        out_shape=jax.ShapeDtypeStruct((cfg.T, cfg.HH * HEAD), BF16),
        scratch_shapes=[pltpu.VMEM((HP, 2, HEAD, HEAD), F32),
                        pltpu.VMEM((cfg.S, W), F32), pltpu.VMEM((cfg.S, W), F32)],
        compiler_params=_params("arbitrary", "arbitrary", "arbitrary"),
        name="hgrn",
    )(lb_logits, gain3, proj, proj, proj, proj, proj, proj, proj)


def _merge_kernel(oa_ref, ob_ref, *refs, nsub):
    ga_refs, gb_refs = refs[:nsub], refs[nsub:2 * nsub]
    wa_ref, wb_ref, wo_ref, o_ref, wobf_ref, wabf_ref, wbbf_ref = refs[2 * nsub:]
    wobf_ref[...] = wo_ref[...].astype(BF16)

    @pl.when(pl.program_id(1) == 0)
    def _():
        wabf_ref[...] = wa_ref[...].astype(BF16)
        wbbf_ref[...] = wb_ref[...].astype(BF16)

    ya = _dot(oa_ref[...], wabf_ref[...])
    yb = _dot(ob_ref[...], wbbf_ref[...])
    gw = ya.shape[1] // nsub
    for u in range(nsub):
        cols = slice(u * gw, (u + 1) * gw)
        o_ref[:, cols] = (jax.nn.sigmoid(ga_refs[u][...].astype(F32)) * ya[:, cols]
                          + jax.nn.sigmoid(gb_refs[u][...].astype(F32)) * yb[:, cols]).astype(BF16)


def _merge(cfg, oa, ob, proj, wa, wb, wo):
    offs, _ = cfg.offsets
    gw = _pick(cfg.D, 512)
    assert offs[8] % gw == 0 and offs[9] % gw == 0
    nsub = 2 if cfg.D % (2 * gw) == 0 else 1
    tn = nsub * gw
    ga0, gb0 = offs[8] // gw, offs[9] // gw
    tm = _pick(cfg.T, 1024)
    ka, kb = oa.shape[1], ob.shape[1]
    gate = lambda c0, u: pl.BlockSpec((tm, gw), lambda j, i: (i, c0 + nsub * j + u))
    ni = cfg.T // tm
    wrows = wo.shape[0] // ((cfg.D // tn) * ni)
    assert wrows * (cfg.D // tn) * ni == wo.shape[0] and wrows % BF16_ROWS == 0
    wslice = pl.BlockSpec((wrows, wo.shape[1]), lambda j, i: (j * ni + i, 0))
    return pl.pallas_call(
        functools.partial(_merge_kernel, nsub=nsub),
        grid=(cfg.D // tn, cfg.T // tm),
        in_specs=[pl.BlockSpec((tm, ka), lambda j, i: (i, 0)),
                  pl.BlockSpec((tm, kb), lambda j, i: (i, 0)),
                  *[gate(ga0, u) for u in range(nsub)],
                  *[gate(gb0, u) for u in range(nsub)],
                  pl.BlockSpec((ka, tn), lambda j, i: (0, j)),
                  pl.BlockSpec((kb, tn), lambda j, i: (0, j)),
                  wslice],
        out_specs=[pl.BlockSpec((tm, tn), lambda j, i: (i, j)), wslice],
        out_shape=[jax.ShapeDtypeStruct((cfg.T, cfg.D), BF16), jax.ShapeDtypeStruct(wo.shape, BF16)],
        scratch_shapes=[pltpu.VMEM((ka, tn), BF16), pltpu.VMEM((kb, tn), BF16)],
        compiler_params=_params("arbitrary", "arbitrary"),
        name="merge",
    )(oa, ob, *([proj] * (2 * nsub)), wa, wb, wo)


def _cast_kernel(w_ref, o_ref):
    o_ref[...] = w_ref[...].astype(BF16)


def _to_bf16(w):
    k, n = w.shape
    tk = _pick(k, 512)
    return pl.pallas_call(
        _cast_kernel,
        grid=(k // tk,),
        in_specs=[pl.BlockSpec((tk, n), lambda i: (i, 0))],
        out_specs=pl.BlockSpec((tk, n), lambda i: (i, 0)),
        out_shape=jax.ShapeDtypeStruct((k, n), BF16),
        compiler_params=_params("arbitrary"),
        name="wcast",
    )(w)


def _outproj_kernel(a_ref, w_ref, x_ref, mod_ref, gpost_ref, gpre_ref, wr_ref,
                    x1_ref, h_ref, aff_ref, y_ref, *, E, SUB):
    i = pl.program_id(0)

    @pl.when(i == 0)
    def _():
        y_ref[...] = jnp.zeros_like(y_ref)

    def fused(cur, prev):
        y_ref[cur] = _dot(a_ref[...], w_ref[...])
        wr = wr_ref[...]
        w_hi = wr.astype(BF16)
        w_pair = jnp.concatenate([w_hi, (wr - w_hi.astype(F32)).astype(BF16)], axis=1)
        post_gain = mod_ref[0, 2:3, :] * gpost_ref[...]
        pre_gain = gpre_ref[...] * (1.0 + mod_ref[0, 4:5, :])
        for u in range(a_ref.shape[0] // SUB):
            rows = slice(u * SUB, (u + 1) * SUB)
            x1 = x_ref[rows, :] + _rms(y_ref[prev, rows, :]) * post_gain
            x1_ref[rows, :] = x1
            h = _rms(x1) * pre_gain + mod_ref[0, 3:4, :]
            h_bf = h.astype(BF16)
            h_ref[rows, :] = h_bf
            both = _dot(h_bf, w_pair)
            logits = both[:, :LANES] + both[:, LANES:]
            lane = lax.broadcasted_iota(jnp.int32, logits.shape, 1)
            logits = jnp.where(lane < E, logits, -jnp.inf)
            p = jnp.exp(logits - jnp.max(logits, axis=-1, keepdims=True))
            aff_ref[rows, :] = p / jnp.sum(p, axis=-1, keepdims=True)

    @pl.when(i % 2 == 0)
    def _():
        fused(0, 1)

    @pl.when(i % 2 == 1)
    def _():
        fused(1, 0)


def _outproj(cfg, merged, w_bf, x2, mod3, gpost, gpre, wr_pad):
    tm = _pick(cfg.S, 512)
    sub = _pick(tm, 256)
    per_b = cfg.S // tm
    n = cfg.T // tm
    D = cfg.D
    const = lambda i: (0, 0)
    cur = lambda i: (jnp.minimum(i, n - 1), 0)
    lag = lambda i: (jnp.maximum(i - 1, 0), 0)
    return pl.pallas_call(
        functools.partial(_outproj_kernel, E=cfg.E, SUB=sub),
        grid=(n + 1,),
        in_specs=[pl.BlockSpec((tm, D), cur),
                  pl.BlockSpec((D, D), const, pipeline_mode=pl.Buffered(1)),
                  pl.BlockSpec((tm, D), lag),
                  pl.BlockSpec((1, 6, D), lambda i: (jnp.maximum(i - 1, 0) // per_b, 0, 0)),
                  pl.BlockSpec((1, D), const),
                  pl.BlockSpec((1, D), const),
                  pl.BlockSpec((D, LANES), const)],
        out_specs=[pl.BlockSpec((tm, D), lag),
                   pl.BlockSpec((tm, D), lag),
                   pl.BlockSpec((tm, LANES), lag)],
        out_shape=[jax.ShapeDtypeStruct((cfg.T, D), F32),
                   jax.ShapeDtypeStruct((cfg.T, D), BF16),
                   jax.ShapeDtypeStruct((cfg.T, LANES), F32)],
        scratch_shapes=[pltpu.VMEM((2, tm, D), F32)],
        compiler_params=_params("arbitrary"),
        name="outproj",
    )(merged, w_bf, x2, mod3, gpost, gpre, wr_pad)


def _route_kernel(aff_ref, posm_ref, postm_ref, offs_ref, a_ref, *, B, E, S, C):
    nt = S // LANES
    BE = B * E
    for b in range(B):
        for t in range(nt):
            tok = slice(b * S + t * LANES, b * S + (t + 1) * LANES)
            a_ref[b * E:(b + 1) * E, t * LANES:(t + 1) * LANES] = aff_ref[tok, :].T[:E, :]

    cap = jnp.float32(C)

    def narrow(_, carry):
        lo, hi = carry
        width = hi - lo
        a = a_ref[...]
        new_lo, new_hi = lo, hi
        found = jnp.zeros(lo.shape, jnp.bool_)
        for j in range(SEARCH_PROBES, 0, -1):
            m = lo + (j / (SEARCH_PROBES + 1)) * width
            keep = jnp.sum(jnp.where(a >= m, 1.0, 0.0), axis=1, keepdims=True) >= cap
            new_lo = jnp.where(keep & ~found, m, new_lo)
            new_hi = jnp.where(keep | found, new_hi, m)
            found = found | keep
        return new_lo, new_hi

    lo, hi = lax.fori_loop(0, SEARCH_ROUNDS, narrow,
                           (jnp.zeros((BE, 1), F32), jnp.full((BE, 1), 2.0, F32)))
    need = cap - jnp.sum(jnp.where(a_ref[...] >= hi, 1.0, 0.0), axis=1, keepdims=True)

    ui = lax.broadcasted_iota(jnp.int32, (LANES, LANES), 0)
    uj = lax.broadcasted_iota(jnp.int32, (LANES, LANES), 1)
    upper = jnp.where(ui <= uj, 1.0, 0.0).astype(BF16)
    lane = lax.broadcasted_iota(jnp.int32, (BE, LANES), 1)
    run_tie = jnp.zeros((BE, 1), F32)
    run_sel = jnp.zeros((BE, 1), F32)
    offs = jnp.zeros((BE, LANES), F32)
    pad = jnp.full((LANES - E, LANES), -1.0, F32)
    for t in range(nt):
        cols = slice(t * LANES, (t + 1) * LANES)
        a = a_ref[:, cols]
        tie = (a >= lo) & (a < hi)
        tie_t = jnp.where(tie, 1.0, 0.0)
        tie_rank = _dot(tie_t.astype(BF16), upper) - tie_t + run_tie
        sel = (a >= hi) | (tie & (tie_rank < need))
        sel_t = jnp.where(sel, 1.0, 0.0)
        pos = _dot(sel_t.astype(BF16), upper) - sel_t + run_sel
        posm = jnp.where(sel, pos, -1.0)
        posm_ref[:, cols] = posm
        for b in range(B):
            tok = slice(b * S + t * LANES, b * S + (t + 1) * LANES)
            postm_ref[tok, :] = jnp.concatenate([posm[b * E:(b + 1) * E], pad], axis=0).T
        offs = jnp.where(lane == t, run_sel, offs)
        run_tie = run_tie + jnp.sum(tie_t, axis=1, keepdims=True)
        run_sel = run_sel + jnp.sum(sel_t, axis=1, keepdims=True)
    offs = jnp.where(lane == nt, run_sel, offs)
    offs_ref[...] = offs.astype(jnp.int32)


def _route(cfg, aff_tm):
    B, E, S = cfg.B, cfg.E, cfg.S
    assert S // LANES < LANES and E % 8 == 0
    whole = lambda i: (0, 0)
    return pl.pallas_call(
        functools.partial(_route_kernel, B=B, E=E, S=S, C=cfg.C),
        grid=(1,),
        in_specs=[pl.BlockSpec((cfg.T, LANES), whole)],
        out_specs=[pl.BlockSpec((B * E, S), whole),
                   pl.BlockSpec((cfg.T, LANES), whole),
                   pl.BlockSpec((B * E, LANES), whole)],
        out_shape=[jax.ShapeDtypeStruct((B * E, S), F32),
                   jax.ShapeDtypeStruct((cfg.T, LANES), F32),
                   jax.ShapeDtypeStruct((B * E, LANES), jnp.int32)],
        scratch_shapes=[pltpu.VMEM((B * E, S), F32)],
        compiler_params=_params("arbitrary"),
        name="route",
    )(aff_tm)


def _slot_window(offs_ref, row, token_block, C):
    base = row * LANES + token_block * (COMBINE_TOKENS // LANES)
    first, end = offs_ref[base], offs_ref[base + COMBINE_TOKENS // LANES]
    start = jnp.minimum((first // BF16_ROWS) * BF16_ROWS, C - COMBINE_WINDOW)
    return pl.multiple_of(start, BF16_ROWS), end


def _extra_windows(start, end):
    return (jnp.maximum(end - start - COMBINE_WINDOW, 0) + COMBINE_WINDOW - 1) // COMBINE_WINDOW


def _gather_kernel(offs_ref, posm_ref, h_ref, x_ref, *, E, G, C):
    b, grp, kb = pl.program_id(0), pl.program_id(1), pl.program_id(2)
    W, TB = COMBINE_WINDOW, COMBINE_TOKENS

    @pl.when(kb == 0)
    def _():
        x_ref[...] = jnp.zeros_like(x_ref)

    slot0 = lax.broadcasted_iota(jnp.int32, (W, TB), 0)
    starts, pieces = [], []
    for j in range(G):
        start, _ = _slot_window(offs_ref, b * E + grp * G + j, kb, C)
        starts.append(start)
        pieces.append(jnp.where(posm_ref[j] == (start + slot0).astype(F32), 1.0, 0.0).astype(BF16))
    rows = _dot(jnp.concatenate(pieces, axis=0), h_ref[...]).astype(BF16)
    for j in range(G):
        win = pl.ds(starts[j], W)
        x_ref[j, win, :] = x_ref[j, win, :] + rows[j * W:(j + 1) * W]

    def spill(j, carry):
        start, end = _slot_window(offs_ref, b * E + grp * G + j, kb, C)

        def extra(w, c2):
            cur = start + (w + 1) * W
            s2 = pl.multiple_of(jnp.minimum(cur, C - W), BF16_ROWS)
            slot = s2 + slot0
            onehot = jnp.where((posm_ref[j] == slot.astype(F32)) & (slot >= cur), 1.0, 0.0).astype(BF16)
            win = pl.ds(s2, W)
            x_ref[j, win, :] = x_ref[j, win, :] + _dot(onehot, h_ref[...]).astype(BF16)
            return c2

        return lax.fori_loop(0, _extra_windows(start, end), extra, carry)

    lax.fori_loop(0, G, spill, 0)


def _gather(cfg, offs_flat, posm3, h2):
    B, E, S, C, D = cfg.B, cfg.E, cfg.S, cfg.C, cfg.D
    G, TB = min(E, GATHER_EXPERTS), COMBINE_TOKENS
    assert E % G == 0 and S % TB == 0 and C % COMBINE_WINDOW == 0
    nk = S // TB
    grid_spec = pltpu.PrefetchScalarGridSpec(
        num_scalar_prefetch=1,
        grid=(B, E // G, nk),
        in_specs=[pl.BlockSpec((G, 1, TB), lambda b, g, k, offs: (b * (E // G) + g, 0, k)),
                  pl.BlockSpec((TB, D), lambda b, g, k, offs: (b * nk + k, 0))],
        out_specs=pl.BlockSpec((G, C, D), lambda b, g, k, offs: (g, b, 0)),
    )
    return pl.pallas_call(
        functools.partial(_gather_kernel, E=E, G=G, C=C),
        grid_spec=grid_spec,
        out_shape=jax.ShapeDtypeStruct((E, B * C, D), BF16),
        compiler_params=_params("arbitrary", "arbitrary", "arbitrary"),
        name="gather",
    )(offs_flat, posm3, h2)


def _ffn_kernel(x_ref, wg_ref, wu_ref, wd_ref, y_ref, acc_ref):
    f = pl.program_id(1)

    @pl.when(f == 0)
    def _():
        acc_ref[...] = jnp.zeros_like(acc_ref)

    x = x_ref[0]
    a = (_silu(_dot(x, wg_ref[0].astype(BF16))) * _dot(x, wu_ref[0].astype(BF16))).astype(BF16)
    acc_ref[...] += _dot(a, wd_ref[0].astype(BF16))

    @pl.when(f == pl.num_programs(1) - 1)
    def _():
        y_ref[0] = acc_ref[...].astype(BF16)


def _ffn(cfg, xe, wg, wu, wd):
    B, E, C, D, FF = cfg.B, cfg.E, cfg.C, cfg.D, cfg.FF
    tf = _pick(FF, 256)
    return pl.pallas_call(
        _ffn_kernel,
        grid=(E, FF // tf),
        in_specs=[pl.BlockSpec((1, B * C, D), lambda e, f: (e, 0, 0)),
                  pl.BlockSpec((1, D, tf), lambda e, f: (e, 0, f)),
                  pl.BlockSpec((1, D, tf), lambda e, f: (e, 0, f)),
                  pl.BlockSpec((1, tf, D), lambda e, f: (e, f, 0))],
        out_specs=pl.BlockSpec((1, B * C, D), lambda e, f: (e, 0, 0)),
        out_shape=jax.ShapeDtypeStruct((E, B * C, D), BF16),
        scratch_shapes=[pltpu.VMEM((B * C, D), F32)],
        compiler_params=_params("arbitrary", "arbitrary"),
        name="ffn",
    )(xe, wg, wu, wd)


def _combine_kernel(offs_ref, pos_ref, aff_ref, ppos_ref, paff_ref, y_hbm, x1_ref, mod_ref, g_ref, o_ref,
                    z_ref, acc_ref, zx_ref, sem, semx, *, E, C, NK):
    b, k = pl.program_id(0), pl.program_id(1)
    W, TB = COMBINE_WINDOW, COMBINE_TOKENS
    lane = lax.broadcasted_iota(jnp.int32, (TB, LANES), 1)

    def window_copy(e, start, half):
        return pltpu.make_async_copy(y_hbm.at[e, pl.ds(b * C + start, W), :],
                                     z_ref.at[half, pl.ds(e * W, W), :], sem.at[half])

    def windows(block, half, wait):
        starts = []
        for e in range(E):
            start, _ = _slot_window(offs_ref, b * E + e, block, C)
            cp = window_copy(e, start, half)
            cp.wait() if wait else cp.start()
            starts.append(start)
        return starts

    @pl.when(k == 0)
    def _():
        acc_ref[...] = jnp.zeros_like(acc_ref)
        windows(0, 0, wait=False)

    def spill(e, carry):
        start, end = _slot_window(offs_ref, b * E + e, k - 1, C)

        def extra(w, c2):
            mine = lane == e
            pos = jnp.sum(jnp.where(mine, ppos_ref[...], 0.0), axis=1, keepdims=True)
            aff = jnp.sum(jnp.where(mine, paff_ref[...], 0.0), axis=1, keepdims=True)
            cur = start + (w + 1) * W
            s2 = pl.multiple_of(jnp.minimum(cur, C - W), BF16_ROWS)
            slot = s2 + lax.broadcasted_iota(jnp.int32, (TB, W), 1)
            m = jnp.where((pos == slot.astype(F32)) & (slot >= cur), aff, 0.0).astype(BF16)
            cp = pltpu.make_async_copy(y_hbm.at[e, pl.ds(b * C + s2, W), :], zx_ref, semx)
            cp.start()
            cp.wait()
            acc_ref[(k + 1) % 2] += _dot(m, zx_ref[...])
            return c2

        return lax.fori_loop(0, _extra_windows(start, end), extra, carry)

    @pl.when(k > 0)
    def _():
        lax.fori_loop(0, E, spill, 0)

    def fused(cur, prev):
        starts = [_slot_window(offs_ref, b * E + e, k, C)[0] for e in range(E)]
        o_ref[...] = x1_ref[...] + mod_ref[0, 5:6, :] * (_rms(acc_ref[prev]) * g_ref[...])
        seg = lane // W
        off = lane - seg * W
        group = LANES // W
        pieces = []
        for p in range(E // group):
            slot = jnp.zeros((TB, LANES), jnp.int32)
            pos = jnp.full((TB, LANES), -1.0, F32)
            aff = jnp.zeros((TB, LANES), F32)
            for q in range(group):
                e = p * group + q
                mine = seg == q
                slot = jnp.where(mine, starts[e] + off, slot)
                pos = jnp.where(mine, pos_ref[:, e:e + 1], pos)
                aff = jnp.where(mine, aff_ref[:, e:e + 1], aff)
            pieces.append(jnp.where(pos == slot.astype(F32), aff, 0.0).astype(BF16))
        m = jnp.concatenate(pieces, axis=1)
        windows(k + 1, prev, wait=False)
        windows(k, cur, wait=True)
        acc_ref[cur] = _dot(m, z_ref[cur])

        @pl.when(k == NK)
        def _():
            windows(k + 1, prev, wait=True)

    @pl.when(k % 2 == 0)
    def _():
        fused(0, 1)

    @pl.when(k % 2 == 1)
    def _():
        fused(1, 0)


def _combine(cfg, offs_flat, pos_tm, aff_tm, y, x1, mod3, g):
    B, E, S, C, D = cfg.B, cfg.E, cfg.S, cfg.C, cfg.D
    W, TB = COMBINE_WINDOW, COMBINE_TOKENS
    assert LANES % W == 0 and E % (LANES // W) == 0 and C % W == 0 and W % BF16_ROWS == 0 and S % TB == 0
    nk = S // TB
    assert (nk + 3) * (TB // LANES) <= LANES
    cur = lambda b, k, offs: (b * nk + jnp.minimum(k, nk - 1), 0)
    lag = lambda b, k, offs: (b * nk + jnp.maximum(k - 1, 0), 0)
    grid_spec = pltpu.PrefetchScalarGridSpec(
        num_scalar_prefetch=1,
        grid=(B, nk + 1),
        in_specs=[pl.BlockSpec((TB, LANES), cur),
                  pl.BlockSpec((TB, LANES), cur),
                  pl.BlockSpec((TB, LANES), lag),
                  pl.BlockSpec((TB, LANES), lag),
                  pl.BlockSpec(memory_space=pl.ANY),
                  pl.BlockSpec((TB, D), lag),
                  pl.BlockSpec((1, 6, D), lambda b, k, offs: (b, 0, 0)),
                  pl.BlockSpec((1, D), lambda b, k, offs: (0, 0))],
        out_specs=pl.BlockSpec((TB, D), lag),
        scratch_shapes=[pltpu.VMEM((2, E * W, D), BF16), pltpu.VMEM((2, TB, D), F32),
                        pltpu.VMEM((W, D), BF16),
                        pltpu.SemaphoreType.DMA((2,)), pltpu.SemaphoreType.DMA(())],
    )
    return pl.pallas_call(
        functools.partial(_combine_kernel, E=E, C=C, NK=nk),
        grid_spec=grid_spec,
        out_shape=jax.ShapeDtypeStruct((cfg.T, D), F32),
        compiler_params=_params("arbitrary", "arbitrary"),
        name="combine",
    )(offs_flat, pos_tm, aff_tm, pos_tm, aff_tm, y, x1, mod3, g)


def _layer(cfg, layer, x2, c_pad, pos2, inv, sgn, w_ada, b_ada, g_pre_mix, g_post_mix, g_pre_ffn,
           g_post_ffn, w_in, hg_lb_logits, hg_out_norm, attn_sink, w_branch_a, w_branch_b, w_out,
           w_router, w_exp_gate, w_exp_up, w_exp_down):
    D, E = cfg.D, cfg.E
    row = lambda v: v.reshape(1, -1)
    mod = _ada(c_pad, w_ada[layer], row(b_ada[layer]))[:cfg.B]
    mod3 = mod.reshape(cfg.B, 6, D)
    h = _prenorm(cfg, x2, row(g_pre_mix[layer]), mod3)
    proj = _inproj(h, w_in[layer])
    q_rot, k_rot = _rope(cfg, proj, pos2, inv, sgn)
    att = _attn(cfg, q_rot, k_rot, proj, attn_sink[layer])
    gain3 = hg_out_norm[layer].reshape(cfg.HH // HGRN_HEADS_PER_STEP, 1, HGRN_HEADS_PER_STEP * HEAD)
    oa = _hgrn(cfg, proj, hg_lb_logits, gain3, layer)
    merged, w_out_bf = _merge(cfg, oa, att, proj, w_branch_a[layer], w_branch_b[layer], w_out[layer])
    wr_pad = jnp.pad(w_router[layer], ((0, 0), (0, LANES - E)))
    x1, h2, aff_tm = _outproj(cfg, merged, w_out_bf, x2, mod3, row(g_post_mix[layer]),
                              row(g_pre_ffn[layer]), wr_pad)
    posm, pos_tm, offs = _route(cfg, aff_tm)
    offs_flat = offs.reshape(-1)
    xe = _gather(cfg, offs_flat, posm.reshape(cfg.B * E, 1, cfg.S), h2)
    ye = _ffn(cfg, xe, w_exp_gate[layer], w_exp_up[layer], w_exp_down[layer])
    return _combine(cfg, offs_flat, pos_tm, aff_tm, ye, x1, mod3, row(g_post_ffn[layer]))


def _block(cfg, x, c, positions, w_ada, *rest):
    depth = w_ada.shape[0]
    x2 = x.reshape(cfg.T, cfg.D)
    c_pad = jnp.pad(c, ((0, 8 - cfg.B), (0, 0)))
    pos2 = positions.reshape(cfg.T, 1)
    half = HEAD // 2
    inv_half = ROPE_THETA ** (-jnp.arange(half, dtype=F32) / half)
    inv = jnp.concatenate([inv_half, inv_half]).reshape(1, HEAD)
    sgn = jnp.concatenate([-jnp.ones((half,), F32), jnp.ones((half,), F32)]).reshape(1, HEAD)
    for layer in range(depth):
        x2 = _layer(cfg, layer, x2, c_pad, pos2, inv, sgn, w_ada, *rest)
    return x2.reshape(cfg.B, cfg.S, cfg.D)


def kernel(x, c, positions, w_ada, b_ada, g_pre_mix, g_post_mix, g_pre_ffn, g_post_ffn, w_in,
           hg_lb_logits, hg_out_norm, attn_sink, w_branch_a, w_branch_b, w_out, w_router,
           w_exp_gate, w_exp_up, w_exp_down):
    B, S, D = x.shape
    cfg = Cfg(B=B, S=S, D=D,
              HH=hg_out_norm.shape[1], AH=attn_sink.shape[1],
              KVH=(w_in.shape[2] - 5 * hg_out_norm.shape[1] * HEAD - attn_sink.shape[1] * HEAD - 2 * D) // (2 * HEAD),
              E=w_router.shape[2], FF=w_exp_gate.shape[3])
    return _block(cfg, x, c, positions, w_ada, b_ada, g_pre_mix, g_post_mix, g_pre_ffn, g_post_ffn,
                  w_in, hg_lb_logits, hg_out_norm, attn_sink, w_branch_a, w_branch_b, w_out,
                  w_router, w_exp_gate, w_exp_up, w_exp_down)
```
